```python
import jax, jax.numpy as jnp
from jax import lax
import numpy as np

D_MODEL = 2048
BATCH = 32
SEQ = 256
DEPTH = 2
DEC_BATCH = 4
DEC_SEQ = 4096
PAST_LEN = 512

GRID_W = 64
M_HEADS = 8
M_DK = 64
M_DV = 128
CHUNK = 64
IGATE_CAP = 15.0
MH_EPS = 1e-6
NEG_INIT = -1e30
A_HEADS = 8
A_KV_HEADS = 2
A_GROUP = A_HEADS // A_KV_HEADS
A_HD = 128
WINDOW = 128
Q_BLOCK = 128
ROPE_BASE = 10000.0
MIX_W = M_HEADS * M_DV + A_HEADS * A_HD
IN_SIZES = (M_HEADS * M_DK, M_HEADS * M_DK, M_HEADS * M_DV, M_HEADS * M_DV, 4 * M_HEADS,
            A_HEADS * A_HD, A_KV_HEADS * A_HD, A_KV_HEADS * A_HD)
IN_COLS = sum(IN_SIZES)
SPLIT_POINTS = tuple(int(s) for s in np.cumsum(IN_SIZES)[:-1])
N_EXPERTS = 16
N_GROUPS = 4
EXPERTS_PER_GROUP = N_EXPERTS // N_GROUPS
TOP_K = 2
D_EXPERT = 512
ALPHA = (2 * DEPTH) ** 0.25
BETA = (8 * DEPTH) ** -0.25
LN_EPS = 1e-5

kernel_name = "hybrid_mlstm_swa_moe_diffusion_step"


def layer_norm_plain(x):
    xf = x.astype(jnp.float32)
    mu = jnp.mean(xf, -1, keepdims=True)
    xc = xf - mu
    return xc * lax.rsqrt(jnp.mean(xc * xc, -1, keepdims=True) + LN_EPS)


def layer_norm(x, g, b):
    y = layer_norm_plain(x) * g.astype(jnp.float32) + b.astype(jnp.float32)
    return y.astype(x.dtype)


def modulate(x, shift, scale):
    return (layer_norm_plain(x) * (1.0 + scale) + shift).astype(x.dtype)


def axial_rope(x, rows):
    half = A_HD // 2
    row = jnp.repeat(jnp.arange(rows), GRID_W).astype(jnp.float32)
    col = jnp.tile(jnp.arange(GRID_W), rows).astype(jnp.float32)
    inv = ROPE_BASE ** (-jnp.arange(0, half, 2, dtype=jnp.float32) / half)

    def rot(xa, pos):
        ang = pos[:, None] * inv[None, :]
        cos = jnp.cos(ang)[None, :, None, :]
        sin = jnp.sin(ang)[None, :, None, :]
        x1, x2 = xa[..., : half // 2], xa[..., half // 2:]
        return jnp.concatenate([x1 * cos - x2 * sin, x2 * cos + x1 * sin], -1)

    return jnp.concatenate([rot(x[..., :half], row), rot(x[..., half:], col)], -1)


def mlstm_scan(q, k, v, ig, lf, C0, n0, m0):
    B, S, H, _ = q.shape
    nc = S // CHUNK

    def to_chunks(a):
        a = a.reshape((B, nc, CHUNK, H) + a.shape[3:])
        return jnp.moveaxis(a, 2, 3).swapaxes(0, 1)

    tril = jnp.tril(jnp.ones((CHUNK, CHUNK), dtype=bool))

    def step(carry, xs):
        C, n, m = carry
        qc, kc, vc, ic, fc = xs
        b = jnp.cumsum(fc, axis=-1)
        g = b[..., -1]
        dmat = jnp.where(tril, b[..., :, None] - b[..., None, :] + ic[..., None, :], -jnp.inf)
        inter = b + m[..., None]
        m_s = jnp.maximum(inter, jnp.max(dmat, -1))
        w_intra = jnp.exp(dmat - m_s[..., None])
        w_inter = jnp.exp(inter - m_s)
        s_qk = jnp.einsum('bhsd,bhrd->bhsr', qc, kc) * w_intra
        num = jnp.einsum('bhsr,bhrv->bhsv', s_qk, vc) + w_inter[..., None] * jnp.einsum('bhsd,bhdv->bhsv', qc, C)
        den = jnp.sum(s_qk, -1) + w_inter * jnp.einsum('bhsd,bhd->bhs', qc, n)
        h = num / jnp.maximum(jnp.abs(den), jnp.exp(-m_s))[..., None]
        a = g[..., None] - b + ic
        m_new = jnp.maximum(g + m, jnp.max(a, -1))
        w_a = jnp.exp(a - m_new[..., None])
        w_c = jnp.exp(g + m - m_new)
        C_new = w_c[..., None, None] * C + jnp.einsum('bhr,bhrd,bhrv->bhdv', w_a, kc, vc)
        n_new = w_c[..., None] * n + jnp.einsum('bhr,bhrd->bhd', w_a, kc)
        return (C_new, n_new, m_new), h

    xs = (to_chunks(q), to_chunks(k), to_chunks(v), to_chunks(ig), to_chunks(lf))
    (C, n, m), hs = lax.scan(step, (C0, n0, m0), xs)
    hs = jnp.moveaxis(hs.swapaxes(0, 1), 3, 2).reshape(B, S, H, -1)
    return hs, C, n, m


def context_attention(q, k, v, sink):
    B, S = q.shape[:2]
    nb = S // Q_BLOCK
    qb = q.reshape(B, nb, Q_BLOCK, A_KV_HEADS, A_GROUP, A_HD).swapaxes(0, 1)
    sink_b = jnp.broadcast_to(sink.reshape(1, A_KV_HEADS, A_GROUP, 1, 1), (B, A_KV_HEADS, A_GROUP, Q_BLOCK, 1))
    scale = A_HD ** -0.5

    def one_block(qblk):
        s = jnp.einsum('bqkgd,bskd->bkgqs', qblk, k) * scale
        p = jax.nn.softmax(jnp.concatenate([s, sink_b], -1), axis=-1)
        return jnp.einsum('bkgqs,bskd->bqkgd', p[..., :-1], v)

    out = lax.map(one_block, qb)
    return out.swapaxes(0, 1).reshape(B, S, A_HEADS * A_HD)


def latent_attention(q, k, v, ck, cv, sink):
    B, S = q.shape[:2]
    P = ck.shape[1]
    nb = S // Q_BLOCK
    span = Q_BLOCK + 2 * WINDOW
    pad = ((0, 0), (WINDOW, WINDOW), (0, 0), (0, 0))
    kp = jnp.pad(k, pad)
    vp = jnp.pad(v, pad)
    qb = q.reshape(B, nb, Q_BLOCK, A_KV_HEADS, A_GROUP, A_HD).swapaxes(0, 1)
    sink_b = jnp.broadcast_to(sink.reshape(1, A_KV_HEADS, A_GROUP, 1, 1), (B, A_KV_HEADS, A_GROUP, Q_BLOCK, 1))
    scale = A_HD ** -0.5

    def one_block(args):
        qblk, blk = args
        start = blk * Q_BLOCK
        kb = lax.dynamic_slice_in_dim(kp, start, span, axis=1)
        vb = lax.dynamic_slice_in_dim(vp, start, span, axis=1)
        qpos = start + jnp.arange(Q_BLOCK)
        kpos = start - WINDOW + jnp.arange(span)
        valid = (jnp.abs(qpos[:, None] - kpos[None, :]) <= WINDOW) & (kpos >= 0)[None, :] & (kpos < S)[None, :]
        s_lat = jnp.where(valid, jnp.einsum('bqkgd,bskd->bkgqs', qblk, kb) * scale, -jnp.inf)
        s_ctx = jnp.einsum('bqkgd,bckd->bkgqc', qblk, ck) * scale
        p = jax.nn.softmax(jnp.concatenate([s_lat, s_ctx, sink_b], -1), axis=-1)
        return (jnp.einsum('bkgqs,bskd->bqkgd', p[..., :span], vb)
                + jnp.einsum('bkgqc,bckd->bqkgd', p[..., span:span + P], cv))

    out = lax.map(one_block, (qb, jnp.arange(nb)))
    return out.swapaxes(0, 1).reshape(B, S, A_HEADS * A_HD)


def mixer(h, w_in_l, b_gate_l, mh_g_l, sink_l, w_out_l, rows, ctx_k, ctx_v, state0):
    f32 = jnp.float32
    B, S, _ = h.shape
    proj = h @ w_in_l
    qm, km, vm, om, gt, qa, ka, va = jnp.split(proj, SPLIT_POINTS, axis=-1)
    qm = qm.astype(f32).reshape(B, S, M_HEADS, M_DK)
    km = km.astype(f32).reshape(B, S, M_HEADS, M_DK) * (M_DK ** -0.5)
    vm = vm.astype(f32).reshape(B, S, M_HEADS, M_DV)
    gt = gt.astype(f32).reshape(B, S, 4, M_HEADS) + b_gate_l.astype(f32).reshape(4, M_HEADS)
    ig = IGATE_CAP * jnp.tanh(gt[:, :, :2] / IGATE_CAP)
    lf = jax.nn.log_sigmoid(gt[:, :, 2:])
    if state0 is None:
        C0 = jnp.zeros((B, 2, M_HEADS, M_DK, M_DV), f32)
        n0 = jnp.zeros((B, 2, M_HEADS, M_DK), f32)
        m0 = jnp.full((B, 2, M_HEADS), NEG_INIT, f32)
    else:
        C0, n0, m0 = (a.astype(f32) for a in state0)
    h_f, C_f, n_f, m_f = mlstm_scan(qm, km, vm, ig[:, :, 0], lf[:, :, 0], C0[:, 0], n0[:, 0], m0[:, 0])
    h_b, C_b, n_b, m_b = mlstm_scan(qm[:, ::-1], km[:, ::-1], vm[:, ::-1], ig[:, ::-1, 1], lf[:, ::-1, 1],
                                    C0[:, 1], n0[:, 1], m0[:, 1])
    hm = h_f + h_b[:, ::-1]
    hm = hm * lax.rsqrt(jnp.mean(hm * hm, -1, keepdims=True) + MH_EPS) * mh_g_l.astype(f32).reshape(M_HEADS, M_DV)
    hm = hm.reshape(B, S, M_HEADS * M_DV) * jax.nn.sigmoid(om.astype(f32))
    qa = qa.astype(f32).reshape(B, S, A_HEADS, A_HD)
    ka_h = ka.reshape(B, S, A_KV_HEADS, A_HD)
    va_h = va.reshape(B, S, A_KV_HEADS, A_HD)
    sink = sink_l.astype(f32).reshape(A_KV_HEADS, A_GROUP)
    if ctx_k is None:
        attn = context_attention(qa.reshape(B, S, A_KV_HEADS, A_GROUP, A_HD), ka_h.astype(f32), va_h.astype(f32), sink)
    else:
        qr = axial_rope(qa, rows).reshape(B, S, A_KV_HEADS, A_GROUP, A_HD)
        kr = axial_rope(ka_h.astype(f32), rows)
        attn = latent_attention(qr, kr, va_h.astype(f32), ctx_k.astype(f32), ctx_v.astype(f32), sink)
    out = jnp.concatenate([hm, attn], -1).astype(h.dtype) @ w_out_l
    ctx_state = (ka_h, va_h, jnp.stack([C_f, C_b], 1), jnp.stack([n_f, n_b], 1), jnp.stack([m_f, m_b], 1))
    return out, ctx_state


def moe(h, w_router, b_router, wg, wu, wd):
    B, S, D = h.shape
    t = h.reshape(B * S, D)
    T = t.shape[0]
    scores = jax.nn.sigmoid((t @ w_router).astype(jnp.float32))
    sel = scores + b_router.astype(jnp.float32)
    grp_score = jnp.sum(lax.top_k(sel.reshape(T, N_GROUPS, EXPERTS_PER_GROUP), 2)[0], -1)
    best = jnp.argmax(grp_score, -1)
    expert_group = jnp.arange(N_EXPERTS) // EXPERTS_PER_GROUP
    sel_m = jnp.where(expert_group[None, :] == best[:, None], sel, -jnp.inf)
    _, idx = lax.top_k(sel_m, TOP_K)
    w = jnp.take_along_axis(scores, idx, -1)
    w = w / jnp.sum(w, -1, keepdims=True)
    gate = jnp.sum(jax.nn.one_hot(idx, N_EXPERTS, dtype=jnp.float32) * w[..., None], 1)

    def expert(y, xs):
        wg_e, wu_e, wd_e, g_e = xs
        a = jax.nn.silu(t @ wg_e) * (t @ wu_e)
        return y + g_e[:, None] * (a @ wd_e).astype(jnp.float32), None

    y, _ = lax.scan(expert, jnp.zeros((T, D), jnp.float32), (wg, wu, wd, gate.T))
    return y.reshape(B, S, D).astype(h.dtype)


def layer(x, cond, l, w_ada, b_ada, w_in, b_gate, mh_norm_g, attn_sink, w_out, ln1_g, ln1_b, ln2_g, ln2_b,
          w_router, b_router, w_exp_gate, w_exp_up, w_exp_down, rows, ctx_k, ctx_v, state0):
    mod = jax.nn.silu(cond) @ w_ada[l] + b_ada[l]
    mod = mod.astype(jnp.float32).reshape(mod.shape[0], 1, 6, D_MODEL)
    sh1, sc1, g1, sh2, sc2, g2 = (mod[:, :, i] for i in range(6))
    f, ctx_state = mixer(modulate(x, sh1, sc1), w_in[l], b_gate[l], mh_norm_g[l], attn_sink[l], w_out[l],
                         rows, ctx_k, ctx_v, state0)
    x = layer_norm(ALPHA * x + (g1 * f).astype(x.dtype), ln1_g[l], ln1_b[l])
    f = moe(modulate(x, sh2, sc2), w_router, b_router, w_exp_gate[l], w_exp_up[l], w_exp_down[l])
    x = layer_norm(ALPHA * x + (g2 * f).astype(x.dtype), ln2_g[l], ln2_b[l])
    return x, ctx_state


def setup_inputs(seed: int = 0) -> dict:
    key = jax.random.key(seed)
    ks = jax.random.split(key, 32)
    f32 = jnp.float32

    def nrm(k, shape, s):
        return s * jax.random.normal(k, shape, f32)

    b_i = nrm(ks[12], (DEPTH, 2 * M_HEADS), 0.1)
    b_f = jnp.tile(jnp.linspace(3.0, 6.0, M_HEADS, dtype=f32), (DEPTH, 2)) + nrm(ks[13], (DEPTH, 2 * M_HEADS), 0.1)
    return {
        "x_prompt": nrm(ks[0], (BATCH, SEQ, D_MODEL), 1.0),
        "x_sample": nrm(ks[1], (DEC_BATCH, DEC_SEQ, D_MODEL), 1.0),
        "cache_k": nrm(ks[3], (DEC_BATCH, DEPTH, PAST_LEN, A_KV_HEADS, A_HD), 1.0),
        "cache_v": nrm(ks[4], (DEC_BATCH, DEPTH, PAST_LEN, A_KV_HEADS, A_HD), 1.0),
        "state_C": nrm(ks[5], (DEC_BATCH, DEPTH, 2, M_HEADS, M_DK, M_DV), 0.1),
        "state_n": nrm(ks[6], (DEC_BATCH, DEPTH, 2, M_HEADS, M_DK), 0.1),
        "state_m": nrm(ks[7], (DEC_BATCH, DEPTH, 2, M_HEADS), 1.0),
        "c": nrm(ks[2], (DEC_BATCH, D_MODEL), 1.0),
        "c_ctx": nrm(ks[8], (D_MODEL,), 1.0),
        "w_ada": nrm(ks[9], (DEPTH, D_MODEL, 6 * D_MODEL), 0.5 * D_MODEL ** -0.5),
        "b_ada": nrm(ks[10], (DEPTH, 6 * D_MODEL), 0.02),
        "w_in": nrm(ks[11], (DEPTH, D_MODEL, IN_COLS), D_MODEL ** -0.5),
        "b_gate": jnp.concatenate([b_i, b_f], -1),
        "mh_norm_g": 1.0 + nrm(ks[14], (DEPTH, M_HEADS * M_DV), 0.02),
        "attn_sink": nrm(ks[15], (DEPTH, A_HEADS), 1.0),
        "w_out": nrm(ks[16], (DEPTH, MIX_W, D_MODEL), BETA * MIX_W ** -0.5),
        "ln1_g": 1.0 + nrm(ks[17], (DEPTH, D_MODEL), 0.02),
        "ln1_b": nrm(ks[18], (DEPTH, D_MODEL), 0.02),
        "ln2_g": 1.0 + nrm(ks[19], (DEPTH, D_MODEL), 0.02),
        "ln2_b": nrm(ks[20], (DEPTH, D_MODEL), 0.02),
        "w_router": nrm(ks[21], (D_MODEL, N_EXPERTS), D_MODEL ** -0.5),
        "b_router": nrm(ks[22], (N_EXPERTS,), 0.01),
        "w_exp_gate": nrm(ks[23], (DEPTH, N_EXPERTS, D_MODEL, D_EXPERT), D_MODEL ** -0.5),
        "w_exp_up": nrm(ks[24], (DEPTH, N_EXPERTS, D_MODEL, D_EXPERT), D_MODEL ** -0.5),
        "w_exp_down": nrm(ks[25], (DEPTH, N_EXPERTS, D_EXPERT, D_MODEL), BETA * D_EXPERT ** -0.5),
    }


def reference(x_prompt, x_sample, cache_k, cache_v, state_C, state_n, state_m, c, c_ctx,
              w_ada, b_ada, w_in, b_gate, mh_norm_g, attn_sink, w_out, ln1_g, ln1_b, ln2_g, ln2_b,
              w_router, b_router, w_exp_gate, w_exp_up, w_exp_down):
    x = x_prompt
    ks_, vs_, Cs, ns, ms = [], [], [], [], []
    for l in range(DEPTH):
        x, (k_l, v_l, C_l, n_l, m_l) = layer(
            x, c_ctx[None, :], l, w_ada, b_ada, w_in, b_gate, mh_norm_g, attn_sink, w_out,
            ln1_g, ln1_b, ln2_g, ln2_b, w_router, b_router, w_exp_gate, w_exp_up, w_exp_down,
            None, None, None, None)
        ks_.append(k_l); vs_.append(v_l); Cs.append(C_l); ns.append(n_l); ms.append(m_l)
    y_prompt = x
    new_cache_k = jnp.stack(ks_, 1)
    new_cache_v = jnp.stack(vs_, 1)
    new_state_C = jnp.stack(Cs, 1)
    new_state_n = jnp.stack(ns, 1)
    new_state_m = jnp.stack(ms, 1)
    rows = x_sample.shape[1] // GRID_W
    x = x_sample
    for l in range(DEPTH):
        x, _ = layer(
            x, c, l, w_ada, b_ada, w_in, b_gate, mh_norm_g, attn_sink, w_out,
            ln1_g, ln1_b, ln2_g, ln2_b, w_router, b_router, w_exp_gate, w_exp_up, w_exp_down,
            rows, cache_k[:, l], cache_v[:, l], (state_C[:, l], state_n[:, l], state_m[:, l]))
    y_sample = x
    return (y_prompt, y_sample, new_cache_k, new_cache_v, new_state_C, new_state_n, new_state_m)
```

```python
import functools

import jax
import jax.numpy as jnp
import numpy as np
from jax import lax
from jax.experimental import pallas as pl
from jax.experimental.pallas import tpu as pltpu

F32 = jnp.float32
BF16 = jnp.bfloat16

M_HEADS = 8
M_DK = 64
M_DV = 128
IGATE_CAP = 15.0
MH_EPS = 1e-6
NEG_INIT = -1e30
A_HEADS = 8
A_KV_HEADS = 2
A_GROUP = A_HEADS // A_KV_HEADS
A_HD = 128
WINDOW = 128
GRID_W = 64
ROPE_BASE = 10000.0
N_EXPERTS = 16
N_GROUPS = 4
EXPERTS_PER_GROUP = N_EXPERTS // N_GROUPS
LN_EPS = 1e-5

LANES = 128
VMEM_LIMIT = 56 * 1024 * 1024

QM_OFF = 0
KM_OFF = M_HEADS * M_DK
VM_OFF = KM_OFF + M_HEADS * M_DK
OM_OFF = VM_OFF + M_HEADS * M_DV
QA_OFF = OM_OFF + M_HEADS * M_DV
KA_OFF = QA_OFF + A_HEADS * A_HD
VA_OFF = KA_OFF + A_KV_HEADS * A_HD
MAIN_COLS = VA_OFF + A_KV_HEADS * A_HD
GATE_SRC_OFF = OM_OFF + M_HEADS * M_DV
N_PAIRS = M_HEADS // 2
PAIR_ROWS = 2 * M_DK
CHUNK = 128
N_BUCKETS = N_GROUPS * 6
PAIR_LO = (0, 0, 0, 1, 1, 2)
PAIR_HI = (1, 2, 3, 2, 3, 3)


def _cparams(sem):
    return pltpu.CompilerParams(dimension_semantics=sem, vmem_limit_bytes=VMEM_LIMIT)


def _ln_plain(x):
    mu = jnp.mean(x, axis=-1, keepdims=True)
    xc = x - mu
    return xc * lax.rsqrt(jnp.mean(xc * xc, axis=-1, keepdims=True) + LN_EPS)


def _ada_kernel(c_ref, w_ref, b_ref, o_ref):
    c = c_ref[...]
    s = (c * jax.nn.sigmoid(c)).astype(BF16)
    o_ref[0] = jnp.dot(s, w_ref[0].astype(BF16), preferred_element_type=F32) + b_ref[0]


def _ada_mod(cond, w_ada, b_ada):
    depth, d, n = w_ada.shape
    rows = cond.shape[0]
    tn = 1024
    return pl.pallas_call(
        _ada_kernel,
        grid=(depth, n // tn),
        in_specs=[
            pl.BlockSpec((rows, d), lambda l, j: (0, 0)),
            pl.BlockSpec((1, d, tn), lambda l, j: (l, 0, j)),
            pl.BlockSpec((1, 1, tn), lambda l, j: (l, 0, j)),
        ],
        out_specs=pl.BlockSpec((1, rows, tn), lambda l, j: (l, 0, j)),
        out_shape=jax.ShapeDtypeStruct((depth, rows, n), F32),
        compiler_params=_cparams(("parallel", "parallel")),
        name="ada_mod",
    )(cond, w_ada, b_ada.reshape(depth, 1, n))


def _inproj_kernel(x_ref, mod_ref, w_ref, wg_ref, bg_ref, proj_ref, gates_ref, kv_ref, h_scr, *, nj):
    j = pl.program_id(1)

    @pl.when(j == 0)
    def _():
        hn = _ln_plain(x_ref[...])
        h = hn * (1.0 + mod_ref[0, 1:2, :]) + mod_ref[0, 0:1, :]
        hb = h.astype(BF16)
        h_scr[...] = hb
        gates_ref[...] = jnp.dot(hb, wg_ref[...], preferred_element_type=F32) + bg_ref[...]

    acc = jnp.dot(h_scr[...], w_ref[...], preferred_element_type=F32)
    proj_ref[...] = acc.astype(BF16)

    @pl.when(j == nj - 1)
    def _():
        kv_ref[...] = acc


def _inproj(x, modl, cond_of_tile, w_main, w_gate, b_gate, tm):
    t, d = x.shape
    tn = 512
    nj = MAIN_COLS // tn
    assert KA_OFF == (nj - 1) * tn
    return pl.pallas_call(
        functools.partial(_inproj_kernel, nj=nj),
        grid=(t // tm, nj),
        in_specs=[
            pl.BlockSpec((tm, d), lambda i, j: (i, 0)),
            pl.BlockSpec((1, 6, d), lambda i, j: (cond_of_tile(i), 0, 0)),
            pl.BlockSpec((d, tn), lambda i, j: (0, j)),
            pl.BlockSpec((d, 2 * LANES), lambda i, j: (0, 0)),
            pl.BlockSpec((1, 2 * LANES), lambda i, j: (0, 0)),
        ],
        out_specs=[
            pl.BlockSpec((tm, tn), lambda i, j: (i, j)),
            pl.BlockSpec((tm, 2 * LANES), lambda i, j: (i, 0)),
            pl.BlockSpec((tm, tn), lambda i, j: (i, 0)),
        ],
        out_shape=[
            jax.ShapeDtypeStruct((t, MAIN_COLS), BF16),
            jax.ShapeDtypeStruct((t, 2 * LANES), F32),
            jax.ShapeDtypeStruct((t, tn), F32),
        ],
        scratch_shapes=[pltpu.VMEM((tm, d), BF16)],
        compiler_params=_cparams(("parallel", "arbitrary")),
        name="inproj",
    )(x, modl, w_main, w_gate, b_gate)


def _gate_kernel(g_ref, gb_ref, gu_ref, gut_ref, *, tg):
    gi = g_ref[:, :LANES]
    gf = g_ref[:, LANES:]
    ig = IGATE_CAP * jnp.tanh(gi / IGATE_CAP)
    lf = jax.nn.log_sigmoid(gf)
    lane = lax.broadcasted_iota(jnp.int32, (1, LANES), 1)
    is_fwd = (lane % 8) < 2
    s_i = lax.broadcasted_iota(jnp.int32, (CHUNK, CHUNK), 0)
    r_i = lax.broadcasted_iota(jnp.int32, (CHUNK, CHUNK), 1)
    tri_lo = (r_i <= s_i).astype(F32)
    tri_hi = (r_i >= s_i).astype(F32)
    for c in range(tg // CHUNK):
        rows = slice(c * CHUNK, (c + 1) * CHUNK)
        lfc = lf[rows]
        pre = jnp.dot(tri_lo, lfc, preferred_element_type=F32, precision=lax.Precision.HIGHEST)
        suf = jnp.dot(tri_hi, lfc, preferred_element_type=F32, precision=lax.Precision.HIGHEST)
        b = jnp.where(is_fwd, pre, suf)
        u = ig[rows] - b
        gb_ref[rows, :] = b
        gu_ref[rows, :] = u
        gut_ref[:, rows] = u.T[: 8 * N_PAIRS]


def _gate_prep(gates, tg):
    t = gates.shape[0]
    return pl.pallas_call(
        functools.partial(_gate_kernel, tg=tg),
        grid=(t // tg,),
        in_specs=[pl.BlockSpec((tg, 2 * LANES), lambda i: (i, 0))],
        out_specs=[
            pl.BlockSpec((tg, LANES), lambda i: (i, 0)),
            pl.BlockSpec((tg, LANES), lambda i: (i, 0)),
            pl.BlockSpec((8 * N_PAIRS, tg), lambda i: (0, i)),
        ],
        out_shape=[
            jax.ShapeDtypeStruct((t, LANES), F32),
            jax.ShapeDtypeStruct((t, LANES), F32),
            jax.ShapeDtypeStruct((8 * N_PAIRS, t), F32),
        ],
        compiler_params=_cparams(("parallel",)),
        name="gate_prep",
    )(gates)


def _mlstm_kernel(*refs, seq, has_state, emit_state):
    it = iter(refs)
    q_ref, k_ref, v_ref, om_ref, gb_ref, gu_ref, gut_ref, gain_ref = (next(it) for _ in range(8))
    if has_state:
        c0_ref, n0_ref, m0_ref = next(it), next(it), next(it)
    out_ref = next(it)
    if emit_state:
        cout_ref, nout_ref, mout_ref = next(it), next(it), next(it)
    hf_scr, cst_scr = next(it), next(it)

    b_id = pl.program_id(0)
    hp = pl.program_id(1)
    nc = seq // CHUNK
    L = CHUNK

    lane = lax.broadcasted_iota(jnp.int32, (1, LANES), 1)
    row128 = lax.broadcasted_iota(jnp.int32, (PAIR_ROWS, 1), 0)
    s_i = lax.broadcasted_iota(jnp.int32, (L, L), 0)
    r_i = lax.broadcasted_iota(jnp.int32, (L, L), 1)
    e0row = (lane == 0).astype(F32)
    e0blk = jnp.broadcast_to(e0row, (L, LANES)).astype(BF16)

    def pick_lane(x, j):
        return jnp.sum(jnp.where(lane == j, x, 0.0), axis=1, keepdims=True)

    for d in range(2):
        tri = (r_i <= s_i) if d == 0 else (r_i >= s_i)
        m_init = []
        for hh in range(2):
            if has_state:
                rowmask = (row128 // M_DK) == hh
                cst_scr[hh, :, :LANES] = jnp.where(rowmask, c0_ref[0, d, 0], 0.0)
                cst_scr[hh, :, LANES:] = jnp.where(rowmask, n0_ref[0, d, 0], 0.0) * e0row
                m0 = m0_ref[b_id * (2 * M_HEADS) + d * M_HEADS + hp * 2 + hh]
                m_init.append(jnp.full((1, 1), m0, F32))
            else:
                cst_scr[hh] = jnp.zeros((PAIR_ROWS, 2 * LANES), F32)
                m_init.append(jnp.full((1, 1), NEG_INIT, F32))

        def body(t, ms, d=d, tri=tri):
            c = t if d == 0 else nc - 1 - t
            r0 = pl.multiple_of(c * L, L)
            q2 = q_ref[pl.ds(r0, L), :]
            k2 = k_ref[pl.ds(r0, L), :]
            gbc = gb_ref[pl.ds(r0, L), :]
            guc = gu_ref[pl.ds(r0, L), :]
            gend = gb_ref[pl.ds(r0 + (L - 1 if d == 0 else 0), 1), :]
            new_ms = []
            for hh in range(2):
                jl = hp * 8 + d * 2 + hh
                lm = (lane // M_DK) == hh
                qh = jnp.where(lm, q2, jnp.zeros_like(q2))
                kh = jnp.where(lm, k2, jnp.zeros_like(k2))
                vh = v_ref[pl.ds(r0, L), hh * LANES:(hh + 1) * LANES]
                vext = jnp.concatenate([vh, e0blk], axis=1)
                urow = gut_ref[d * 2 + hh:d * 2 + hh + 1, pl.ds(r0, L)]
                ucol = pick_lane(guc, jl)
                bcol = pick_lane(gbc, jl)
                g = pick_lane(gend, jl)
                m = ms[hh]
                umat = jnp.where(tri, urow, -jnp.inf)
                cmu = jnp.max(umat, axis=1, keepdims=True)
                mm = jnp.maximum(m, cmu)
                w = jnp.exp(umat - mm)
                sqk = lax.dot_general(qh, kh, (((1,), (1,)), ((), ())), preferred_element_type=F32)
                p = (sqk * w).astype(BF16)
                intra = jnp.dot(p, vext, preferred_element_type=F32)
                cs = cst_scr[hh]
                inter = jnp.dot(qh, cs.astype(BF16), preferred_element_type=F32)
                nd = intra + jnp.exp(m - mm) * inter
                num = nd[:, :LANES]
                den = nd[:, LANES:LANES + 1]
                hv = num / jnp.maximum(jnp.abs(den), jnp.exp(-bcol - mm))
                cols = slice(hh * LANES, (hh + 1) * LANES)
                if d == 0:
                    hf_scr[pl.ds(r0, L), cols] = hv
                else:
                    tot = hf_scr[pl.ds(r0, L), cols] + hv
                    ms2 = jnp.mean(tot * tot, axis=1, keepdims=True)
                    y = tot * lax.rsqrt(ms2 + MH_EPS) * gain_ref[:, cols]
                    y = y * jax.nn.sigmoid(om_ref[pl.ds(r0, L), cols].astype(F32))
                    out_ref[pl.ds(r0, L), cols] = y.astype(BF16)
                maxu = jnp.max(urow, axis=1, keepdims=True)
                m_new = g + jnp.maximum(m, maxu)
                wa = jnp.exp(g + ucol - m_new)
                wc = jnp.exp(g + m - m_new)
                kw = (kh.astype(F32) * wa).astype(BF16)
                upd = lax.dot_general(kw, vext, (((0,), (0,)), ((), ())), preferred_element_type=F32)
                cst_scr[hh] = wc * cs + upd
                new_ms.append(m_new)
            return tuple(new_ms)

        ms_fin = lax.fori_loop(0, nc, body, tuple(m_init))

        if emit_state:
            cout_ref[0, d, 0] = cst_scr[0, :, :LANES] + cst_scr[1, :, :LANES]
            nout_ref[0, d, 0] = cst_scr[0, :, LANES:LANES + 1] + cst_scr[1, :, LANES:LANES + 1]
            for hh in range(2):
                mout_ref[0, 0, d * 2 + hh:d * 2 + hh + 1, :] = jnp.broadcast_to(ms_fin[hh], (1, LANES))

    if emit_state:
        mout_ref[0, 0, 4:8, :] = jnp.zeros((4, LANES), F32)


def _mlstm(proj, gb, gu, gut, gain, row_off, nseq, seq, state0, emit_state):
    assert row_off % seq == 0
    rb = row_off // seq
    has_state = state0 is not None
    kernel = functools.partial(_mlstm_kernel, seq=seq, has_state=has_state, emit_state=emit_state)
    in_specs = [
        pl.BlockSpec((seq, PAIR_ROWS), lambda b, h: (rb + b, QM_OFF // PAIR_ROWS + h)),
        pl.BlockSpec((seq, PAIR_ROWS), lambda b, h: (rb + b, KM_OFF // PAIR_ROWS + h)),
        pl.BlockSpec((seq, 2 * M_DV), lambda b, h: (rb + b, VM_OFF // (2 * M_DV) + h)),
        pl.BlockSpec((seq, 2 * M_DV), lambda b, h: (rb + b, OM_OFF // (2 * M_DV) + h)),
        pl.BlockSpec((seq, LANES), lambda b, h: (rb + b, 0)),
        pl.BlockSpec((seq, LANES), lambda b, h: (rb + b, 0)),
        pl.BlockSpec((8, seq), lambda b, h: (h, rb + b)),
        pl.BlockSpec((1, 2 * M_DV), lambda b, h: (0, h)),
    ]
    args = [proj, proj, proj, proj, gb, gu, gut, gain]
    if has_state:
        c0, n0, m0 = state0
        in_specs += [
            pl.BlockSpec((1, 2, 1, PAIR_ROWS, M_DV), lambda b, h: (b, 0, h, 0, 0)),
            pl.BlockSpec((1, 2, 1, PAIR_ROWS, 1), lambda b, h: (b, 0, h, 0, 0)),
            pl.BlockSpec(memory_space=pltpu.SMEM),
        ]
        args += [c0, n0, m0]
    out_specs = [pl.BlockSpec((seq, 2 * M_DV), lambda b, h: (b, h))]
    out_shape = [jax.ShapeDtypeStruct((nseq * seq, M_HEADS * M_DV), BF16)]
    if emit_state:
        out_specs += [
            pl.BlockSpec((1, 2, 1, PAIR_ROWS, M_DV), lambda b, h: (b, 0, h, 0, 0)),
            pl.BlockSpec((1, 2, 1, PAIR_ROWS, 1), lambda b, h: (b, 0, h, 0, 0)),
            pl.BlockSpec((1, 1, 8, LANES), lambda b, h: (b, h, 0, 0)),
        ]
        out_shape += [
            jax.ShapeDtypeStruct((nseq, 2, N_PAIRS, PAIR_ROWS, M_DV), F32),
            jax.ShapeDtypeStruct((nseq, 2, N_PAIRS, PAIR_ROWS, 1), F32),
            jax.ShapeDtypeStruct((nseq, N_PAIRS, 8, LANES), F32),
        ]
    return pl.pallas_call(
        kernel,
        grid=(nseq, N_PAIRS),
        in_specs=in_specs,
        out_specs=out_specs,
        out_shape=out_shape,
        scratch_shapes=[pltpu.VMEM((seq, 2 * M_DV), F32), pltpu.VMEM((2, PAIR_ROWS, 2 * LANES), F32)],
        compiler_params=_cparams(("parallel", "parallel")),
        name="mlstm",
    )(*args)


def _attn_ctx_kernel(sink_ref, q_ref, k_ref, v_ref, o_ref, *, seq, nb):
    kvh = pl.program_id(1)
    scale = A_HD ** -0.5
    for s in range(nb):
        rows = slice(s * seq, (s + 1) * seq)
        k = k_ref[rows, :]
        v = v_ref[rows, :]
        for g in range(A_GROUP):
            cols = slice(g * A_HD, (g + 1) * A_HD)
            q = q_ref[rows, cols]
            sc = lax.dot_general(q, k, (((1,), (1,)), ((), ())), preferred_element_type=F32) * scale
            sk = sink_ref[kvh * A_GROUP + g]
            mx = jnp.maximum(jnp.max(sc, axis=1, keepdims=True), sk)
            p = jnp.exp(sc - mx)
            den = jnp.sum(p, axis=1, keepdims=True) + jnp.exp(sk - mx)
            o = jnp.dot(p.astype(BF16), v, preferred_element_type=F32) / den
            o_ref[rows, cols] = o.astype(BF16)


def _attn_ctx(proj, sink, nseq, seq, nb):
    gw = A_GROUP * A_HD
    return pl.pallas_call(
        functools.partial(_attn_ctx_kernel, seq=seq, nb=nb),
        grid=(nseq // nb, A_KV_HEADS),
        in_specs=[
            pl.BlockSpec(memory_space=pltpu.SMEM),
            pl.BlockSpec((nb * seq, gw), lambda b, h: (b, QA_OFF // gw + h)),
            pl.BlockSpec((nb * seq, A_HD), lambda b, h: (b, KA_OFF // A_HD + h)),
            pl.BlockSpec((nb * seq, A_HD), lambda b, h: (b, VA_OFF // A_HD + h)),
        ],
        out_specs=pl.BlockSpec((nb * seq, gw), lambda b, h: (b, h)),
        out_shape=jax.ShapeDtypeStruct((nseq * seq, A_HEADS * A_HD), BF16),
        compiler_params=_cparams(("parallel", "parallel")),
        name="attn_ctx",
    )(sink, proj, proj, proj)


def _rope_kernel(q_ref, k_ref, cos_ref, sa_ref, sb_ref, qo_ref, ko_ref):
    cos = cos_ref[...]
    sa = sa_ref[...]
    sb = sb_ref[...]

    def rot(x):
        return x * cos + pltpu.roll(x, LANES - A_HD // 4, 1) * sa + pltpu.roll(x, A_HD // 4, 1) * sb

    for h in range(A_HEADS):
        cols = slice(h * A_HD, (h + 1) * A_HD)
        qo_ref[:, cols] = rot(q_ref[:, cols].astype(F32)).astype(BF16)
    for h in range(A_KV_HEADS):
        cols = slice(h * A_HD, (h + 1) * A_HD)
        ko_ref[:, cols] = rot(k_ref[:, cols].astype(F32)).astype(BF16)


def _rope(proj, tables, row_off, nrows, seq, tr):
    cos, sa, sb = tables
    rb = row_off // tr
    nps = seq // tr
    qw = A_HEADS * A_HD
    kw = A_KV_HEADS * A_HD
    tab = pl.BlockSpec((tr, A_HD), lambda i: (i % nps, 0))
    return pl.pallas_call(
        _rope_kernel,
        grid=(nrows // tr,),
        in_specs=[
            pl.BlockSpec((tr, qw), lambda i: (rb + i, QA_OFF // qw)),
            pl.BlockSpec((tr, kw), lambda i: (rb + i, KA_OFF // kw)),
            tab, tab, tab,
        ],
        out_specs=[pl.BlockSpec((tr, qw), lambda i: (i, 0)), pl.BlockSpec((tr, kw), lambda i: (i, 0))],
        out_shape=[jax.ShapeDtypeStruct((nrows, qw), BF16), jax.ShapeDtypeStruct((nrows, kw), BF16)],
        compiler_params=_cparams(("parallel",)),
        name="rope",
    )(proj, proj, cos, sa, sb)


def _rope_tables(seq):
    half = A_HD // 2
    pos = np.arange(seq)
    row = (pos // GRID_W).astype(np.float32)
    col = (pos % GRID_W).astype(np.float32)
    inv = (ROPE_BASE ** (-np.arange(0, half, 2, dtype=np.float32) / half)).astype(np.float32)
    ang_r = row[:, None] * inv[None, :]
    ang_c = col[:, None] * inv[None, :]
    ang = np.concatenate([ang_r, ang_r, ang_c, ang_c], axis=1).astype(np.float32)
    cos = np.cos(ang).astype(np.float32)
    sin = np.sin(ang).astype(np.float32)
    first = (np.arange(A_HD) % half) < (half // 2)
    sa = np.where(first[None, :], -sin, 0.0).astype(np.float32)
    sb = np.where(first[None, :], 0.0, sin).astype(np.float32)
    return jnp.asarray(cos), jnp.asarray(sa), jnp.asarray(sb)


def _attn_lat_kernel(sink_ref, q_ref, kp_ref, kc_ref, kn_ref, vp_ref, vc_ref, vn_ref, ck_ref, cv_ref, o_ref,
                     *, nblk):
    kvh = pl.program_id(1)
    i = pl.program_id(2)
    qb = WINDOW
    scale = A_HD ** -0.5
    q = jnp.concatenate([q_ref[:, g * A_HD:(g + 1) * A_HD] for g in range(A_GROUP)], axis=0)
    kwin = jnp.concatenate([kp_ref[...], kc_ref[...], kn_ref[...]], axis=0)
    vwin = jnp.concatenate([vp_ref[...], vc_ref[...], vn_ref[...]], axis=0)
    s_lat = lax.dot_general(q, kwin, (((1,), (1,)), ((), ())), preferred_element_type=F32) * scale
    s_ctx = lax.dot_general(q, ck_ref[0, 0], (((1,), (1,)), ((), ())), preferred_element_type=F32) * scale
    r = lax.broadcasted_iota(jnp.int32, (A_GROUP * qb, 3 * qb), 0) % qb
    c = lax.broadcasted_iota(jnp.int32, (A_GROUP * qb, 3 * qb), 1)
    c_lo = jnp.where(i > 0, 0, qb)
    c_hi = jnp.where(i < nblk - 1, 3 * qb, 2 * qb)
    valid = (c >= r) & (c <= r + 2 * WINDOW) & (c >= c_lo) & (c < c_hi)
    s_lat = jnp.where(valid, s_lat, NEG_INIT)
    hrow = lax.broadcasted_iota(jnp.int32, (A_GROUP * qb, 1), 0) // qb
    sk = jnp.zeros((A_GROUP * qb, 1), F32)
    for g in range(A_GROUP):
        sk = jnp.where(hrow == g, sink_ref[kvh * A_GROUP + g], sk)
    mx = jnp.maximum(jnp.maximum(jnp.max(s_lat, axis=1, keepdims=True), jnp.max(s_ctx, axis=1, keepdims=True)), sk)
    p_lat = jnp.exp(s_lat - mx)
    p_ctx = jnp.exp(s_ctx - mx)
    den = jnp.sum(p_lat, axis=1, keepdims=True) + jnp.sum(p_ctx, axis=1, keepdims=True) + jnp.exp(sk - mx)
    o = (jnp.dot(p_lat.astype(BF16), vwin, preferred_element_type=F32)
         + jnp.dot(p_ctx.astype(BF16), cv_ref[0, 0], preferred_element_type=F32)) / den
    for g in range(A_GROUP):
        o_ref[:, g * A_HD:(g + 1) * A_HD] = o[g * qb:(g + 1) * qb].astype(BF16)


def _attn_lat(qr, kr, proj, ck, cv, sink, row_off, nseq, seq):
    qb = WINDOW
    nblk = seq // qb
    gw = A_GROUP * A_HD
    rb = row_off // qb
    p_len = ck.shape[2]
    prev = lambda b, h, i: b * nblk + jnp.maximum(i - 1, 0)
    cur = lambda b, h, i: b * nblk + i
    nxt = lambda b, h, i: b * nblk + jnp.minimum(i + 1, nblk - 1)
    vcol = VA_OFF // A_HD
    return pl.pallas_call(
        functools.partial(_attn_lat_kernel, nblk=nblk),
        grid=(nseq, A_KV_HEADS, nblk),
        in_specs=[
            pl.BlockSpec(memory_space=pltpu.SMEM),
            pl.BlockSpec((qb, gw), lambda b, h, i: (cur(b, h, i), h)),
            pl.BlockSpec((qb, A_HD), lambda b, h, i: (prev(b, h, i), h)),
            pl.BlockSpec((qb, A_HD), lambda b, h, i: (cur(b, h, i), h)),
            pl.BlockSpec((qb, A_HD), lambda b, h, i: (nxt(b, h, i), h)),
            pl.BlockSpec((qb, A_HD), lambda b, h, i: (rb + prev(b, h, i), vcol + h)),
            pl.BlockSpec((qb, A_HD), lambda b, h, i: (rb + cur(b, h, i), vcol + h)),
            pl.BlockSpec((qb, A_HD), lambda b, h, i: (rb + nxt(b, h, i), vcol + h)),
            pl.BlockSpec((1, 1, p_len, A_HD), lambda b, h, i: (b, h, 0, 0)),
            pl.BlockSpec((1, 1, p_len, A_HD), lambda b, h, i: (b, h, 0, 0)),
        ],
        out_specs=pl.BlockSpec((qb, gw), lambda b, h, i: (cur(b, h, i), h)),
        out_shape=jax.ShapeDtypeStruct((nseq * seq, A_HEADS * A_HD), BF16),
        compiler_params=_cparams(("parallel", "parallel", "parallel")),
        name="attn_lat",
    )(sink, qr, kr, kr, kr, proj, proj, proj, ck, cv)


def _outproj_kernel(mm_ref, ma_ref, w_ref, x_ref, mod_ref, lng_ref, lnb_ref, wr_ref, br_ref,
                    x1_ref, h2_ref, bkt_ref, wts_ref, *, alpha):
    half = w_ref.shape[0] // 2
    f = (jnp.dot(mm_ref[...], w_ref[:half, :], preferred_element_type=F32)
         + jnp.dot(ma_ref[...], w_ref[half:, :], preferred_element_type=F32))
    z = alpha * x_ref[...] + mod_ref[0, 2:3, :] * f
    x1 = _ln_plain(z) * lng_ref[...] + lnb_ref[...]
    x1_ref[...] = x1
    h2 = (_ln_plain(x1) * (1.0 + mod_ref[0, 4:5, :]) + mod_ref[0, 3:4, :]).astype(BF16)
    bits = lax.bitcast_convert_type(h2.astype(F32), jnp.uint32)
    hd = bits.shape[1] // 2
    h2_ref[...] = (bits[:, hd:] & jnp.uint32(0xFFFF0000)) | (bits[:, :hd] >> 16)
    logits = jnp.dot(h2, wr_ref[...], preferred_element_type=F32)
    scores = jax.nn.sigmoid(logits)
    sel = scores + br_ref[...]
    sc_t = scores.T
    sel_t = sel.T
    sv = [sel_t[e:e + 1, :] for e in range(N_EXPERTS)]
    cv = [sc_t[e:e + 1, :] for e in range(N_EXPERTS)]
    gs = []
    for g in range(N_GROUPS):
        v = sv[4 * g:4 * g + 4]
        best = None
        for a, b in zip(PAIR_LO, PAIR_HI):
            ps = v[a] + v[b]
            best = ps if best is None else jnp.maximum(best, ps)
        gs.append(best)
    gmax = jnp.maximum(jnp.maximum(gs[0], gs[1]), jnp.maximum(gs[2], gs[3]))
    grp = jnp.full(gmax.shape, N_GROUPS - 1, jnp.int32)
    for g in range(N_GROUPS - 2, -1, -1):
        grp = jnp.where(gs[g] == gmax, g, grp)
    def pick(vals, k):
        out = vals[k]
        for g in range(1, N_GROUPS):
            out = jnp.where(grp == g, vals[4 * g + k], out)
        return out
    gv = [pick(sv, k) for k in range(EXPERTS_PER_GROUP)]
    gc = [pick(cv, k) for k in range(EXPERTS_PER_GROUP)]
    m1 = jnp.maximum(jnp.maximum(gv[0], gv[1]), jnp.maximum(gv[2], gv[3]))
    k0 = jnp.full(m1.shape, 3, jnp.int32)
    for k in range(2, -1, -1):
        k0 = jnp.where(gv[k] == m1, k, k0)
    gv2 = [jnp.where(k0 == k, -jnp.inf, gv[k]) for k in range(4)]
    m2 = jnp.maximum(jnp.maximum(gv2[0], gv2[1]), jnp.maximum(gv2[2], gv2[3]))
    k1 = jnp.full(m1.shape, 3, jnp.int32)
    for k in range(2, -1, -1):
        k1 = jnp.where(gv2[k] == m2, k, k1)
    lo = jnp.minimum(k0, k1)
    hi = jnp.maximum(k0, k1)
    pair = jnp.where(lo == 0, hi - 1, jnp.where(lo == 1, hi + 1, 5))
    bkt_ref[...] = grp * 6 + pair
    s_lo = jnp.where(lo == 0, gc[0], jnp.where(lo == 1, gc[1], gc[2]))
    s_hi = jnp.where(hi == 1, gc[1], jnp.where(hi == 2, gc[2], gc[3]))
    tot = s_lo + s_hi
    wts_ref[0:1, :] = s_lo / tot
    wts_ref[1:2, :] = s_hi / tot
    wts_ref[2:8, :] = jnp.zeros((6, tot.shape[1]), F32)


def _outproj(mix_m, mix_a, w_out, x, modl, cond_of_tile, ln_g, ln_b, w_router, b_router, alpha, tm):
    t, d = x.shape
    hw = mix_m.shape[1]
    row = lambda i: (i, 0)
    const = lambda i: (0, 0)
    return pl.pallas_call(
        functools.partial(_outproj_kernel, alpha=alpha),
        grid=(t // tm,),
        in_specs=[
            pl.BlockSpec((tm, hw), row),
            pl.BlockSpec((tm, hw), row),
            pl.BlockSpec((2 * hw, d), const),
            pl.BlockSpec((tm, d), row),
            pl.BlockSpec((1, 6, d), lambda i: (cond_of_tile(i), 0, 0)),
            pl.BlockSpec((1, d), const),
            pl.BlockSpec((1, d), const),
            pl.BlockSpec((d, LANES), const),
            pl.BlockSpec((1, LANES), const),
        ],
        out_specs=[
            pl.BlockSpec((tm, d), row),
            pl.BlockSpec((tm, d // 2), row),
            pl.BlockSpec((1, tm), lambda i: (0, i)),
            pl.BlockSpec((8, tm), lambda i: (0, i)),
        ],
        out_shape=[
            jax.ShapeDtypeStruct((t, d), F32),
            jax.ShapeDtypeStruct((t, d // 2), jnp.uint32),
            jax.ShapeDtypeStruct((1, t), jnp.int32),
            jax.ShapeDtypeStruct((8, t), F32),
        ],
        compiler_params=_cparams(("parallel",)),
        name="outproj_route",
    )(mix_m, mix_a, w_out, x, modl, ln_g, ln_b, w_router, b_router)


def _moe_kernel(te0_ref, te1_ref, tnv_ref, perm_ref, permn_ref, x_hbm, ws_ref, wgu0_ref, wgu1_ref, wd0_ref, wd1_ref,
                y_hbm, xbuf, ybuf, gsem, ssem, *, tm_e, d_exp, n_tiles):
    g = pl.program_id(0)
    nv = tnv_ref[g]
    slot = lax.rem(g, 2)
    oslot = 1 - slot

    def gather_copy(sl, r, idx):
        return pltpu.make_async_copy(x_hbm.at[pl.ds(idx, 1)], xbuf.at[sl, pl.ds(r, 1)], gsem.at[sl])

    def scatter_copy(sl, r, idx):
        return pltpu.make_async_copy(ybuf.at[sl, pl.ds(r, 1)], y_hbm.at[pl.ds(idx, 1)], ssem.at[sl])

    def start_gather(sl, idx_ref):
        def body(r, c):
            gather_copy(sl, r, idx_ref[0, 0, r]).start()
            return c
        lax.fori_loop(0, tm_e, body, 0, unroll=8)

    def wait_scatter(sl, n):
        def body(r, c):
            scatter_copy(sl, r, 0).wait()
            return c
        lax.fori_loop(0, n, body, 0)

    @pl.when(g == 0)
    def _():
        start_gather(0, perm_ref)

    @pl.when(nv > 0)
    def _():
        g_next = jnp.minimum(g + 1, n_tiles - 1)
        has_next = jnp.logical_and(g + 1 < n_tiles, tnv_ref[g_next] > 0)

        @pl.when(has_next)
        def _():
            start_gather(oslot, permn_ref)

        def g_wait(r, c):
            gather_copy(slot, r, 0).wait()
            return c
        lax.fori_loop(0, tm_e, g_wait, 0, unroll=8)

        xu = xbuf[slot]
        x_lo = lax.bitcast_convert_type(xu << 16, F32).astype(BF16)
        x_hi = lax.bitcast_convert_type(xu & jnp.uint32(0xFFFF0000), F32).astype(BF16)
        xb = jnp.concatenate([x_lo, x_hi], axis=1)
        y = None
        for k, (wgu_ref, wd_ref) in enumerate(((wgu0_ref, wd0_ref), (wgu1_ref, wd1_ref))):
            gu = jnp.dot(xb, wgu_ref[0], preferred_element_type=F32)
            gt = gu[:, :d_exp]
            a = (gt * jax.nn.sigmoid(gt)) * gu[:, d_exp:]
            ye = jnp.dot(a.astype(BF16), wd_ref[0], preferred_element_type=F32)
            ye = ws_ref[:, k:k + 1] * ye
            y = ye if y is None else y + ye
        ybuf[slot] = y

        def s_start(r, c):
            scatter_copy(slot, r, perm_ref[0, 0, r]).start()
            return c
        lax.fori_loop(0, nv, s_start, 0)

        @pl.when(g > 0)
        def _():
            wait_scatter(oslot, tnv_ref[jnp.maximum(g - 1, 0)])

        @pl.when(jnp.logical_not(has_next))
        def _():
            wait_scatter(slot, nv)


def _moe(xh, bkt, wts, wgu, wd, tm_e):
    t, dh = xh.shape
    d = 2 * dh
    d_exp = wd.shape[1]
    n_tiles = t // tm_e + N_BUCKETS
    n_rows = n_tiles * tm_e
    ids = jnp.arange(N_BUCKETS, dtype=jnp.int32)
    onehot = (bkt[:, None] == ids[None, :]).astype(jnp.int32)
    counts = jnp.sum(onehot, axis=0)
    tiles_b = (counts + tm_e - 1) // tm_e
    tile_end = jnp.cumsum(tiles_b)
    row_start = (tile_end - tiles_b) * tm_e
    rank = jnp.take_along_axis(jnp.cumsum(onehot, axis=0), bkt[:, None], axis=1)[:, 0] - 1
    pos = row_start[bkt] + rank
    tile_ids = jnp.arange(n_tiles, dtype=jnp.int32)
    used = tile_end[-1]
    tb = jnp.sum((tile_end[None, :] <= jnp.minimum(tile_ids, used - 1)[:, None]).astype(jnp.int32), axis=1)
    tile_in_b = tile_ids - (tile_end - tiles_b)[tb]
    tnv = jnp.where(tile_ids < used, jnp.clip(counts[tb] - tile_in_b * tm_e, 0, tm_e), 0).astype(jnp.int32)
    lo = jnp.asarray(PAIR_LO, jnp.int32)
    hi = jnp.asarray(PAIR_HI, jnp.int32)
    te0 = (tb // 6) * EXPERTS_PER_GROUP + lo[tb % 6]
    te1 = (tb // 6) * EXPERTS_PER_GROUP + hi[tb % 6]
    tok = jnp.arange(t, dtype=jnp.int32)
    perm = jnp.zeros((n_rows,), jnp.int32).at[pos].set(tok)
    w_sorted = wts.T[perm]
    perm3 = perm.reshape(n_tiles, 1, tm_e)

    grid_spec = pltpu.PrefetchScalarGridSpec(
        num_scalar_prefetch=3,
        grid=(n_tiles,),
        in_specs=[
            pl.BlockSpec((1, 1, tm_e), lambda g, *_: (g, 0, 0), memory_space=pltpu.SMEM),
            pl.BlockSpec((1, 1, tm_e), lambda g, *_: (jnp.minimum(g + 1, n_tiles - 1), 0, 0),
                         memory_space=pltpu.SMEM),
            pl.BlockSpec(memory_space=pl.ANY),
            pl.BlockSpec((tm_e, 2), lambda g, *_: (g, 0)),
            pl.BlockSpec((1, d, 2 * d_exp), lambda g, e0, e1, tv: (e0[g], 0, 0)),
            pl.BlockSpec((1, d, 2 * d_exp), lambda g, e0, e1, tv: (e1[g], 0, 0)),
            pl.BlockSpec((1, d_exp, d), lambda g, e0, e1, tv: (e0[g], 0, 0)),
            pl.BlockSpec((1, d_exp, d), lambda g, e0, e1, tv: (e1[g], 0, 0)),
        ],
        out_specs=pl.BlockSpec(memory_space=pl.ANY),
        scratch_shapes=[
            pltpu.VMEM((2, tm_e, dh), jnp.uint32),
            pltpu.VMEM((2, tm_e, d), F32),
            pltpu.SemaphoreType.DMA((2,)),
            pltpu.SemaphoreType.DMA((2,)),
        ],
    )
    y = pl.pallas_call(
        functools.partial(_moe_kernel, tm_e=tm_e, d_exp=d_exp, n_tiles=n_tiles),
        grid_spec=grid_spec,
        out_shape=jax.ShapeDtypeStruct((t, d), F32),
        compiler_params=_cparams(("arbitrary",)),
        name="moe_experts",
    )(te0, te1, tnv, perm3, perm3, xh, w_sorted, wgu, wgu, wd, wd)
    return y


def _final_kernel(x1_ref, y_ref, mod_ref, g_ref, b_ref, o_ref, *, alpha):
    z = alpha * x1_ref[...] + mod_ref[0, 5:6, :] * y_ref[...]
    o_ref[...] = _ln_plain(z) * g_ref[...] + b_ref[...]


def _final_ln(x1, y, modl, cond_of_tile, ln_g, ln_b, alpha, tm):
    t, d = x1.shape
    row = lambda i: (i, 0)
    const = lambda i: (0, 0)
    return pl.pallas_call(
        functools.partial(_final_kernel, alpha=alpha),
        grid=(t // tm,),
        in_specs=[
            pl.BlockSpec((tm, d), row),
            pl.BlockSpec((tm, d), row),
            pl.BlockSpec((1, 6, d), lambda i: (cond_of_tile(i), 0, 0)),
            pl.BlockSpec((1, d), const),
            pl.BlockSpec((1, d), const),
        ],
        out_specs=pl.BlockSpec((tm, d), row),
        out_shape=jax.ShapeDtypeStruct((t, d), F32),
        compiler_params=_cparams(("parallel",)),
        name="final_ln",
    )(x1, y, modl, ln_g, ln_b)


def _gate_columns():
    src_i = np.zeros((8 * N_PAIRS,), np.int32)
    src_f = np.zeros((8 * N_PAIRS,), np.int32)
    used = np.zeros((8 * N_PAIRS,), bool)
    for hp in range(N_PAIRS):
        for dd in range(2):
            for hh in range(2):
                j = hp * 8 + dd * 2 + hh
                head = hp * 2 + hh
                src_i[j] = dd * M_HEADS + head
                src_f[j] = (2 + dd) * M_HEADS + head
                used[j] = True
    return src_i, src_f, used


def kernel(x_prompt, x_sample, cache_k, cache_v, state_C, state_n, state_m, c, c_ctx, w_ada, b_ada, w_in, b_gate,
           mh_norm_g, attn_sink, w_out, ln1_g, ln1_b, ln2_g, ln2_b, w_router, b_router, w_exp_gate, w_exp_up,
           w_exp_down):
    batch, seq, d = x_prompt.shape
    dec_batch, dec_seq, _ = x_sample.shape
    depth = w_in.shape[0]
    n_ctx = batch * seq
    n_lat = dec_batch * dec_seq
    t = n_ctx + n_lat
    alpha = (2 * depth) ** 0.25
    tm = 512
    tm_e = 256
    assert n_ctx % tm == 0 and dec_seq % tm == 0 and seq % CHUNK == 0 and dec_seq % CHUNK == 0
    assert n_ctx % dec_seq == 0 and n_ctx % WINDOW == 0 and t % tm_e == 0

    n_ctx_tiles = n_ctx // tm
    tiles_per_seq = dec_seq // tm

    def cond_of_tile(i):
        return jnp.where(i < n_ctx_tiles, 0, 1 + (i - n_ctx_tiles) // tiles_per_seq)

    n_cond = 1 + dec_batch
    cond_rows = -(-n_cond // 8) * 8
    cond = jnp.concatenate([c_ctx[None, :], c, jnp.zeros((cond_rows - n_cond, d), F32)], axis=0)
    mod = _ada_mod(cond, w_ada, b_ada).reshape(depth, cond_rows, 6, d)

    x = jnp.concatenate([x_prompt.reshape(n_ctx, d), x_sample.reshape(n_lat, d)], axis=0)

    src_i, src_f, used = _gate_columns()
    rope_tab = _rope_tables(dec_seq)
    wr = jnp.pad(w_router, ((0, 0), (0, LANES - N_EXPERTS))).astype(BF16)
    br = jnp.pad(b_router, (0, LANES - N_EXPERTS)).reshape(1, LANES)

    ks, vs, cs, ns, ms = [], [], [], [], []
    for l in range(depth):
        wl = w_in[l]
        kscale = M_DK ** -0.5
        w_main = jnp.concatenate(
            [wl[:, :KM_OFF], wl[:, KM_OFF:VM_OFF] * kscale, wl[:, VM_OFF:GATE_SRC_OFF],
             wl[:, GATE_SRC_OFF + 4 * M_HEADS:]], axis=1).astype(BF16)
        wgt = wl[:, GATE_SRC_OFF:GATE_SRC_OFF + 4 * M_HEADS]
        zpad = jnp.zeros((d, LANES - 8 * N_PAIRS), F32)
        w_gate = jnp.concatenate([wgt[:, src_i], zpad, wgt[:, src_f], zpad], axis=1).astype(BF16)
        bpad = jnp.zeros((LANES - 8 * N_PAIRS,), F32)
        bg = jnp.concatenate([b_gate[l][src_i], bpad, b_gate[l][src_f], bpad]).reshape(1, 2 * LANES)
        modl = mod[l]

        proj, gates, kv = _inproj(x, modl, cond_of_tile, w_main, w_gate, bg, tm)
        gb, gu, gut = _gate_prep(gates, 1024)

        gain = mh_norm_g[l].reshape(1, M_HEADS * M_DV)
        sink = attn_sink[l]
        hm_c, c_new, n_new, m_new = _mlstm(proj, gb, gu, gut, gain, 0, batch, seq, None, True)
        at_c = _attn_ctx(proj, sink, batch, seq, 4)
        c0 = state_C[:, l].reshape(dec_batch, 2, N_PAIRS, PAIR_ROWS, M_DV)
        n0 = state_n[:, l].reshape(dec_batch, 2, N_PAIRS, PAIR_ROWS, 1)
        m0 = state_m[:, l].reshape(dec_batch * 2 * M_HEADS)
        (hm_l,) = _mlstm(proj, gb, gu, gut, gain, n_ctx, dec_batch, dec_seq, (c0, n0, m0), False)
        qr, kr = _rope(proj, rope_tab, n_ctx, n_lat, dec_seq, 512)
        ck = jnp.transpose(cache_k[:, l], (0, 2, 1, 3)).astype(BF16)
        cv = jnp.transpose(cache_v[:, l], (0, 2, 1, 3)).astype(BF16)
        at_l = _attn_lat(qr, kr, proj, ck, cv, sink, n_ctx, dec_batch, dec_seq)

        mix_m = jnp.concatenate([hm_c, hm_l], axis=0)
        mix_a = jnp.concatenate([at_c, at_l], axis=0)
        x1, h2, bkt, wts = _outproj(mix_m, mix_a, w_out[l].astype(BF16), x, modl, cond_of_tile,
                                    ln1_g[l].reshape(1, d), ln1_b[l].reshape(1, d), wr, br, alpha, tm)
        wgu = jnp.concatenate([w_exp_gate[l], w_exp_up[l]], axis=-1).astype(BF16)
        wd = w_exp_down[l].astype(BF16)
        y = _moe(h2, bkt[0], wts[:2], wgu, wd, tm_e)
        x = _final_ln(x1, y, modl, cond_of_tile, ln2_g[l].reshape(1, d), ln2_b[l].reshape(1, d), alpha, tm)

        ks.append(kv[:n_ctx, :A_KV_HEADS * A_HD].reshape(batch, seq, A_KV_HEADS, A_HD))
        vs.append(kv[:n_ctx, A_KV_HEADS * A_HD:].reshape(batch, seq, A_KV_HEADS, A_HD))
        cs.append(c_new.reshape(batch, 2, M_HEADS, M_DK, M_DV))
        ns.append(n_new.reshape(batch, 2, M_HEADS, M_DK))
        m4 = m_new[:, :, :4, 0].reshape(batch, N_PAIRS, 2, 2)
        ms.append(jnp.transpose(m4, (0, 2, 1, 3)).reshape(batch, 2, M_HEADS))

    y_prompt = x[:n_ctx].reshape(batch, seq, d)
    y_sample = x[n_ctx:].reshape(dec_batch, dec_seq, d)
    return (y_prompt, y_sample, jnp.stack(ks, 1), jnp.stack(vs, 1), jnp.stack(cs, 1), jnp.stack(ns, 1),
            jnp.stack(ms, 1))
```

```python
import functools

import jax
import jax.numpy as jnp
import numpy as np
from jax import lax
from jax.experimental import pallas as pl
from jax.experimental.pallas import tpu as pltpu

F32 = jnp.float32
BF16 = jnp.bfloat16

M_HEADS = 8
M_DK = 64
M_DV = 128
IGATE_CAP = 15.0
MH_EPS = 1e-6
NEG_INIT = -1e30
A_HEADS = 8
A_KV_HEADS = 2
A_GROUP = A_HEADS // A_KV_HEADS
A_HD = 128
WINDOW = 128
GRID_W = 64
ROPE_BASE = 10000.0
N_EXPERTS = 16
N_GROUPS = 4
EXPERTS_PER_GROUP = N_EXPERTS // N_GROUPS
LN_EPS = 1e-5

LANES = 128
VMEM_LIMIT = 56 * 1024 * 1024

QM_OFF = 0
KM_OFF = M_HEADS * M_DK
VM_OFF = KM_OFF + M_HEADS * M_DK
OM_OFF = VM_OFF + M_HEADS * M_DV
QA_OFF = OM_OFF + M_HEADS * M_DV
KA_OFF = QA_OFF + A_HEADS * A_HD
VA_OFF = KA_OFF + A_KV_HEADS * A_HD
MAIN_COLS = VA_OFF + A_KV_HEADS * A_HD
GATE_SRC_OFF = OM_OFF + M_HEADS * M_DV
N_PAIRS = M_HEADS // 2
PAIR_ROWS = 2 * M_DK
CHUNK = 128
N_BUCKETS = N_GROUPS * 6
PAIR_LO = (0, 0, 0, 1, 1, 2)
PAIR_HI = (1, 2, 3, 2, 3, 3)


def _cparams(sem):
    return pltpu.CompilerParams(dimension_semantics=sem, vmem_limit_bytes=VMEM_LIMIT)


def _ln_plain(x):
    mu = jnp.mean(x, axis=-1, keepdims=True)
    xc = x - mu
    return xc * lax.rsqrt(jnp.mean(xc * xc, axis=-1, keepdims=True) + LN_EPS)


def _ada_kernel(c_ref, w_ref, b_ref, o_ref):
    c = c_ref[...]
    s = (c * jax.nn.sigmoid(c)).astype(BF16)
    o_ref[0] = jnp.dot(s, w_ref[0].astype(BF16), preferred_element_type=F32) + b_ref[0]


def _ada_mod(cond, w_ada, b_ada):
    depth, d, n = w_ada.shape
    rows = cond.shape[0]
    tn = 1024
    return pl.pallas_call(
        _ada_kernel,
        grid=(depth, n // tn),
        in_specs=[
            pl.BlockSpec((rows, d), lambda l, j: (0, 0)),
            pl.BlockSpec((1, d, tn), lambda l, j: (l, 0, j)),
            pl.BlockSpec((1, 1, tn), lambda l, j: (l, 0, j)),
        ],
        out_specs=pl.BlockSpec((1, rows, tn), lambda l, j: (l, 0, j)),
        out_shape=jax.ShapeDtypeStruct((depth, rows, n), F32),
        compiler_params=_cparams(("parallel", "parallel")),
        name="ada_mod",
    )(cond, w_ada, b_ada.reshape(depth, 1, n))


def _split_specs(tm, width, na, off_b):
    spec_a = pl.BlockSpec((tm, width), lambda i, *_: (jnp.minimum(i, na - 1), 0))
    spec_b = pl.BlockSpec((tm, width), lambda i, *_: (jnp.maximum(i - na, 0) + off_b, 0))
    return spec_a, spec_b


def _inproj_kernel(xa_ref, xb_ref, mod_ref, w_ref, wg_ref, bg_ref, proj_ref, gates_ref, kv_ref, h_scr, *, nj, na):
    j = pl.program_id(1)

    @pl.when(j == 0)
    def _():
        x = jnp.where(pl.program_id(0) < na, xa_ref[...], xb_ref[...])
        hn = _ln_plain(x)
        h = hn * (1.0 + mod_ref[0, 1:2, :]) + mod_ref[0, 0:1, :]
        hb = h.astype(BF16)
        h_scr[...] = hb
        gates_ref[...] = jnp.dot(hb, wg_ref[...], preferred_element_type=F32) + bg_ref[...]

    acc = jnp.dot(h_scr[...], w_ref[...], preferred_element_type=F32)
    proj_ref[...] = acc.astype(BF16)

    @pl.when(j == nj - 1)
    def _():
        kv_ref[...] = acc


def _inproj(xs, t, modl, cond_of_tile, w_main, w_gate, b_gate, tm):
    x_a, x_b, na, off_b = xs
    d = x_a.shape[1]
    tn = 512
    nj = MAIN_COLS // tn
    assert KA_OFF == (nj - 1) * tn
    spec_a, spec_b = _split_specs(tm, d, na, off_b)
    return pl.pallas_call(
        functools.partial(_inproj_kernel, nj=nj, na=na),
        grid=(t // tm, nj),
        in_specs=[
            spec_a,
            spec_b,
            pl.BlockSpec((1, 6, d), lambda i, j: (cond_of_tile(i), 0, 0)),
            pl.BlockSpec((d, tn), lambda i, j: (0, j)),
            pl.BlockSpec((d, 2 * LANES), lambda i, j: (0, 0)),
            pl.BlockSpec((1, 2 * LANES), lambda i, j: (0, 0)),
        ],
        out_specs=[
            pl.BlockSpec((tm, tn), lambda i, j: (i, j)),
            pl.BlockSpec((tm, 2 * LANES), lambda i, j: (i, 0)),
            pl.BlockSpec((tm, tn), lambda i, j: (i, 0)),
        ],
        out_shape=[
            jax.ShapeDtypeStruct((t, MAIN_COLS), BF16),
            jax.ShapeDtypeStruct((t, 2 * LANES), F32),
            jax.ShapeDtypeStruct((t, tn), F32),
        ],
        scratch_shapes=[pltpu.VMEM((tm, d), BF16)],
        compiler_params=_cparams(("parallel", "arbitrary")),
        name="inproj",
    )(x_a, x_b, modl, w_main, w_gate, b_gate)


def _gate_kernel(g_ref, gb_ref, gu_ref, gut_ref, *, tg):
    gi = g_ref[:, :LANES]
    gf = g_ref[:, LANES:]
    ig = IGATE_CAP * jnp.tanh(gi / IGATE_CAP)
    lf = jax.nn.log_sigmoid(gf)
    lane = lax.broadcasted_iota(jnp.int32, (1, LANES), 1)
    is_fwd = (lane % 8) < 2
    s_i = lax.broadcasted_iota(jnp.int32, (CHUNK, CHUNK), 0)
    r_i = lax.broadcasted_iota(jnp.int32, (CHUNK, CHUNK), 1)
    tri_lo = (r_i <= s_i).astype(F32)
    tri_hi = (r_i >= s_i).astype(F32)
    for c in range(tg // CHUNK):
        rows = slice(c * CHUNK, (c + 1) * CHUNK)
        lfc = lf[rows]
        pre = jnp.dot(tri_lo, lfc, preferred_element_type=F32, precision=lax.Precision.HIGHEST)
        suf = jnp.dot(tri_hi, lfc, preferred_element_type=F32, precision=lax.Precision.HIGHEST)
        b = jnp.where(is_fwd, pre, suf)
        u = ig[rows] - b
        gb_ref[rows, :] = b
        gu_ref[rows, :] = u
        gut_ref[:, rows] = u.T[: 8 * N_PAIRS]


def _gate_prep(gates, tg):
    t = gates.shape[0]
    return pl.pallas_call(
        functools.partial(_gate_kernel, tg=tg),
        grid=(t // tg,),
        in_specs=[pl.BlockSpec((tg, 2 * LANES), lambda i: (i, 0))],
        out_specs=[
            pl.BlockSpec((tg, LANES), lambda i: (i, 0)),
            pl.BlockSpec((tg, LANES), lambda i: (i, 0)),
            pl.BlockSpec((8 * N_PAIRS, tg), lambda i: (0, i)),
        ],
        out_shape=[
            jax.ShapeDtypeStruct((t, LANES), F32),
            jax.ShapeDtypeStruct((t, LANES), F32),
            jax.ShapeDtypeStruct((8 * N_PAIRS, t), F32),
        ],
        compiler_params=_cparams(("parallel",)),
        name="gate_prep",
    )(gates)


def _mlstm_kernel(*refs, seq, has_state, emit_state):
    it = iter(refs)
    q_ref, k_ref, v_ref, om_ref, gb_ref, gu_ref, gut_ref, gain_ref = (next(it) for _ in range(8))
    if has_state:
        c0_ref, n0_ref, m0_ref = next(it), next(it), next(it)
    out_ref = next(it)
    if emit_state:
        cout_ref, nout_ref, mout_ref = next(it), next(it), next(it)
    hf_scr, cst_scr = next(it), next(it)

    b_id = pl.program_id(0)
    hp = pl.program_id(1)
    nc = seq // CHUNK
    L = CHUNK

    lane = lax.broadcasted_iota(jnp.int32, (1, LANES), 1)
    row128 = lax.broadcasted_iota(jnp.int32, (PAIR_ROWS, 1), 0)
    s_i = lax.broadcasted_iota(jnp.int32, (L, L), 0)
    r_i = lax.broadcasted_iota(jnp.int32, (L, L), 1)
    e0row = (lane == 0).astype(F32)
    e0blk = jnp.broadcast_to(e0row, (L, LANES)).astype(BF16)

    def pick_lane(x, j):
        return jnp.sum(jnp.where(lane == j, x, 0.0), axis=1, keepdims=True)

    for d in range(2):
        tri = (r_i <= s_i) if d == 0 else (r_i >= s_i)
        m_init = []
        for hh in range(2):
            if has_state:
                rowmask = (row128 // M_DK) == hh
                cst_scr[hh, :, :LANES] = jnp.where(rowmask, c0_ref[0, d, 0], 0.0)
                cst_scr[hh, :, LANES:] = jnp.where(rowmask, n0_ref[0, d, 0], 0.0) * e0row
                m0 = m0_ref[b_id * (2 * M_HEADS) + d * M_HEADS + hp * 2 + hh]
                m_init.append(jnp.full((1, 1), m0, F32))
            else:
                cst_scr[hh] = jnp.zeros((PAIR_ROWS, 2 * LANES), F32)
                m_init.append(jnp.full((1, 1), NEG_INIT, F32))

        def body(t, ms, d=d, tri=tri):
            c = t if d == 0 else nc - 1 - t
            r0 = pl.multiple_of(c * L, L)
            q2 = q_ref[pl.ds(r0, L), :]
            k2 = k_ref[pl.ds(r0, L), :]
            gbc = gb_ref[pl.ds(r0, L), :]
            guc = gu_ref[pl.ds(r0, L), :]
            gend = gb_ref[pl.ds(r0 + (L - 1 if d == 0 else 0), 1), :]
            new_ms = []
            for hh in range(2):
                jl = hp * 8 + d * 2 + hh
                lm = (lane // M_DK) == hh
                qh = jnp.where(lm, q2, jnp.zeros_like(q2))
                kh = jnp.where(lm, k2, jnp.zeros_like(k2))
                vh = v_ref[pl.ds(r0, L), hh * LANES:(hh + 1) * LANES]
                vext = jnp.concatenate([vh, e0blk], axis=1)
                urow = gut_ref[d * 2 + hh:d * 2 + hh + 1, pl.ds(r0, L)]
                ucol = pick_lane(guc, jl)
                bcol = pick_lane(gbc, jl)
                g = pick_lane(gend, jl)
                m = ms[hh]
                umat = jnp.where(tri, urow, -jnp.inf)
                cmu = jnp.max(umat, axis=1, keepdims=True)
                mm = jnp.maximum(m, cmu)
                w = jnp.exp(umat - mm)
                sqk = lax.dot_general(qh, kh, (((1,), (1,)), ((), ())), preferred_element_type=F32)
                p = (sqk * w).astype(BF16)
                intra = jnp.dot(p, vext, preferred_element_type=F32)
                cs = cst_scr[hh]
                inter = jnp.dot(qh, cs.astype(BF16), preferred_element_type=F32)
                nd = intra + jnp.exp(m - mm) * inter
                num = nd[:, :LANES]
                den = nd[:, LANES:LANES + 1]
                hv = num / jnp.maximum(jnp.abs(den), jnp.exp(-bcol - mm))
                cols = slice(hh * LANES, (hh + 1) * LANES)
                if d == 0:
                    hf_scr[pl.ds(r0, L), cols] = hv
                else:
                    tot = hf_scr[pl.ds(r0, L), cols] + hv
                    ms2 = jnp.mean(tot * tot, axis=1, keepdims=True)
                    y = tot * lax.rsqrt(ms2 + MH_EPS) * gain_ref[:, cols]
                    y = y * jax.nn.sigmoid(om_ref[pl.ds(r0, L), cols].astype(F32))
                    out_ref[pl.ds(r0, L), cols] = y.astype(BF16)
                maxu = jnp.max(urow, axis=1, keepdims=True)
                m_new = g + jnp.maximum(m, maxu)
                wa = jnp.exp(g + ucol - m_new)
                wc = jnp.exp(g + m - m_new)
                kw = (kh.astype(F32) * wa).astype(BF16)
                upd = lax.dot_general(kw, vext, (((0,), (0,)), ((), ())), preferred_element_type=F32)
                cst_scr[hh] = wc * cs + upd
                new_ms.append(m_new)
            return tuple(new_ms)

        ms_fin = lax.fori_loop(0, nc, body, tuple(m_init))

        if emit_state:
            cout_ref[0, d, 0] = cst_scr[0, :, :LANES] + cst_scr[1, :, :LANES]
            nout_ref[0, d, 0] = cst_scr[0, :, LANES:LANES + 1] + cst_scr[1, :, LANES:LANES + 1]
            for hh in range(2):
                mout_ref[0, 0, d * 2 + hh:d * 2 + hh + 1, :] = jnp.broadcast_to(ms_fin[hh], (1, LANES))

    if emit_state:
        mout_ref[0, 0, 4:8, :] = jnp.zeros((4, LANES), F32)


def _mlstm(proj, gb, gu, gut, gain, row_off, nseq, seq, state0, emit_state):
    assert row_off % seq == 0
    rb = row_off // seq
    has_state = state0 is not None
    kernel = functools.partial(_mlstm_kernel, seq=seq, has_state=has_state, emit_state=emit_state)
    in_specs = [
        pl.BlockSpec((seq, PAIR_ROWS), lambda b, h: (rb + b, QM_OFF // PAIR_ROWS + h)),
        pl.BlockSpec((seq, PAIR_ROWS), lambda b, h: (rb + b, KM_OFF // PAIR_ROWS + h)),
        pl.BlockSpec((seq, 2 * M_DV), lambda b, h: (rb + b, VM_OFF // (2 * M_DV) + h)),
        pl.BlockSpec((seq, 2 * M_DV), lambda b, h: (rb + b, OM_OFF // (2 * M_DV) + h)),
        pl.BlockSpec((seq, LANES), lambda b, h: (rb + b, 0)),
        pl.BlockSpec((seq, LANES), lambda b, h: (rb + b, 0)),
        pl.BlockSpec((8, seq), lambda b, h: (h, rb + b)),
        pl.BlockSpec((1, 2 * M_DV), lambda b, h: (0, h)),
    ]
    args = [proj, proj, proj, proj, gb, gu, gut, gain]
    if has_state:
        c0, n0, m0 = state0
        in_specs += [
            pl.BlockSpec((1, 2, 1, PAIR_ROWS, M_DV), lambda b, h: (b, 0, h, 0, 0)),
            pl.BlockSpec((1, 2, 1, PAIR_ROWS, 1), lambda b, h: (b, 0, h, 0, 0)),
            pl.BlockSpec(memory_space=pltpu.SMEM),
        ]
        args += [c0, n0, m0]
    out_specs = [pl.BlockSpec((seq, 2 * M_DV), lambda b, h: (b, h))]
    out_shape = [jax.ShapeDtypeStruct((nseq * seq, M_HEADS * M_DV), BF16)]
    if emit_state:
        out_specs += [
            pl.BlockSpec((1, 2, 1, PAIR_ROWS, M_DV), lambda b, h: (b, 0, h, 0, 0)),
            pl.BlockSpec((1, 2, 1, PAIR_ROWS, 1), lambda b, h: (b, 0, h, 0, 0)),
            pl.BlockSpec((1, 1, 8, LANES), lambda b, h: (b, h, 0, 0)),
        ]
        out_shape += [
            jax.ShapeDtypeStruct((nseq, 2, N_PAIRS, PAIR_ROWS, M_DV), F32),
            jax.ShapeDtypeStruct((nseq, 2, N_PAIRS, PAIR_ROWS, 1), F32),
            jax.ShapeDtypeStruct((nseq, N_PAIRS, 8, LANES), F32),
        ]
    return pl.pallas_call(
        kernel,
        grid=(nseq, N_PAIRS),
        in_specs=in_specs,
        out_specs=out_specs,
        out_shape=out_shape,
        scratch_shapes=[pltpu.VMEM((seq, 2 * M_DV), F32), pltpu.VMEM((2, PAIR_ROWS, 2 * LANES), F32)],
        compiler_params=_cparams(("parallel", "parallel")),
        name="mlstm",
    )(*args)


def _attn_ctx_kernel(sink_ref, q_ref, k_ref, v_ref, o_ref, *, seq, nb):
    kvh = pl.program_id(1)
    scale = A_HD ** -0.5
    for s in range(nb):
        rows = slice(s * seq, (s + 1) * seq)
        k = k_ref[rows, :]
        v = v_ref[rows, :]
        for g in range(A_GROUP):
            cols = slice(g * A_HD, (g + 1) * A_HD)
            q = q_ref[rows, cols]
            sc = lax.dot_general(q, k, (((1,), (1,)), ((), ())), preferred_element_type=F32) * scale
            sk = sink_ref[kvh * A_GROUP + g]
            mx = jnp.maximum(jnp.max(sc, axis=1, keepdims=True), sk)
            p = jnp.exp(sc - mx)
            den = jnp.sum(p, axis=1, keepdims=True) + jnp.exp(sk - mx)
            o = jnp.dot(p.astype(BF16), v, preferred_element_type=F32) / den
            o_ref[rows, cols] = o.astype(BF16)


def _attn_ctx(proj, sink, nseq, seq, nb):
    gw = A_GROUP * A_HD
    return pl.pallas_call(
        functools.partial(_attn_ctx_kernel, seq=seq, nb=nb),
        grid=(nseq // nb, A_KV_HEADS),
        in_specs=[
            pl.BlockSpec(memory_space=pltpu.SMEM),
            pl.BlockSpec((nb * seq, gw), lambda b, h: (b, QA_OFF // gw + h)),
            pl.BlockSpec((nb * seq, A_HD), lambda b, h: (b, KA_OFF // A_HD + h)),
            pl.BlockSpec((nb * seq, A_HD), lambda b, h: (b, VA_OFF // A_HD + h)),
        ],
        out_specs=pl.BlockSpec((nb * seq, gw), lambda b, h: (b, h)),
        out_shape=jax.ShapeDtypeStruct((nseq * seq, A_HEADS * A_HD), BF16),
        compiler_params=_cparams(("parallel", "parallel")),
        name="attn_ctx",
    )(sink, proj, proj, proj)


def _rope_kernel(q_ref, k_ref, cos_ref, sa_ref, sb_ref, qo_ref, ko_ref):
    cos = cos_ref[...]
    sa = sa_ref[...]
    sb = sb_ref[...]

    def rot(x):
        return x * cos + pltpu.roll(x, LANES - A_HD // 4, 1) * sa + pltpu.roll(x, A_HD // 4, 1) * sb

    for h in range(A_HEADS):
        cols = slice(h * A_HD, (h + 1) * A_HD)
        qo_ref[:, cols] = rot(q_ref[:, cols].astype(F32)).astype(BF16)
    for h in range(A_KV_HEADS):
        cols = slice(h * A_HD, (h + 1) * A_HD)
        ko_ref[:, cols] = rot(k_ref[:, cols].astype(F32)).astype(BF16)


def _rope(proj, tables, row_off, nrows, seq, tr):
    cos, sa, sb = tables
    rb = row_off // tr
    nps = seq // tr
    qw = A_HEADS * A_HD
    kw = A_KV_HEADS * A_HD
    tab = pl.BlockSpec((tr, A_HD), lambda i: (i % nps, 0))
    return pl.pallas_call(
        _rope_kernel,
        grid=(nrows // tr,),
        in_specs=[
            pl.BlockSpec((tr, qw), lambda i: (rb + i, QA_OFF // qw)),
            pl.BlockSpec((tr, kw), lambda i: (rb + i, KA_OFF // kw)),
            tab, tab, tab,
        ],
        out_specs=[pl.BlockSpec((tr, qw), lambda i: (i, 0)), pl.BlockSpec((tr, kw), lambda i: (i, 0))],
        out_shape=[jax.ShapeDtypeStruct((nrows, qw), BF16), jax.ShapeDtypeStruct((nrows, kw), BF16)],
        compiler_params=_cparams(("parallel",)),
        name="rope",
    )(proj, proj, cos, sa, sb)


def _rope_tables(seq):
    half = A_HD // 2
    pos = np.arange(seq)
    row = (pos // GRID_W).astype(np.float32)
    col = (pos % GRID_W).astype(np.float32)
    inv = (ROPE_BASE ** (-np.arange(0, half, 2, dtype=np.float32) / half)).astype(np.float32)
    ang_r = row[:, None] * inv[None, :]
    ang_c = col[:, None] * inv[None, :]
    ang = np.concatenate([ang_r, ang_r, ang_c, ang_c], axis=1).astype(np.float32)
    cos = np.cos(ang).astype(np.float32)
    sin = np.sin(ang).astype(np.float32)
    first = (np.arange(A_HD) % half) < (half // 2)
    sa = np.where(first[None, :], -sin, 0.0).astype(np.float32)
    sb = np.where(first[None, :], 0.0, sin).astype(np.float32)
    return jnp.asarray(cos), jnp.asarray(sa), jnp.asarray(sb)


def _attn_lat_kernel(sink_ref, q_ref, kp_ref, kc_ref, kn_ref, vp_ref, vc_ref, vn_ref, ck_ref, cv_ref, o_ref,
                     *, nblk):
    kvh = pl.program_id(1)
    i = pl.program_id(2)
    qb = WINDOW
    scale = A_HD ** -0.5
    q = jnp.concatenate([q_ref[:, g * A_HD:(g + 1) * A_HD] for g in range(A_GROUP)], axis=0)
    kwin = jnp.concatenate([kp_ref[...], kc_ref[...], kn_ref[...]], axis=0)
    vwin = jnp.concatenate([vp_ref[...], vc_ref[...], vn_ref[...]], axis=0)
    s_lat = lax.dot_general(q, kwin, (((1,), (1,)), ((), ())), preferred_element_type=F32) * scale
    s_ctx = lax.dot_general(q, ck_ref[0, 0], (((1,), (1,)), ((), ())), preferred_element_type=F32) * scale
    r = lax.broadcasted_iota(jnp.int32, (A_GROUP * qb, 3 * qb), 0) % qb
    c = lax.broadcasted_iota(jnp.int32, (A_GROUP * qb, 3 * qb), 1)
    c_lo = jnp.where(i > 0, 0, qb)
    c_hi = jnp.where(i < nblk - 1, 3 * qb, 2 * qb)
    valid = (c >= r) & (c <= r + 2 * WINDOW) & (c >= c_lo) & (c < c_hi)
    s_lat = jnp.where(valid, s_lat, NEG_INIT)
    hrow = lax.broadcasted_iota(jnp.int32, (A_GROUP * qb, 1), 0) // qb
    sk = jnp.zeros((A_GROUP * qb, 1), F32)
    for g in range(A_GROUP):
        sk = jnp.where(hrow == g, sink_ref[kvh * A_GROUP + g], sk)
    mx = jnp.maximum(jnp.maximum(jnp.max(s_lat, axis=1, keepdims=True), jnp.max(s_ctx, axis=1, keepdims=True)), sk)
    p_lat = jnp.exp(s_lat - mx)
    p_ctx = jnp.exp(s_ctx - mx)
    den = jnp.sum(p_lat, axis=1, keepdims=True) + jnp.sum(p_ctx, axis=1, keepdims=True) + jnp.exp(sk - mx)
    o = (jnp.dot(p_lat.astype(BF16), vwin, preferred_element_type=F32)
         + jnp.dot(p_ctx.astype(BF16), cv_ref[0, 0], preferred_element_type=F32)) / den
    for g in range(A_GROUP):
        o_ref[:, g * A_HD:(g + 1) * A_HD] = o[g * qb:(g + 1) * qb].astype(BF16)


def _attn_lat(qr, kr, proj, ck, cv, sink, row_off, nseq, seq):
    qb = WINDOW
    nblk = seq // qb
    gw = A_GROUP * A_HD
    rb = row_off // qb
    p_len = ck.shape[2]
    prev = lambda b, h, i: b * nblk + jnp.maximum(i - 1, 0)
    cur = lambda b, h, i: b * nblk + i
    nxt = lambda b, h, i: b * nblk + jnp.minimum(i + 1, nblk - 1)
    vcol = VA_OFF // A_HD
    return pl.pallas_call(
        functools.partial(_attn_lat_kernel, nblk=nblk),
        grid=(nseq, A_KV_HEADS, nblk),
        in_specs=[
            pl.BlockSpec(memory_space=pltpu.SMEM),
            pl.BlockSpec((qb, gw), lambda b, h, i: (cur(b, h, i), h)),
            pl.BlockSpec((qb, A_HD), lambda b, h, i: (prev(b, h, i), h)),
            pl.BlockSpec((qb, A_HD), lambda b, h, i: (cur(b, h, i), h)),
            pl.BlockSpec((qb, A_HD), lambda b, h, i: (nxt(b, h, i), h)),
            pl.BlockSpec((qb, A_HD), lambda b, h, i: (rb + prev(b, h, i), vcol + h)),
            pl.BlockSpec((qb, A_HD), lambda b, h, i: (rb + cur(b, h, i), vcol + h)),
            pl.BlockSpec((qb, A_HD), lambda b, h, i: (rb + nxt(b, h, i), vcol + h)),
            pl.BlockSpec((1, 1, p_len, A_HD), lambda b, h, i: (b, h, 0, 0)),
            pl.BlockSpec((1, 1, p_len, A_HD), lambda b, h, i: (b, h, 0, 0)),
        ],
        out_specs=pl.BlockSpec((qb, gw), lambda b, h, i: (cur(b, h, i), h)),
        out_shape=jax.ShapeDtypeStruct((nseq * seq, A_HEADS * A_HD), BF16),
        compiler_params=_cparams(("parallel", "parallel", "parallel")),
        name="attn_lat",
    )(sink, qr, kr, kr, kr, proj, proj, proj, ck, cv)


def _outproj_kernel(mmc_ref, mml_ref, mac_ref, mal_ref, w_ref, xa_ref, xb_ref, mod_ref, lng_ref, lnb_ref, wr_ref,
                    br_ref, x1_ref, xw_ref, bkt_ref, *, alpha, na):
    is_a = pl.program_id(0) < na
    d = xa_ref.shape[1]
    half = w_ref.shape[0] // 2
    mix_m = jnp.where(is_a, mmc_ref[...], mml_ref[...])
    mix_a = jnp.where(is_a, mac_ref[...], mal_ref[...])
    f = (jnp.dot(mix_m, w_ref[:half, :], preferred_element_type=F32)
         + jnp.dot(mix_a, w_ref[half:, :], preferred_element_type=F32))
    x = jnp.where(is_a, xa_ref[...], xb_ref[...])
    z = alpha * x + mod_ref[0, 2:3, :] * f
    x1 = _ln_plain(z) * lng_ref[...] + lnb_ref[...]
    x1_ref[...] = x1
    h2 = _ln_plain(x1) * (1.0 + mod_ref[0, 4:5, :]) + mod_ref[0, 3:4, :]
    xw_ref[:, :d] = h2
    logits = jnp.dot(h2.astype(BF16), wr_ref[...], preferred_element_type=F32)
    scores = jax.nn.sigmoid(logits)
    sel = scores + br_ref[...]
    sc_t = scores.T
    sel_t = sel.T
    sv = [sel_t[e:e + 1, :] for e in range(N_EXPERTS)]
    cv = [sc_t[e:e + 1, :] for e in range(N_EXPERTS)]
    gs = []
    for g in range(N_GROUPS):
        v = sv[4 * g:4 * g + 4]
        best = None
        for a, b in zip(PAIR_LO, PAIR_HI):
            ps = v[a] + v[b]
            best = ps if best is None else jnp.maximum(best, ps)
        gs.append(best)
    gmax = jnp.maximum(jnp.maximum(gs[0], gs[1]), jnp.maximum(gs[2], gs[3]))
    grp = jnp.full(gmax.shape, N_GROUPS - 1, jnp.int32)
    for g in range(N_GROUPS - 2, -1, -1):
        grp = jnp.where(gs[g] == gmax, g, grp)
    def pick(vals, k):
        out = vals[k]
        for g in range(1, N_GROUPS):
            out = jnp.where(grp == g, vals[4 * g + k], out)
        return out
    gv = [pick(sv, k) for k in range(EXPERTS_PER_GROUP)]
    gc = [pick(cv, k) for k in range(EXPERTS_PER_GROUP)]
    m1 = jnp.maximum(jnp.maximum(gv[0], gv[1]), jnp.maximum(gv[2], gv[3]))
    k0 = jnp.full(m1.shape, 3, jnp.int32)
    for k in range(2, -1, -1):
        k0 = jnp.where(gv[k] == m1, k, k0)
    gv2 = [jnp.where(k0 == k, -jnp.inf, gv[k]) for k in range(4)]
    m2 = jnp.maximum(jnp.maximum(gv2[0], gv2[1]), jnp.maximum(gv2[2], gv2[3]))
    k1 = jnp.full(m1.shape, 3, jnp.int32)
    for k in range(2, -1, -1):
        k1 = jnp.where(gv2[k] == m2, k, k1)
    lo = jnp.minimum(k0, k1)
    hi = jnp.maximum(k0, k1)
    pair = jnp.where(lo == 0, hi - 1, jnp.where(lo == 1, hi + 1, 5))
    bkt_ref[...] = grp * 6 + pair
    s_lo = jnp.where(lo == 0, gc[0], jnp.where(lo == 1, gc[1], gc[2]))
    s_hi = jnp.where(hi == 1, gc[1], jnp.where(hi == 2, gc[2], gc[3]))
    tot = s_lo + s_hi
    wrow = lax.broadcasted_iota(jnp.int32, (LANES, tot.shape[1]), 0)
    w_t = jnp.where(wrow == 0, s_lo / tot, jnp.where(wrow == 1, s_hi / tot, 0.0))
    xw_ref[:, d:] = w_t.T


def _outproj(mixes, w_out, xs, t, modl, cond_of_tile, ln_g, ln_b, w_router, b_router, alpha, tm):
    x_a, x_b, na, off_b = xs
    d = x_a.shape[1]
    hw = mixes[0].shape[1]
    row = lambda i: (i, 0)
    const = lambda i: (0, 0)
    mix_a, mix_b = _split_specs(tm, hw, na, 0)
    x_sa, x_sb = _split_specs(tm, d, na, off_b)
    return pl.pallas_call(
        functools.partial(_outproj_kernel, alpha=alpha, na=na),
        grid=(t // tm,),
        in_specs=[
            mix_a, mix_b, mix_a, mix_b,
            pl.BlockSpec((2 * hw, d), const),
            x_sa, x_sb,
            pl.BlockSpec((1, 6, d), lambda i: (cond_of_tile(i), 0, 0)),
            pl.BlockSpec((1, d), const),
            pl.BlockSpec((1, d), const),
            pl.BlockSpec((d, LANES), const),
            pl.BlockSpec((1, LANES), const),
        ],
        out_specs=[
            pl.BlockSpec((tm, d), row),
            pl.BlockSpec((tm, d + LANES), row),
            pl.BlockSpec((1, tm), lambda i: (0, i)),
        ],
        out_shape=[
            jax.ShapeDtypeStruct((t, d), F32),
            jax.ShapeDtypeStruct((t, d + LANES), F32),
            jax.ShapeDtypeStruct((1, t), jnp.int32),
        ],
        compiler_params=_cparams(("parallel",)),
        name="outproj_route",
    )(mixes[0], mixes[1], mixes[2], mixes[3], w_out, x_a, x_b, modl, ln_g, ln_b, w_router, b_router)


ROW_DMA_UNROLL = 8


def _moe_kernel(te0_ref, te1_ref, tnv_ref, pg_ref, pgn_ref, ps_ref, x_hbm, wgu0_ref, wgu1_ref, wd0_ref, wd1_ref,
                y_hbm, xbuf, ybuf, gsem, ssem, *, tm_e, d, d_exp, n_tiles, t):
    g = pl.program_id(0)
    nv = tnv_ref[g]
    slot = lax.rem(g, 2)
    oslot = 1 - slot

    def start_rows(copy_of_row, idx_ref):
        def body(i, c):
            for u in range(ROW_DMA_UNROLL):
                r = i * ROW_DMA_UNROLL + u
                copy_of_row(r, idx_ref[0, 0, r]).start(priority=u % 2)
            return c
        lax.fori_loop(0, tm_e // ROW_DMA_UNROLL, body, 0)

    def start_gather(sl, idx_ref):
        start_rows(lambda r, idx: pltpu.make_async_copy(x_hbm.at[pl.ds(idx, 1)], xbuf.at[sl, pl.ds(r, 1)],
                                                        gsem.at[sl]), idx_ref)

    def start_scatter(sl, idx_ref):
        start_rows(lambda r, idx: pltpu.make_async_copy(ybuf.at[sl, pl.ds(r, 1)], y_hbm.at[pl.ds(idx, 1)],
                                                        ssem.at[sl]), idx_ref)

    def wait_gather(sl):
        pltpu.make_async_copy(x_hbm.at[pl.ds(0, tm_e)], xbuf.at[sl], gsem.at[sl]).wait()

    def wait_scatter(sl):
        pltpu.make_async_copy(ybuf.at[sl], y_hbm.at[pl.ds(0, tm_e)], ssem.at[sl]).wait()

    @pl.when(g == 0)
    def _():
        ybuf[1] = jnp.zeros((tm_e, d), F32)
        for k in range(2):
            pltpu.make_async_copy(ybuf.at[1], y_hbm.at[pl.ds(t + k * tm_e, tm_e)], ssem.at[1]).start()
        for k in range(2):
            wait_scatter(1)
        start_gather(0, pg_ref)

    @pl.when(nv > 0)
    def _():
        g_next = jnp.minimum(g + 1, n_tiles - 1)
        has_next = jnp.logical_and(g + 1 < n_tiles, tnv_ref[g_next] > 0)

        @pl.when(has_next)
        def _():
            start_gather(oslot, pgn_ref)

        wait_gather(slot)
        xb = xbuf[slot, :, :d].astype(BF16)
        y = None
        for k, (wgu_ref, wd_ref) in enumerate(((wgu0_ref, wd0_ref), (wgu1_ref, wd1_ref))):
            gu = jnp.dot(xb, wgu_ref[0], preferred_element_type=F32)
            gt = gu[:, :d_exp]
            a = (gt * jax.nn.sigmoid(gt)) * gu[:, d_exp:]
            ye = jnp.dot(a.astype(BF16), wd_ref[0], preferred_element_type=F32)
            ye = xbuf[slot, :, d + k:d + k + 1] * ye
            y = ye if y is None else y + ye
        ybuf[slot] = y
        start_scatter(slot, ps_ref)

        @pl.when(g > 0)
        def _():
            wait_scatter(oslot)

        @pl.when(jnp.logical_not(has_next))
        def _():
            wait_scatter(slot)


def _moe(xw, bkt, wgu, wd, tm_e):
    t, dw = xw.shape
    d = dw - LANES
    d_exp = wd.shape[1]
    n_tiles = t // tm_e + N_BUCKETS
    n_rows = n_tiles * tm_e
    ids = jnp.arange(N_BUCKETS, dtype=jnp.int32)
    onehot = (bkt[:, None] == ids[None, :]).astype(jnp.int32)
    counts = jnp.sum(onehot, axis=0)
    tiles_b = (counts + tm_e - 1) // tm_e
    tile_end = jnp.cumsum(tiles_b)
    row_start = (tile_end - tiles_b) * tm_e
    rank = jnp.take_along_axis(jnp.cumsum(onehot, axis=0), bkt[:, None], axis=1)[:, 0] - 1
    pos = row_start[bkt] + rank
    tile_ids = jnp.arange(n_tiles, dtype=jnp.int32)
    used = tile_end[-1]
    tb = jnp.sum((tile_end[None, :] <= jnp.minimum(tile_ids, used - 1)[:, None]).astype(jnp.int32), axis=1)
    tile_in_b = tile_ids - (tile_end - tiles_b)[tb]
    tnv = jnp.where(tile_ids < used, jnp.clip(counts[tb] - tile_in_b * tm_e, 0, tm_e), 0).astype(jnp.int32)
    lo = jnp.asarray(PAIR_LO, jnp.int32)
    hi = jnp.asarray(PAIR_HI, jnp.int32)
    te0 = (tb // 6) * EXPERTS_PER_GROUP + lo[tb % 6]
    te1 = (tb // 6) * EXPERTS_PER_GROUP + hi[tb % 6]
    tok = jnp.arange(t, dtype=jnp.int32)
    slot_row = jnp.arange(n_rows, dtype=jnp.int32)
    dump = t + ((slot_row // tm_e) % 2) * tm_e + slot_row % tm_e
    perm_s = dump.at[pos].set(tok)
    perm_g = jnp.where(perm_s < t, perm_s, 0)
    pg3 = perm_g.reshape(n_tiles, 1, tm_e)
    ps3 = perm_s.reshape(n_tiles, 1, tm_e)

    grid_spec = pltpu.PrefetchScalarGridSpec(
        num_scalar_prefetch=3,
        grid=(n_tiles,),
        in_specs=[
            pl.BlockSpec((1, 1, tm_e), lambda g, *_: (g, 0, 0), memory_space=pltpu.SMEM),
            pl.BlockSpec((1, 1, tm_e), lambda g, *_: (jnp.minimum(g + 1, n_tiles - 1), 0, 0),
                         memory_space=pltpu.SMEM),
            pl.BlockSpec((1, 1, tm_e), lambda g, *_: (g, 0, 0), memory_space=pltpu.SMEM),
            pl.BlockSpec(memory_space=pl.ANY),
            pl.BlockSpec((1, d, 2 * d_exp), lambda g, e0, e1, tv: (e0[g], 0, 0)),
            pl.BlockSpec((1, d, 2 * d_exp), lambda g, e0, e1, tv: (e1[g], 0, 0)),
            pl.BlockSpec((1, d_exp, d), lambda g, e0, e1, tv: (e0[g], 0, 0)),
            pl.BlockSpec((1, d_exp, d), lambda g, e0, e1, tv: (e1[g], 0, 0)),
        ],
        out_specs=pl.BlockSpec(memory_space=pl.ANY),
        scratch_shapes=[
            pltpu.VMEM((2, tm_e, dw), F32),
            pltpu.VMEM((2, tm_e, d), F32),
            pltpu.SemaphoreType.DMA((2,)),
            pltpu.SemaphoreType.DMA((2,)),
        ],
    )
    y = pl.pallas_call(
        functools.partial(_moe_kernel, tm_e=tm_e, d=d, d_exp=d_exp, n_tiles=n_tiles, t=t),
        grid_spec=grid_spec,
        out_shape=jax.ShapeDtypeStruct((t + 2 * tm_e, d), F32),
        compiler_params=_cparams(("arbitrary",)),
        name="moe_experts",
    )(te0, te1, tnv, pg3, pg3, ps3, xw, wgu, wgu, wd, wd)
    return y


def _final_kernel(x1_ref, y_ref, mod_ref, g_ref, b_ref, *o_refs, alpha, na):
    z = alpha * x1_ref[...] + mod_ref[0, 5:6, :] * y_ref[...]
    out = _ln_plain(z) * g_ref[...] + b_ref[...]
    if len(o_refs) == 1:
        o_refs[0][...] = out
    else:
        @pl.when(pl.program_id(0) < na)
        def _():
            o_refs[0][...] = out

        @pl.when(pl.program_id(0) >= na)
        def _():
            o_refs[1][...] = out


def _final_ln(x1, y, modl, cond_of_tile, ln_g, ln_b, alpha, tm, split_at=None):
    t, d = x1.shape
    row = lambda i: (i, 0)
    const = lambda i: (0, 0)
    if split_at is None:
        out_specs = pl.BlockSpec((tm, d), row)
        out_shape = jax.ShapeDtypeStruct((t, d), F32)
    else:
        out_specs = list(_split_specs(tm, d, split_at, 0))
        out_shape = [jax.ShapeDtypeStruct((split_at * tm, d), F32), jax.ShapeDtypeStruct((t - split_at * tm, d), F32)]
    return pl.pallas_call(
        functools.partial(_final_kernel, alpha=alpha, na=split_at),
        grid=(t // tm,),
        in_specs=[
            pl.BlockSpec((tm, d), row),
            pl.BlockSpec((tm, d), row),
            pl.BlockSpec((1, 6, d), lambda i: (cond_of_tile(i), 0, 0)),
            pl.BlockSpec((1, d), const),
            pl.BlockSpec((1, d), const),
        ],
        out_specs=out_specs,
        out_shape=out_shape,
        compiler_params=_cparams(("arbitrary",)),
        name="final_ln",
    )(x1, y, modl, ln_g, ln_b)


def _gate_columns():
    src_i = np.zeros((8 * N_PAIRS,), np.int32)
    src_f = np.zeros((8 * N_PAIRS,), np.int32)
    for hp in range(N_PAIRS):
        for dd in range(2):
            for hh in range(2):
                j = hp * 8 + dd * 2 + hh
                head = hp * 2 + hh
                src_i[j] = dd * M_HEADS + head
                src_f[j] = (2 + dd) * M_HEADS + head
    return src_i, src_f


def kernel(x_prompt, x_sample, cache_k, cache_v, state_C, state_n, state_m, c, c_ctx, w_ada, b_ada, w_in, b_gate,
           mh_norm_g, attn_sink, w_out, ln1_g, ln1_b, ln2_g, ln2_b, w_router, b_router, w_exp_gate, w_exp_up,
           w_exp_down):
    batch, seq, d = x_prompt.shape
    dec_batch, dec_seq, _ = x_sample.shape
    depth = w_in.shape[0]
    n_ctx = batch * seq
    n_lat = dec_batch * dec_seq
    t = n_ctx + n_lat
    alpha = (2 * depth) ** 0.25
    tm = 512
    tm_e = 256
    assert n_ctx % tm == 0 and dec_seq % tm == 0 and seq % CHUNK == 0 and dec_seq % CHUNK == 0
    assert n_ctx % dec_seq == 0 and n_ctx % WINDOW == 0 and t % tm_e == 0

    n_ctx_tiles = n_ctx // tm
    tiles_per_seq = dec_seq // tm

    def cond_of_tile(i):
        return jnp.where(i < n_ctx_tiles, 0, 1 + (i - n_ctx_tiles) // tiles_per_seq)

    n_cond = 1 + dec_batch
    cond_rows = -(-n_cond // 8) * 8
    cond = jnp.concatenate([c_ctx[None, :], c, jnp.zeros((cond_rows - n_cond, d), F32)], axis=0)
    mod = _ada_mod(cond, w_ada, b_ada).reshape(depth, cond_rows, 6, d)

    xs = (x_prompt.reshape(n_ctx, d), x_sample.reshape(n_lat, d), n_ctx_tiles, 0)

    src_i, src_f = _gate_columns()
    rope_tab = _rope_tables(dec_seq)
    wr = jnp.pad(w_router, ((0, 0), (0, LANES - N_EXPERTS))).astype(BF16)
    br = jnp.pad(b_router, (0, LANES - N_EXPERTS)).reshape(1, LANES)

    ks, vs, cs, ns, ms = [], [], [], [], []
    for l in range(depth):
        wl = w_in[l]
        kscale = M_DK ** -0.5
        w_main = jnp.concatenate(
            [wl[:, :KM_OFF], wl[:, KM_OFF:VM_OFF] * kscale, wl[:, VM_OFF:GATE_SRC_OFF],
             wl[:, GATE_SRC_OFF + 4 * M_HEADS:]], axis=1).astype(BF16)
        wgt = wl[:, GATE_SRC_OFF:GATE_SRC_OFF + 4 * M_HEADS]
        zpad = jnp.zeros((d, LANES - 8 * N_PAIRS), F32)
        w_gate = jnp.concatenate([wgt[:, src_i], zpad, wgt[:, src_f], zpad], axis=1).astype(BF16)
        bpad = jnp.zeros((LANES - 8 * N_PAIRS,), F32)
        bg = jnp.concatenate([b_gate[l][src_i], bpad, b_gate[l][src_f], bpad]).reshape(1, 2 * LANES)
        modl = mod[l]

        proj, gates, kv = _inproj(xs, t, modl, cond_of_tile, w_main, w_gate, bg, tm)
        gb, gu, gut = _gate_prep(gates, 1024)

        gain = mh_norm_g[l].reshape(1, M_HEADS * M_DV)
        sink = attn_sink[l]
        hm_c, c_new, n_new, m_new = _mlstm(proj, gb, gu, gut, gain, 0, batch, seq, None, True)
        at_c = _attn_ctx(proj, sink, batch, seq, 4)
        c0 = state_C[:, l].reshape(dec_batch, 2, N_PAIRS, PAIR_ROWS, M_DV)
        n0 = state_n[:, l].reshape(dec_batch, 2, N_PAIRS, PAIR_ROWS, 1)
        m0 = state_m[:, l].reshape(dec_batch * 2 * M_HEADS)
        (hm_l,) = _mlstm(proj, gb, gu, gut, gain, n_ctx, dec_batch, dec_seq, (c0, n0, m0), False)
        qr, kr = _rope(proj, rope_tab, n_ctx, n_lat, dec_seq, 512)
        ck = jnp.transpose(cache_k[:, l], (0, 2, 1, 3)).astype(BF16)
        cv = jnp.transpose(cache_v[:, l], (0, 2, 1, 3)).astype(BF16)
        at_l = _attn_lat(qr, kr, proj, ck, cv, sink, n_ctx, dec_batch, dec_seq)

        x1, xw, bkt = _outproj((hm_c, hm_l, at_c, at_l), w_out[l].astype(BF16), xs, t, modl, cond_of_tile,
                               ln1_g[l].reshape(1, d), ln1_b[l].reshape(1, d), wr, br, alpha, tm)
        wgu = jnp.concatenate([w_exp_gate[l], w_exp_up[l]], axis=-1).astype(BF16)
        wd = w_exp_down[l].astype(BF16)
        y = _moe(xw, bkt[0], wgu, wd, tm_e)
        last = l == depth - 1
        x_new = _final_ln(x1, y, modl, cond_of_tile, ln2_g[l].reshape(1, d), ln2_b[l].reshape(1, d), alpha, tm,
                          split_at=n_ctx_tiles if last else None)
        if last:
            y_ctx, y_lat = x_new
        else:
            xs = (x_new, x_new, n_ctx_tiles, n_ctx_tiles)

        ks.append(kv[:n_ctx, :A_KV_HEADS * A_HD].reshape(batch, seq, A_KV_HEADS, A_HD))
        vs.append(kv[:n_ctx, A_KV_HEADS * A_HD:].reshape(batch, seq, A_KV_HEADS, A_HD))
        cs.append(c_new.reshape(batch, 2, M_HEADS, M_DK, M_DV))
        ns.append(n_new.reshape(batch, 2, M_HEADS, M_DK))
        m4 = m_new[:, :, :4, 0].reshape(batch, N_PAIRS, 2, 2)
        ms.append(jnp.transpose(m4, (0, 2, 1, 3)).reshape(batch, 2, M_HEADS))

    y_prompt = y_ctx.reshape(batch, seq, d)
    y_sample = y_lat.reshape(dec_batch, dec_seq, d)
    return (y_prompt, y_sample, jnp.stack(ks, 1), jnp.stack(vs, 1), jnp.stack(cs, 1), jnp.stack(ns, 1),
            jnp.stack(ms, 1))
```

```python
import functools

import jax
import jax.numpy as jnp
import numpy as np
from jax import lax
from jax.experimental import pallas as pl
from jax.experimental.pallas import tpu as pltpu

F32 = jnp.float32
BF16 = jnp.bfloat16

M_HEADS = 8
M_DK = 64
M_DV = 128
IGATE_CAP = 15.0
MH_EPS = 1e-6
NEG_INIT = -1e30
A_HEADS = 8
A_KV_HEADS = 2
A_GROUP = A_HEADS // A_KV_HEADS
A_HD = 128
WINDOW = 128
GRID_W = 64
ROPE_BASE = 10000.0
N_EXPERTS = 16
N_GROUPS = 4
EXPERTS_PER_GROUP = N_EXPERTS // N_GROUPS
LN_EPS = 1e-5

LANES = 128
VMEM_LIMIT = 56 * 1024 * 1024

QM_OFF = 0
KM_OFF = M_HEADS * M_DK
VM_OFF = KM_OFF + M_HEADS * M_DK
OM_OFF = VM_OFF + M_HEADS * M_DV
QA_OFF = OM_OFF + M_HEADS * M_DV
KA_OFF = QA_OFF + A_HEADS * A_HD
VA_OFF = KA_OFF + A_KV_HEADS * A_HD
MAIN_COLS = VA_OFF + A_KV_HEADS * A_HD
GATE_SRC_OFF = OM_OFF + M_HEADS * M_DV
N_PAIRS = M_HEADS // 2
PAIR_ROWS = 2 * M_DK
CHUNK = 128
N_BUCKETS = N_GROUPS * 6
PAIR_LO = (0, 0, 0, 1, 1, 2)
PAIR_HI = (1, 2, 3, 2, 3, 3)


def _cparams(sem):
    return pltpu.CompilerParams(dimension_semantics=sem, vmem_limit_bytes=VMEM_LIMIT)


def _ln_plain(x):
    mu = jnp.mean(x, axis=-1, keepdims=True)
    xc = x - mu
    return xc * lax.rsqrt(jnp.mean(xc * xc, axis=-1, keepdims=True) + LN_EPS)


def _ada_kernel(c_ref, w_ref, b_ref, o_ref):
    c = c_ref[...]
    s = (c * jax.nn.sigmoid(c)).astype(BF16)
    o_ref[0] = jnp.dot(s, w_ref[0].astype(BF16), preferred_element_type=F32) + b_ref[0]


def _ada_mod(cond, w_ada, b_ada):
    depth, d, n = w_ada.shape
    rows = cond.shape[0]
    tn = 1024
    return pl.pallas_call(
        _ada_kernel,
        grid=(depth, n // tn),
        in_specs=[
            pl.BlockSpec((rows, d), lambda l, j: (0, 0)),
            pl.BlockSpec((1, d, tn), lambda l, j: (l, 0, j)),
            pl.BlockSpec((1, 1, tn), lambda l, j: (l, 0, j)),
        ],
        out_specs=pl.BlockSpec((1, rows, tn), lambda l, j: (l, 0, j)),
        out_shape=jax.ShapeDtypeStruct((depth, rows, n), F32),
        compiler_params=_cparams(("parallel", "parallel")),
        name="ada_mod",
    )(cond, w_ada, b_ada.reshape(depth, 1, n))


def _split_specs(tm, width, na, off_b):
    spec_a = pl.BlockSpec((tm, width), lambda i, *_: (jnp.minimum(i, na - 1), 0))
    spec_b = pl.BlockSpec((tm, width), lambda i, *_: (jnp.maximum(i - na, 0) + off_b, 0))
    return spec_a, spec_b


def _inproj_kernel(xa_ref, xb_ref, mod_ref, w_ref, wg_ref, bg_ref, proj_ref, gates_ref, kv_ref, h_scr, *, tn, na):
    x = jnp.where(pl.program_id(0) < na, xa_ref[...], xb_ref[...])
    hn = _ln_plain(x)
    h = hn * (1.0 + mod_ref[0, 1:2, :]) + mod_ref[0, 0:1, :]
    hb = h.astype(BF16)
    h_scr[...] = hb
    gates_ref[...] = jnp.dot(hb, wg_ref[...], preferred_element_type=F32) + bg_ref[...]
    nj = MAIN_COLS // tn
    for j in range(nj):
        cols = slice(j * tn, (j + 1) * tn)
        acc = jnp.dot(h_scr[...], w_ref[:, cols], preferred_element_type=F32)
        proj_ref[:, cols] = acc.astype(BF16)
        if j == nj - 1:
            kv_ref[...] = acc


def _inproj(xs, t, modl, cond_of_tile, w_main, w_gate, b_gate, tm):
    x_a, x_b, na, off_b = xs
    d = x_a.shape[1]
    tn = MAIN_COLS - KA_OFF
    assert MAIN_COLS % tn == 0
    spec_a, spec_b = _split_specs(tm, d, na, off_b)
    const = lambda i: (0, 0)
    resident = pl.Buffered(1)
    return pl.pallas_call(
        functools.partial(_inproj_kernel, tn=tn, na=na),
        grid=(t // tm,),
        in_specs=[
            spec_a,
            spec_b,
            pl.BlockSpec((1, 6, d), lambda i: (cond_of_tile(i), 0, 0)),
            pl.BlockSpec((d, MAIN_COLS), const, pipeline_mode=resident),
            pl.BlockSpec((d, 2 * LANES), const, pipeline_mode=resident),
            pl.BlockSpec((1, 2 * LANES), const, pipeline_mode=resident),
        ],
        out_specs=[
            pl.BlockSpec((tm, MAIN_COLS), lambda i: (i, 0)),
            pl.BlockSpec((tm, 2 * LANES), lambda i: (i, 0)),
            pl.BlockSpec((tm, tn), lambda i: (i, 0)),
        ],
        out_shape=[
            jax.ShapeDtypeStruct((t, MAIN_COLS), BF16),
            jax.ShapeDtypeStruct((t, 2 * LANES), F32),
            jax.ShapeDtypeStruct((t, tn), F32),
        ],
        scratch_shapes=[pltpu.VMEM((tm, d), BF16)],
        compiler_params=_cparams(("parallel",)),
        name="inproj",
    )(x_a, x_b, modl, w_main, w_gate, b_gate)


def _gate_kernel(g_ref, gb_ref, gu_ref, gut_ref, *, tg):
    gi = g_ref[:, :LANES]
    gf = g_ref[:, LANES:]
    ig = IGATE_CAP * jnp.tanh(gi / IGATE_CAP)
    lf = jax.nn.log_sigmoid(gf)
    lane = lax.broadcasted_iota(jnp.int32, (1, LANES), 1)
    is_fwd = (lane % 8) < 2
    s_i = lax.broadcasted_iota(jnp.int32, (CHUNK, CHUNK), 0)
    r_i = lax.broadcasted_iota(jnp.int32, (CHUNK, CHUNK), 1)
    tri_lo = (r_i <= s_i).astype(F32)
    tri_hi = (r_i >= s_i).astype(F32)
    for c in range(tg // CHUNK):
        rows = slice(c * CHUNK, (c + 1) * CHUNK)
        lfc = lf[rows]
        pre = jnp.dot(tri_lo, lfc, preferred_element_type=F32, precision=lax.Precision.HIGHEST)
        suf = jnp.dot(tri_hi, lfc, preferred_element_type=F32, precision=lax.Precision.HIGHEST)
        b = jnp.where(is_fwd, pre, suf)
        u = ig[rows] - b
        gb_ref[rows, :] = b
        gu_ref[rows, :] = u
        gut_ref[:, rows] = u.T[: 8 * N_PAIRS]


def _gate_prep(gates, tg):
    t = gates.shape[0]
    return pl.pallas_call(
        functools.partial(_gate_kernel, tg=tg),
        grid=(t // tg,),
        in_specs=[pl.BlockSpec((tg, 2 * LANES), lambda i: (i, 0))],
        out_specs=[
            pl.BlockSpec((tg, LANES), lambda i: (i, 0)),
            pl.BlockSpec((tg, LANES), lambda i: (i, 0)),
            pl.BlockSpec((8 * N_PAIRS, tg), lambda i: (0, i)),
        ],
        out_shape=[
            jax.ShapeDtypeStruct((t, LANES), F32),
            jax.ShapeDtypeStruct((t, LANES), F32),
            jax.ShapeDtypeStruct((8 * N_PAIRS, t), F32),
        ],
        compiler_params=_cparams(("parallel",)),
        name="gate_prep",
    )(gates)


def _mlstm_kernel(*refs, seq, has_state, emit_state):
    it = iter(refs)
    q_ref, k_ref, v_ref, om_ref, gb_ref, gu_ref, gut_ref, gain_ref = (next(it) for _ in range(8))
    if has_state:
        c0_ref, n0_ref, m0_ref = next(it), next(it), next(it)
    out_ref = next(it)
    if emit_state:
        cout_ref, nout_ref, mout_ref = next(it), next(it), next(it)
    h_scr, cst_scr = next(it), next(it)

    b_id = pl.program_id(0)
    hp = pl.program_id(1)
    nc = seq // CHUNK
    L = CHUNK

    lane = lax.broadcasted_iota(jnp.int32, (1, LANES), 1)
    row128 = lax.broadcasted_iota(jnp.int32, (PAIR_ROWS, 1), 0)
    s_i = lax.broadcasted_iota(jnp.int32, (L, L), 0)
    r_i = lax.broadcasted_iota(jnp.int32, (L, L), 1)
    e0row = (lane == 0).astype(F32)
    e0blk = jnp.broadcast_to(e0row, (L, LANES)).astype(BF16)

    def pick_lane(x, j):
        return jnp.sum(jnp.where(lane == j, x, 0.0), axis=1, keepdims=True)

    m_init = []
    for d in range(2):
        for hh in range(2):
            if has_state:
                rowmask = (row128 // M_DK) == hh
                cst_scr[d * 2 + hh, :, :LANES] = jnp.where(rowmask, c0_ref[0, d, 0], 0.0)
                cst_scr[d * 2 + hh, :, LANES:] = jnp.where(rowmask, n0_ref[0, d, 0], 0.0) * e0row
                m0 = m0_ref[b_id * (2 * M_HEADS) + d * M_HEADS + hp * 2 + hh]
                m_init.append(jnp.full((1, 1), m0, F32))
            else:
                cst_scr[d * 2 + hh] = jnp.zeros((PAIR_ROWS, 2 * LANES), F32)
                m_init.append(jnp.full((1, 1), NEG_INIT, F32))

    def body(t, ms):
        new_ms = []
        for d in range(2):
            tri = (r_i <= s_i) if d == 0 else (r_i >= s_i)
            c = t if d == 0 else nc - 1 - t
            r0 = pl.multiple_of(c * L, L)
            q2 = q_ref[pl.ds(r0, L), :]
            k2 = k_ref[pl.ds(r0, L), :]
            gbc = gb_ref[pl.ds(r0, L), :]
            guc = gu_ref[pl.ds(r0, L), :]
            gend = gb_ref[pl.ds(r0 + (L - 1 if d == 0 else 0), 1), :]
            for hh in range(2):
                si = d * 2 + hh
                jl = hp * 8 + d * 2 + hh
                lm = (lane // M_DK) == hh
                qh = jnp.where(lm, q2, jnp.zeros_like(q2))
                kh = jnp.where(lm, k2, jnp.zeros_like(k2))
                vh = v_ref[pl.ds(r0, L), hh * LANES:(hh + 1) * LANES]
                vext = jnp.concatenate([vh, e0blk], axis=1)
                urow = gut_ref[d * 2 + hh:d * 2 + hh + 1, pl.ds(r0, L)]
                ucol = pick_lane(guc, jl)
                bcol = pick_lane(gbc, jl)
                g = pick_lane(gend, jl)
                m = ms[si]
                umat = jnp.where(tri, urow, -jnp.inf)
                cmu = jnp.max(umat, axis=1, keepdims=True)
                mm = jnp.maximum(m, cmu)
                w = jnp.exp(umat - mm)
                sqk = lax.dot_general(qh, kh, (((1,), (1,)), ((), ())), preferred_element_type=F32)
                p = (sqk * w).astype(BF16)
                intra = jnp.dot(p, vext, preferred_element_type=F32)
                cs = cst_scr[si]
                inter = jnp.dot(qh, cs.astype(BF16), preferred_element_type=F32)
                nd = intra + jnp.exp(m - mm) * inter
                num = nd[:, :LANES]
                den = nd[:, LANES:LANES + 1]
                hv = num / jnp.maximum(jnp.abs(den), jnp.exp(-bcol - mm))
                h_scr[d, pl.ds(r0, L), hh * LANES:(hh + 1) * LANES] = hv
                maxu = jnp.max(urow, axis=1, keepdims=True)
                m_new = g + jnp.maximum(m, maxu)
                wa = jnp.exp(g + ucol - m_new)
                wc = jnp.exp(g + m - m_new)
                kw = (kh.astype(F32) * wa).astype(BF16)
                upd = lax.dot_general(kw, vext, (((0,), (0,)), ((), ())), preferred_element_type=F32)
                cst_scr[si] = wc * cs + upd
                new_ms.append(m_new)
        return tuple(new_ms)

    ms_fin = lax.fori_loop(0, nc, body, tuple(m_init))

    def combine(c, carry):
        r0 = pl.multiple_of(c * L, L)
        for hh in range(2):
            cols = slice(hh * LANES, (hh + 1) * LANES)
            tot = h_scr[0, pl.ds(r0, L), cols] + h_scr[1, pl.ds(r0, L), cols]
            ms2 = jnp.mean(tot * tot, axis=1, keepdims=True)
            y = tot * lax.rsqrt(ms2 + MH_EPS) * gain_ref[:, cols]
            y = y * jax.nn.sigmoid(om_ref[pl.ds(r0, L), cols].astype(F32))
            out_ref[pl.ds(r0, L), cols] = y.astype(BF16)
        return carry

    lax.fori_loop(0, nc, combine, 0)

    if emit_state:
        for d in range(2):
            cout_ref[0, d, 0] = cst_scr[d * 2, :, :LANES] + cst_scr[d * 2 + 1, :, :LANES]
            nout_ref[0, d, 0] = cst_scr[d * 2, :, LANES:LANES + 1] + cst_scr[d * 2 + 1, :, LANES:LANES + 1]
            for hh in range(2):
                mout_ref[0, 0, d * 2 + hh:d * 2 + hh + 1, :] = jnp.broadcast_to(ms_fin[d * 2 + hh], (1, LANES))
        mout_ref[0, 0, 4:8, :] = jnp.zeros((4, LANES), F32)


def _mlstm(proj, gb, gu, gut, gain, row_off, nseq, seq, state0, emit_state):
    assert row_off % seq == 0
    rb = row_off // seq
    has_state = state0 is not None
    kernel = functools.partial(_mlstm_kernel, seq=seq, has_state=has_state, emit_state=emit_state)
    in_specs = [
        pl.BlockSpec((seq, PAIR_ROWS), lambda b, h: (rb + b, QM_OFF // PAIR_ROWS + h)),
        pl.BlockSpec((seq, PAIR_ROWS), lambda b, h: (rb + b, KM_OFF // PAIR_ROWS + h)),
        pl.BlockSpec((seq, 2 * M_DV), lambda b, h: (rb + b, VM_OFF // (2 * M_DV) + h)),
        pl.BlockSpec((seq, 2 * M_DV), lambda b, h: (rb + b, OM_OFF // (2 * M_DV) + h)),
        pl.BlockSpec((seq, LANES), lambda b, h: (rb + b, 0)),
        pl.BlockSpec((seq, LANES), lambda b, h: (rb + b, 0)),
        pl.BlockSpec((8, seq), lambda b, h: (h, rb + b)),
        pl.BlockSpec((1, 2 * M_DV), lambda b, h: (0, h)),
    ]
    args = [proj, proj, proj, proj, gb, gu, gut, gain]
    if has_state:
        c0, n0, m0 = state0
        in_specs += [
            pl.BlockSpec((1, 2, 1, PAIR_ROWS, M_DV), lambda b, h: (b, 0, h, 0, 0)),
            pl.BlockSpec((1, 2, 1, PAIR_ROWS, 1), lambda b, h: (b, 0, h, 0, 0)),
            pl.BlockSpec(memory_space=pltpu.SMEM),
        ]
        args += [c0, n0, m0]
    out_specs = [pl.BlockSpec((seq, 2 * M_DV), lambda b, h: (b, h))]
    out_shape = [jax.ShapeDtypeStruct((nseq * seq, M_HEADS * M_DV), BF16)]
    if emit_state:
        out_specs += [
            pl.BlockSpec((1, 2, 1, PAIR_ROWS, M_DV), lambda b, h: (b, 0, h, 0, 0)),
            pl.BlockSpec((1, 2, 1, PAIR_ROWS, 1), lambda b, h: (b, 0, h, 0, 0)),
            pl.BlockSpec((1, 1, 8, LANES), lambda b, h: (b, h, 0, 0)),
        ]
        out_shape += [
            jax.ShapeDtypeStruct((nseq, 2, N_PAIRS, PAIR_ROWS, M_DV), F32),
            jax.ShapeDtypeStruct((nseq, 2, N_PAIRS, PAIR_ROWS, 1), F32),
            jax.ShapeDtypeStruct((nseq, N_PAIRS, 8, LANES), F32),
        ]
    return pl.pallas_call(
        kernel,
        grid=(nseq, N_PAIRS),
        in_specs=in_specs,
        out_specs=out_specs,
        out_shape=out_shape,
        scratch_shapes=[pltpu.VMEM((2, seq, 2 * M_DV), F32), pltpu.VMEM((4, PAIR_ROWS, 2 * LANES), F32)],
        compiler_params=_cparams(("parallel", "parallel")),
        name="mlstm",
    )(*args)


def _attn_ctx_kernel(sink_ref, q_ref, k_ref, v_ref, o_ref, *, seq, nb):
    kvh = pl.program_id(1)
    scale = A_HD ** -0.5
    for s in range(nb):
        rows = slice(s * seq, (s + 1) * seq)
        k = k_ref[rows, :]
        v = v_ref[rows, :]
        for g in range(A_GROUP):
            cols = slice(g * A_HD, (g + 1) * A_HD)
            q = q_ref[rows, cols]
            sc = lax.dot_general(q, k, (((1,), (1,)), ((), ())), preferred_element_type=F32) * scale
            sk = sink_ref[kvh * A_GROUP + g]
            mx = jnp.maximum(jnp.max(sc, axis=1, keepdims=True), sk)
            p = jnp.exp(sc - mx)
            den = jnp.sum(p, axis=1, keepdims=True) + jnp.exp(sk - mx)
            o = jnp.dot(p.astype(BF16), v, preferred_element_type=F32) / den
            o_ref[rows, cols] = o.astype(BF16)


def _attn_ctx(proj, sink, nseq, seq, nb):
    gw = A_GROUP * A_HD
    return pl.pallas_call(
        functools.partial(_attn_ctx_kernel, seq=seq, nb=nb),
        grid=(nseq // nb, A_KV_HEADS),
        in_specs=[
            pl.BlockSpec(memory_space=pltpu.SMEM),
            pl.BlockSpec((nb * seq, gw), lambda b, h: (b, QA_OFF // gw + h)),
            pl.BlockSpec((nb * seq, A_HD), lambda b, h: (b, KA_OFF // A_HD + h)),
            pl.BlockSpec((nb * seq, A_HD), lambda b, h: (b, VA_OFF // A_HD + h)),
        ],
        out_specs=pl.BlockSpec((nb * seq, gw), lambda b, h: (b, h)),
        out_shape=jax.ShapeDtypeStruct((nseq * seq, A_HEADS * A_HD), BF16),
        compiler_params=_cparams(("parallel", "parallel")),
        name="attn_ctx",
    )(sink, proj, proj, proj)


def _rope_kernel(q_ref, k_ref, cos_ref, sa_ref, sb_ref, qo_ref, ko_ref):
    cos = cos_ref[...]
    sa = sa_ref[...]
    sb = sb_ref[...]

    def rot(x):
        return x * cos + pltpu.roll(x, LANES - A_HD // 4, 1) * sa + pltpu.roll(x, A_HD // 4, 1) * sb

    for h in range(A_HEADS):
        cols = slice(h * A_HD, (h + 1) * A_HD)
        qo_ref[:, cols] = rot(q_ref[:, cols].astype(F32)).astype(BF16)
    for h in range(A_KV_HEADS):
        cols = slice(h * A_HD, (h + 1) * A_HD)
        ko_ref[:, cols] = rot(k_ref[:, cols].astype(F32)).astype(BF16)


def _rope(proj, tables, row_off, nrows, seq, tr):
    cos, sa, sb = tables
    rb = row_off // tr
    nps = seq // tr
    qw = A_HEADS * A_HD
    kw = A_KV_HEADS * A_HD
    tab = pl.BlockSpec((tr, A_HD), lambda i: (i % nps, 0))
    return pl.pallas_call(
        _rope_kernel,
        grid=(nrows // tr,),
        in_specs=[
            pl.BlockSpec((tr, qw), lambda i: (rb + i, QA_OFF // qw)),
            pl.BlockSpec((tr, kw), lambda i: (rb + i, KA_OFF // kw)),
            tab, tab, tab,
        ],
        out_specs=[pl.BlockSpec((tr, qw), lambda i: (i, 0)), pl.BlockSpec((tr, kw), lambda i: (i, 0))],
        out_shape=[jax.ShapeDtypeStruct((nrows, qw), BF16), jax.ShapeDtypeStruct((nrows, kw), BF16)],
        compiler_params=_cparams(("parallel",)),
        name="rope",
    )(proj, proj, cos, sa, sb)


def _rope_tables(seq):
    half = A_HD // 2
    pos = np.arange(seq)
    row = (pos // GRID_W).astype(np.float32)
    col = (pos % GRID_W).astype(np.float32)
    inv = (ROPE_BASE ** (-np.arange(0, half, 2, dtype=np.float32) / half)).astype(np.float32)
    ang_r = row[:, None] * inv[None, :]
    ang_c = col[:, None] * inv[None, :]
    ang = np.concatenate([ang_r, ang_r, ang_c, ang_c], axis=1).astype(np.float32)
    cos = np.cos(ang).astype(np.float32)
    sin = np.sin(ang).astype(np.float32)
    first = (np.arange(A_HD) % half) < (half // 2)
    sa = np.where(first[None, :], -sin, 0.0).astype(np.float32)
    sb = np.where(first[None, :], 0.0, sin).astype(np.float32)
    return jnp.asarray(cos), jnp.asarray(sa), jnp.asarray(sb)


def _attn_lat_kernel(sink_ref, q_ref, kp_ref, kc_ref, kn_ref, vp_ref, vc_ref, vn_ref, ck_ref, cv_ref, o_ref,
                     *, nblk):
    kvh = pl.program_id(1)
    i = pl.program_id(2)
    qb = WINDOW
    scale = A_HD ** -0.5
    q = jnp.concatenate([q_ref[:, g * A_HD:(g + 1) * A_HD] for g in range(A_GROUP)], axis=0)
    kwin = jnp.concatenate([kp_ref[...], kc_ref[...], kn_ref[...]], axis=0)
    vwin = jnp.concatenate([vp_ref[...], vc_ref[...], vn_ref[...]], axis=0)
    s_lat = lax.dot_general(q, kwin, (((1,), (1,)), ((), ())), preferred_element_type=F32) * scale
    s_ctx = lax.dot_general(q, ck_ref[0, 0], (((1,), (1,)), ((), ())), preferred_element_type=F32) * scale
    r = lax.broadcasted_iota(jnp.int32, (A_GROUP * qb, 3 * qb), 0) % qb
    c = lax.broadcasted_iota(jnp.int32, (A_GROUP * qb, 3 * qb), 1)
    c_lo = jnp.where(i > 0, 0, qb)
    c_hi = jnp.where(i < nblk - 1, 3 * qb, 2 * qb)
    valid = (c >= r) & (c <= r + 2 * WINDOW) & (c >= c_lo) & (c < c_hi)
    s_lat = jnp.where(valid, s_lat, NEG_INIT)
    hrow = lax.broadcasted_iota(jnp.int32, (A_GROUP * qb, 1), 0) // qb
    sk = jnp.zeros((A_GROUP * qb, 1), F32)
    for g in range(A_GROUP):
        sk = jnp.where(hrow == g, sink_ref[kvh * A_GROUP + g], sk)
    mx = jnp.maximum(jnp.maximum(jnp.max(s_lat, axis=1, keepdims=True), jnp.max(s_ctx, axis=1, keepdims=True)), sk)
    p_lat = jnp.exp(s_lat - mx)
    p_ctx = jnp.exp(s_ctx - mx)
    den = jnp.sum(p_lat, axis=1, keepdims=True) + jnp.sum(p_ctx, axis=1, keepdims=True) + jnp.exp(sk - mx)
    o = (jnp.dot(p_lat.astype(BF16), vwin, preferred_element_type=F32)
         + jnp.dot(p_ctx.astype(BF16), cv_ref[0, 0], preferred_element_type=F32)) / den
    for g in range(A_GROUP):
        o_ref[:, g * A_HD:(g + 1) * A_HD] = o[g * qb:(g + 1) * qb].astype(BF16)


def _attn_lat(qr, kr, proj, ck, cv, sink, row_off, nseq, seq):
    qb = WINDOW
    nblk = seq // qb
    gw = A_GROUP * A_HD
    rb = row_off // qb
    p_len = ck.shape[2]
    prev = lambda b, h, i: b * nblk + jnp.maximum(i - 1, 0)
    cur = lambda b, h, i: b * nblk + i
    nxt = lambda b, h, i: b * nblk + jnp.minimum(i + 1, nblk - 1)
    vcol = VA_OFF // A_HD
    return pl.pallas_call(
        functools.partial(_attn_lat_kernel, nblk=nblk),
        grid=(nseq, A_KV_HEADS, nblk),
        in_specs=[
            pl.BlockSpec(memory_space=pltpu.SMEM),
            pl.BlockSpec((qb, gw), lambda b, h, i: (cur(b, h, i), h)),
            pl.BlockSpec((qb, A_HD), lambda b, h, i: (prev(b, h, i), h)),
            pl.BlockSpec((qb, A_HD), lambda b, h, i: (cur(b, h, i), h)),
            pl.BlockSpec((qb, A_HD), lambda b, h, i: (nxt(b, h, i), h)),
            pl.BlockSpec((qb, A_HD), lambda b, h, i: (rb + prev(b, h, i), vcol + h)),
            pl.BlockSpec((qb, A_HD), lambda b, h, i: (rb + cur(b, h, i), vcol + h)),
            pl.BlockSpec((qb, A_HD), lambda b, h, i: (rb + nxt(b, h, i), vcol + h)),
            pl.BlockSpec((1, 1, p_len, A_HD), lambda b, h, i: (b, h, 0, 0)),
            pl.BlockSpec((1, 1, p_len, A_HD), lambda b, h, i: (b, h, 0, 0)),
        ],
        out_specs=pl.BlockSpec((qb, gw), lambda b, h, i: (cur(b, h, i), h)),
        out_shape=jax.ShapeDtypeStruct((nseq * seq, A_HEADS * A_HD), BF16),
        compiler_params=_cparams(("parallel", "parallel", "parallel")),
        name="attn_lat",
    )(sink, qr, kr, kr, kr, proj, proj, proj, ck, cv)


def _outproj_kernel(mmc_ref, mml_ref, mac_ref, mal_ref, w_ref, xa_ref, xb_ref, mod_ref, lng_ref, lnb_ref, wr_ref,
                    br_ref, x1_ref, xw_ref, bkt_ref, *, alpha, na):
    is_a = pl.program_id(0) < na
    d = xa_ref.shape[1]
    half = w_ref.shape[0] // 2
    mix_m = jnp.where(is_a, mmc_ref[...], mml_ref[...])
    mix_a = jnp.where(is_a, mac_ref[...], mal_ref[...])
    f = (jnp.dot(mix_m, w_ref[:half, :], preferred_element_type=F32)
         + jnp.dot(mix_a, w_ref[half:, :], preferred_element_type=F32))
    x = jnp.where(is_a, xa_ref[...], xb_ref[...])
    z = alpha * x + mod_ref[0, 2:3, :] * f
    x1 = _ln_plain(z) * lng_ref[...] + lnb_ref[...]
    x1_ref[...] = x1
    h2 = _ln_plain(x1) * (1.0 + mod_ref[0, 4:5, :]) + mod_ref[0, 3:4, :]
    xw_ref[:, :d] = h2
    logits = jnp.dot(h2.astype(BF16), wr_ref[...], preferred_element_type=F32)
    scores = jax.nn.sigmoid(logits)
    sel = scores + br_ref[...]
    sc_t = scores.T
    sel_t = sel.T
    sv = [sel_t[e:e + 1, :] for e in range(N_EXPERTS)]
    cv = [sc_t[e:e + 1, :] for e in range(N_EXPERTS)]
    gs = []
    for g in range(N_GROUPS):
        v = sv[4 * g:4 * g + 4]
        best = None
        for a, b in zip(PAIR_LO, PAIR_HI):
            ps = v[a] + v[b]
            best = ps if best is None else jnp.maximum(best, ps)
        gs.append(best)
    gmax = jnp.maximum(jnp.maximum(gs[0], gs[1]), jnp.maximum(gs[2], gs[3]))
    grp = jnp.full(gmax.shape, N_GROUPS - 1, jnp.int32)
    for g in range(N_GROUPS - 2, -1, -1):
        grp = jnp.where(gs[g] == gmax, g, grp)
    def pick(vals, k):
        out = vals[k]
        for g in range(1, N_GROUPS):
            out = jnp.where(grp == g, vals[4 * g + k], out)
        return out
    gv = [pick(sv, k) for k in range(EXPERTS_PER_GROUP)]
    gc = [pick(cv, k) for k in range(EXPERTS_PER_GROUP)]
    m1 = jnp.maximum(jnp.maximum(gv[0], gv[1]), jnp.maximum(gv[2], gv[3]))
    k0 = jnp.full(m1.shape, 3, jnp.int32)
    for k in range(2, -1, -1):
        k0 = jnp.where(gv[k] == m1, k, k0)
    gv2 = [jnp.where(k0 == k, -jnp.inf, gv[k]) for k in range(4)]
    m2 = jnp.maximum(jnp.maximum(gv2[0], gv2[1]), jnp.maximum(gv2[2], gv2[3]))
    k1 = jnp.full(m1.shape, 3, jnp.int32)
    for k in range(2, -1, -1):
        k1 = jnp.where(gv2[k] == m2, k, k1)
    lo = jnp.minimum(k0, k1)
    hi = jnp.maximum(k0, k1)
    pair = jnp.where(lo == 0, hi - 1, jnp.where(lo == 1, hi + 1, 5))
    bkt_ref[...] = grp * 6 + pair
    s_lo = jnp.where(lo == 0, gc[0], jnp.where(lo == 1, gc[1], gc[2]))
    s_hi = jnp.where(hi == 1, gc[1], jnp.where(hi == 2, gc[2], gc[3]))
    tot = s_lo + s_hi
    wrow = lax.broadcasted_iota(jnp.int32, (LANES, tot.shape[1]), 0)
    w_t = jnp.where(wrow == 0, s_lo / tot, jnp.where(wrow == 1, s_hi / tot, 0.0))
    xw_ref[:, d:] = w_t.T


def _outproj(mixes, w_out, xs, t, modl, cond_of_tile, ln_g, ln_b, w_router, b_router, alpha, tm):
    x_a, x_b, na, off_b = xs
    d = x_a.shape[1]
    hw = mixes[0].shape[1]
    row = lambda i: (i, 0)
    const = lambda i: (0, 0)
    mix_a, mix_b = _split_specs(tm, hw, na, 0)
    x_sa, x_sb = _split_specs(tm, d, na, off_b)
    return pl.pallas_call(
        functools.partial(_outproj_kernel, alpha=alpha, na=na),
        grid=(t // tm,),
        in_specs=[
            mix_a, mix_b, mix_a, mix_b,
            pl.BlockSpec((2 * hw, d), const, pipeline_mode=pl.Buffered(1)),
            x_sa, x_sb,
            pl.BlockSpec((1, 6, d), lambda i: (cond_of_tile(i), 0, 0)),
            pl.BlockSpec((1, d), const),
            pl.BlockSpec((1, d), const),
            pl.BlockSpec((d, LANES), const),
            pl.BlockSpec((1, LANES), const),
        ],
        out_specs=[
            pl.BlockSpec((tm, d), row),
            pl.BlockSpec((tm, d + LANES), row),
            pl.BlockSpec((1, tm), lambda i: (0, i)),
        ],
        out_shape=[
            jax.ShapeDtypeStruct((t, d), F32),
            jax.ShapeDtypeStruct((t, d + LANES), F32),
            jax.ShapeDtypeStruct((1, t), jnp.int32),
        ],
        compiler_params=_cparams(("parallel",)),
        name="outproj_route",
    )(mixes[0], mixes[1], mixes[2], mixes[3], w_out, x_a, x_b, modl, ln_g, ln_b, w_router, b_router)


ROW_DMA_UNROLL = 8


def _moe_kernel(te0_ref, te1_ref, tnv_ref, pg_ref, pgn_ref, ps_ref, x_hbm, wgu0_ref, wgu1_ref, wd0_ref, wd1_ref,
                y_hbm, xbuf, ybuf, gsem, ssem, *, tm_e, d, d_exp, n_tiles, t):
    g = pl.program_id(0)
    nv = tnv_ref[g]
    slot = lax.rem(g, 2)
    oslot = 1 - slot

    def start_rows(copy_of_row, idx_ref):
        def body(i, c):
            for u in range(ROW_DMA_UNROLL):
                r = i * ROW_DMA_UNROLL + u
                copy_of_row(r, idx_ref[0, 0, r]).start(priority=u % 2)
            return c
        lax.fori_loop(0, tm_e // ROW_DMA_UNROLL, body, 0)

    def start_gather(sl, idx_ref):
        start_rows(lambda r, idx: pltpu.make_async_copy(x_hbm.at[pl.ds(idx, 1)], xbuf.at[sl, pl.ds(r, 1)],
                                                        gsem.at[sl]), idx_ref)

    def start_scatter(sl, idx_ref):
        start_rows(lambda r, idx: pltpu.make_async_copy(ybuf.at[sl, pl.ds(r, 1)], y_hbm.at[pl.ds(idx, 1)],
                                                        ssem.at[sl]), idx_ref)

    def wait_gather(sl):
        pltpu.make_async_copy(x_hbm.at[pl.ds(0, tm_e)], xbuf.at[sl], gsem.at[sl]).wait()

    def wait_scatter(sl):
        pltpu.make_async_copy(ybuf.at[sl], y_hbm.at[pl.ds(0, tm_e)], ssem.at[sl]).wait()

    @pl.when(g == 0)
    def _():
        ybuf[1] = jnp.zeros((tm_e, d), F32)
        for k in range(2):
            pltpu.make_async_copy(ybuf.at[1], y_hbm.at[pl.ds(t + k * tm_e, tm_e)], ssem.at[1]).start()
        for k in range(2):
            wait_scatter(1)
        start_gather(0, pg_ref)

    @pl.when(nv > 0)
    def _():
        g_next = jnp.minimum(g + 1, n_tiles - 1)
        has_next = jnp.logical_and(g + 1 < n_tiles, tnv_ref[g_next] > 0)

        @pl.when(has_next)
        def _():
            start_gather(oslot, pgn_ref)

        wait_gather(slot)
        xb = xbuf[slot, :, :d].astype(BF16)
        y = None
        for k, (wgu_ref, wd_ref) in enumerate(((wgu0_ref, wd0_ref), (wgu1_ref, wd1_ref))):
            gu = jnp.dot(xb, wgu_ref[0], preferred_element_type=F32)
            gt = gu[:, :d_exp]
            a = (gt * jax.nn.sigmoid(gt)) * gu[:, d_exp:]
            ye = jnp.dot(a.astype(BF16), wd_ref[0], preferred_element_type=F32)
            ye = xbuf[slot, :, d + k:d + k + 1] * ye
            y = ye if y is None else y + ye
        ybuf[slot] = y
        start_scatter(slot, ps_ref)

        @pl.when(g > 0)
        def _():
            wait_scatter(oslot)

        @pl.when(jnp.logical_not(has_next))
        def _():
            wait_scatter(slot)


def _moe(xw, bkt, wgu, wd, tm_e):
    t, dw = xw.shape
    d = dw - LANES
    d_exp = wd.shape[1]
    n_tiles = t // tm_e + N_BUCKETS
    n_rows = n_tiles * tm_e
    ids = jnp.arange(N_BUCKETS, dtype=jnp.int32)
    onehot = (bkt[:, None] == ids[None, :]).astype(jnp.int32)
    counts = jnp.sum(onehot, axis=0)
    tiles_b = (counts + tm_e - 1) // tm_e
    tile_end = jnp.cumsum(tiles_b)
    row_start = (tile_end - tiles_b) * tm_e
    rank = jnp.take_along_axis(jnp.cumsum(onehot, axis=0), bkt[:, None], axis=1)[:, 0] - 1
    pos = row_start[bkt] + rank
    tile_ids = jnp.arange(n_tiles, dtype=jnp.int32)
    used = tile_end[-1]
    tb = jnp.sum((tile_end[None, :] <= jnp.minimum(tile_ids, used - 1)[:, None]).astype(jnp.int32), axis=1)
    tile_in_b = tile_ids - (tile_end - tiles_b)[tb]
    tnv = jnp.where(tile_ids < used, jnp.clip(counts[tb] - tile_in_b * tm_e, 0, tm_e), 0).astype(jnp.int32)
    lo = jnp.asarray(PAIR_LO, jnp.int32)
    hi = jnp.asarray(PAIR_HI, jnp.int32)
    te0 = (tb // 6) * EXPERTS_PER_GROUP + lo[tb % 6]
    te1 = (tb // 6) * EXPERTS_PER_GROUP + hi[tb % 6]
    tok = jnp.arange(t, dtype=jnp.int32)
    slot_row = jnp.arange(n_rows, dtype=jnp.int32)
    dump = t + ((slot_row // tm_e) % 2) * tm_e + slot_row % tm_e
    perm_s = dump.at[pos].set(tok)
    perm_g = jnp.where(perm_s < t, perm_s, 0)
    pg3 = perm_g.reshape(n_tiles, 1, tm_e)
    ps3 = perm_s.reshape(n_tiles, 1, tm_e)

    grid_spec = pltpu.PrefetchScalarGridSpec(
        num_scalar_prefetch=3,
        grid=(n_tiles,),
        in_specs=[
            pl.BlockSpec((1, 1, tm_e), lambda g, *_: (g, 0, 0), memory_space=pltpu.SMEM),
            pl.BlockSpec((1, 1, tm_e), lambda g, *_: (jnp.minimum(g + 1, n_tiles - 1), 0, 0),
                         memory_space=pltpu.SMEM),
            pl.BlockSpec((1, 1, tm_e), lambda g, *_: (g, 0, 0), memory_space=pltpu.SMEM),
            pl.BlockSpec(memory_space=pl.ANY),
            pl.BlockSpec((1, d, 2 * d_exp), lambda g, e0, e1, tv: (e0[g], 0, 0)),
            pl.BlockSpec((1, d, 2 * d_exp), lambda g, e0, e1, tv: (e1[g], 0, 0)),
            pl.BlockSpec((1, d_exp, d), lambda g, e0, e1, tv: (e0[g], 0, 0)),
            pl.BlockSpec((1, d_exp, d), lambda g, e0, e1, tv: (e1[g], 0, 0)),
        ],
        out_specs=pl.BlockSpec(memory_space=pl.ANY),
        scratch_shapes=[
            pltpu.VMEM((2, tm_e, dw), F32),
            pltpu.VMEM((2, tm_e, d), F32),
            pltpu.SemaphoreType.DMA((2,)),
            pltpu.SemaphoreType.DMA((2,)),
        ],
    )
    y = pl.pallas_call(
        functools.partial(_moe_kernel, tm_e=tm_e, d=d, d_exp=d_exp, n_tiles=n_tiles, t=t),
        grid_spec=grid_spec,
        out_shape=jax.ShapeDtypeStruct((t + 2 * tm_e, d), F32),
        compiler_params=_cparams(("arbitrary",)),
        name="moe_experts",
    )(te0, te1, tnv, pg3, pg3, ps3, xw, wgu, wgu, wd, wd)
    return y


def _final_kernel(x1_ref, y_ref, mod_ref, g_ref, b_ref, *o_refs, alpha, na):
    z = alpha * x1_ref[...] + mod_ref[0, 5:6, :] * y_ref[...]
    out = _ln_plain(z) * g_ref[...] + b_ref[...]
    if len(o_refs) == 1:
        o_refs[0][...] = out
    else:
        @pl.when(pl.program_id(0) < na)
        def _():
            o_refs[0][...] = out

        @pl.when(pl.program_id(0) >= na)
        def _():
            o_refs[1][...] = out


def _final_ln(x1, y, modl, cond_of_tile, ln_g, ln_b, alpha, tm, split_at=None):
    t, d = x1.shape
    row = lambda i: (i, 0)
    const = lambda i: (0, 0)
    if split_at is None:
        out_specs = pl.BlockSpec((tm, d), row)
        out_shape = jax.ShapeDtypeStruct((t, d), F32)
    else:
        out_specs = list(_split_specs(tm, d, split_at, 0))
        out_shape = [jax.ShapeDtypeStruct((split_at * tm, d), F32), jax.ShapeDtypeStruct((t - split_at * tm, d), F32)]
    return pl.pallas_call(
        functools.partial(_final_kernel, alpha=alpha, na=split_at),
        grid=(t // tm,),
        in_specs=[
            pl.BlockSpec((tm, d), row),
            pl.BlockSpec((tm, d), row),
            pl.BlockSpec((1, 6, d), lambda i: (cond_of_tile(i), 0, 0)),
            pl.BlockSpec((1, d), const),
            pl.BlockSpec((1, d), const),
        ],
        out_specs=out_specs,
        out_shape=out_shape,
        compiler_params=_cparams(("arbitrary",)),
        name="final_ln",
    )(x1, y, modl, ln_g, ln_b)


def _gate_columns():
    src_i = np.zeros((8 * N_PAIRS,), np.int32)
    src_f = np.zeros((8 * N_PAIRS,), np.int32)
    for hp in range(N_PAIRS):
        for dd in range(2):
            for hh in range(2):
                j = hp * 8 + dd * 2 + hh
                head = hp * 2 + hh
                src_i[j] = dd * M_HEADS + head
                src_f[j] = (2 + dd) * M_HEADS + head
    return src_i, src_f


def kernel(x_prompt, x_sample, cache_k, cache_v, state_C, state_n, state_m, c, c_ctx, w_ada, b_ada, w_in, b_gate,
           mh_norm_g, attn_sink, w_out, ln1_g, ln1_b, ln2_g, ln2_b, w_router, b_router, w_exp_gate, w_exp_up,
           w_exp_down):
    batch, seq, d = x_prompt.shape
    dec_batch, dec_seq, _ = x_sample.shape
    depth = w_in.shape[0]
    n_ctx = batch * seq
    n_lat = dec_batch * dec_seq
    t = n_ctx + n_lat
    alpha = (2 * depth) ** 0.25
    tm = 512
    tm_e = 256
    assert n_ctx % tm == 0 and dec_seq % tm == 0 and seq % CHUNK == 0 and dec_seq % CHUNK == 0
    assert n_ctx % dec_seq == 0 and n_ctx % WINDOW == 0 and t % tm_e == 0

    n_ctx_tiles = n_ctx // tm
    tiles_per_seq = dec_seq // tm

    def cond_of_tile(i):
        return jnp.where(i < n_ctx_tiles, 0, 1 + (i - n_ctx_tiles) // tiles_per_seq)

    n_cond = 1 + dec_batch
    cond_rows = -(-n_cond // 8) * 8
    cond = jnp.concatenate([c_ctx[None, :], c, jnp.zeros((cond_rows - n_cond, d), F32)], axis=0)
    mod = _ada_mod(cond, w_ada, b_ada).reshape(depth, cond_rows, 6, d)

    xs = (x_prompt.reshape(n_ctx, d), x_sample.reshape(n_lat, d), n_ctx_tiles, 0)

    src_i, src_f = _gate_columns()
    rope_tab = _rope_tables(dec_seq)
    wr = jnp.pad(w_router, ((0, 0), (0, LANES - N_EXPERTS))).astype(BF16)
    br = jnp.pad(b_router, (0, LANES - N_EXPERTS)).reshape(1, LANES)

    ks, vs, cs, ns, ms = [], [], [], [], []
    for l in range(depth):
        wl = w_in[l]
        kscale = M_DK ** -0.5
        w_main = jnp.concatenate(
            [wl[:, :KM_OFF], wl[:, KM_OFF:VM_OFF] * kscale, wl[:, VM_OFF:GATE_SRC_OFF],
             wl[:, GATE_SRC_OFF + 4 * M_HEADS:]], axis=1).astype(BF16)
        wgt = wl[:, GATE_SRC_OFF:GATE_SRC_OFF + 4 * M_HEADS]
        zpad = jnp.zeros((d, LANES - 8 * N_PAIRS), F32)
        w_gate = jnp.concatenate([wgt[:, src_i], zpad, wgt[:, src_f], zpad], axis=1).astype(BF16)
        bpad = jnp.zeros((LANES - 8 * N_PAIRS,), F32)
        bg = jnp.concatenate([b_gate[l][src_i], bpad, b_gate[l][src_f], bpad]).reshape(1, 2 * LANES)
        modl = mod[l]

        proj, gates, kv = _inproj(xs, t, modl, cond_of_tile, w_main, w_gate, bg, tm)
        gb, gu, gut = _gate_prep(gates, 1024)

        gain = mh_norm_g[l].reshape(1, M_HEADS * M_DV)
        sink = attn_sink[l]
        hm_c, c_new, n_new, m_new = _mlstm(proj, gb, gu, gut, gain, 0, batch, seq, None, True)
        at_c = _attn_ctx(proj, sink, batch, seq, 4)
        c0 = state_C[:, l].reshape(dec_batch, 2, N_PAIRS, PAIR_ROWS, M_DV)
        n0 = state_n[:, l].reshape(dec_batch, 2, N_PAIRS, PAIR_ROWS, 1)
        m0 = state_m[:, l].reshape(dec_batch * 2 * M_HEADS)
        (hm_l,) = _mlstm(proj, gb, gu, gut, gain, n_ctx, dec_batch, dec_seq, (c0, n0, m0), False)
        qr, kr = _rope(proj, rope_tab, n_ctx, n_lat, dec_seq, 512)
        ck = jnp.transpose(cache_k[:, l], (0, 2, 1, 3)).astype(BF16)
        cv = jnp.transpose(cache_v[:, l], (0, 2, 1, 3)).astype(BF16)
        at_l = _attn_lat(qr, kr, proj, ck, cv, sink, n_ctx, dec_batch, dec_seq)

        x1, xw, bkt = _outproj((hm_c, hm_l, at_c, at_l), w_out[l].astype(BF16), xs, t, modl, cond_of_tile,
                               ln1_g[l].reshape(1, d), ln1_b[l].reshape(1, d), wr, br, alpha, tm)
        wgu = jnp.concatenate([w_exp_gate[l], w_exp_up[l]], axis=-1).astype(BF16)
        wd = w_exp_down[l].astype(BF16)
        y = _moe(xw, bkt[0], wgu, wd, tm_e)
        last = l == depth - 1
        x_new = _final_ln(x1, y, modl, cond_of_tile, ln2_g[l].reshape(1, d), ln2_b[l].reshape(1, d), alpha, tm,
                          split_at=n_ctx_tiles if last else None)
        if last:
            y_ctx, y_lat = x_new
        else:
            xs = (x_new, x_new, n_ctx_tiles, n_ctx_tiles)

        ks.append(kv[:n_ctx, :A_KV_HEADS * A_HD].reshape(batch, seq, A_KV_HEADS, A_HD))
        vs.append(kv[:n_ctx, A_KV_HEADS * A_HD:].reshape(batch, seq, A_KV_HEADS, A_HD))
        cs.append(c_new.reshape(batch, 2, M_HEADS, M_DK, M_DV))
        ns.append(n_new.reshape(batch, 2, M_HEADS, M_DK))
        m4 = m_new[:, :, :4, 0].reshape(batch, N_PAIRS, 2, 2)
        ms.append(jnp.transpose(m4, (0, 2, 1, 3)).reshape(batch, 2, M_HEADS))

    y_prompt = y_ctx.reshape(batch, seq, d)
    y_sample = y_lat.reshape(dec_batch, dec_seq, d)
    return (y_prompt, y_sample, jnp.stack(ks, 1), jnp.stack(vs, 1), jnp.stack(cs, 1), jnp.stack(ns, 1),
            jnp.stack(ms, 1))
```

```python
import functools

import jax
import jax.numpy as jnp
import numpy as np
from jax import lax
from jax.experimental import pallas as pl
from jax.experimental.pallas import tpu as pltpu

F32 = jnp.float32
BF16 = jnp.bfloat16

M_HEADS = 8
M_DK = 64
M_DV = 128
IGATE_CAP = 15.0
MH_EPS = 1e-6
NEG_INIT = -1e30
A_HEADS = 8
A_KV_HEADS = 2
A_GROUP = A_HEADS // A_KV_HEADS
A_HD = 128
WINDOW = 128
GRID_W = 64
ROPE_BASE = 10000.0
N_EXPERTS = 16
N_GROUPS = 4
EXPERTS_PER_GROUP = N_EXPERTS // N_GROUPS
LN_EPS = 1e-5

LANES = 128
VMEM_LIMIT = 56 * 1024 * 1024

QM_OFF = 0
KM_OFF = M_HEADS * M_DK
VM_OFF = KM_OFF + M_HEADS * M_DK
OM_OFF = VM_OFF + M_HEADS * M_DV
QA_OFF = OM_OFF + M_HEADS * M_DV
KA_OFF = QA_OFF + A_HEADS * A_HD
VA_OFF = KA_OFF + A_KV_HEADS * A_HD
MAIN_COLS = VA_OFF + A_KV_HEADS * A_HD
GATE_SRC_OFF = OM_OFF + M_HEADS * M_DV
N_PAIRS = M_HEADS // 2
PAIR_ROWS = 2 * M_DK
CHUNK = 128
SUB_ROWS = 256
N_BUCKETS = N_GROUPS * 6
PAIR_LO = (0, 0, 0, 1, 1, 2)
PAIR_HI = (1, 2, 3, 2, 3, 3)


def _cparams(sem):
    return pltpu.CompilerParams(dimension_semantics=sem, vmem_limit_bytes=VMEM_LIMIT)


def _ln_plain(x):
    mu = jnp.mean(x, axis=-1, keepdims=True)
    xc = x - mu
    return xc * lax.rsqrt(jnp.mean(xc * xc, axis=-1, keepdims=True) + LN_EPS)


def _ada_kernel(c_ref, w_ref, b_ref, o_ref):
    c = c_ref[...]
    s = (c * jax.nn.sigmoid(c)).astype(BF16)
    o_ref[0] = jnp.dot(s, w_ref[0].astype(BF16), preferred_element_type=F32) + b_ref[0]


def _ada_mod(cond, w_ada, b_ada):
    depth, d, n = w_ada.shape
    rows = cond.shape[0]
    tn = 1024
    return pl.pallas_call(
        _ada_kernel,
        grid=(depth, n // tn),
        in_specs=[
            pl.BlockSpec((rows, d), lambda l, j: (0, 0)),
            pl.BlockSpec((1, d, tn), lambda l, j: (l, 0, j)),
            pl.BlockSpec((1, 1, tn), lambda l, j: (l, 0, j)),
        ],
        out_specs=pl.BlockSpec((1, rows, tn), lambda l, j: (l, 0, j)),
        out_shape=jax.ShapeDtypeStruct((depth, rows, n), F32),
        compiler_params=_cparams(("parallel", "parallel")),
        name="ada_mod",
    )(cond, w_ada, b_ada.reshape(depth, 1, n))


def _split_specs(tm, width, na, off_b):
    spec_a = pl.BlockSpec((tm, width), lambda i, *_: (jnp.minimum(i, na - 1), 0))
    spec_b = pl.BlockSpec((tm, width), lambda i, *_: (jnp.maximum(i - na, 0) + off_b, 0))
    return spec_a, spec_b


def _inproj_kernel(xa_ref, xb_ref, mod_ref, w_ref, wg_ref, bg_ref, proj_ref, gates_ref, kv_ref, h_scr, *, tn, na):
    is_a = pl.program_id(0) < na
    nj = MAIN_COLS // tn
    for s in range(xa_ref.shape[0] // SUB_ROWS):
        rows = slice(s * SUB_ROWS, (s + 1) * SUB_ROWS)
        x = jnp.where(is_a, xa_ref[rows, :], xb_ref[rows, :])
        hn = _ln_plain(x)
        h = hn * (1.0 + mod_ref[0, 1:2, :]) + mod_ref[0, 0:1, :]
        hb = h.astype(BF16)
        h_scr[rows, :] = hb
        gates_ref[rows, :] = jnp.dot(hb, wg_ref[...], preferred_element_type=F32) + bg_ref[...]
        for j in range(nj):
            cols = slice(j * tn, (j + 1) * tn)
            acc = jnp.dot(h_scr[rows, :], w_ref[:, cols], preferred_element_type=F32)
            proj_ref[rows, cols] = acc.astype(BF16)
            if j == nj - 1:
                kv_ref[rows, :] = acc


def _inproj(xs, t, modl, cond_of_tile, w_main, w_gate, b_gate, tm):
    x_a, x_b, na, off_b = xs
    d = x_a.shape[1]
    tn = MAIN_COLS - KA_OFF
    assert MAIN_COLS % tn == 0
    spec_a, spec_b = _split_specs(tm, d, na, off_b)
    const = lambda i: (0, 0)
    resident = pl.Buffered(1)
    return pl.pallas_call(
        functools.partial(_inproj_kernel, tn=tn, na=na),
        grid=(t // tm,),
        in_specs=[
            spec_a,
            spec_b,
            pl.BlockSpec((1, 6, d), lambda i: (cond_of_tile(i), 0, 0)),
            pl.BlockSpec((d, MAIN_COLS), const, pipeline_mode=resident),
            pl.BlockSpec((d, 2 * LANES), const, pipeline_mode=resident),
            pl.BlockSpec((1, 2 * LANES), const, pipeline_mode=resident),
        ],
        out_specs=[
            pl.BlockSpec((tm, MAIN_COLS), lambda i: (i, 0)),
            pl.BlockSpec((tm, 2 * LANES), lambda i: (i, 0)),
            pl.BlockSpec((tm, tn), lambda i: (i, 0)),
        ],
        out_shape=[
            jax.ShapeDtypeStruct((t, MAIN_COLS), BF16),
            jax.ShapeDtypeStruct((t, 2 * LANES), F32),
            jax.ShapeDtypeStruct((t, tn), F32),
        ],
        scratch_shapes=[pltpu.VMEM((tm, d), BF16)],
        compiler_params=_cparams(("parallel",)),
        name="inproj",
    )(x_a, x_b, modl, w_main, w_gate, b_gate)


def _gate_kernel(g_ref, gb_ref, gu_ref, gut_ref, *, tg):
    gi = g_ref[:, :LANES]
    gf = g_ref[:, LANES:]
    ig = IGATE_CAP * jnp.tanh(gi / IGATE_CAP)
    lf = jax.nn.log_sigmoid(gf)
    lane = lax.broadcasted_iota(jnp.int32, (1, LANES), 1)
    is_fwd = (lane % 8) < 2
    s_i = lax.broadcasted_iota(jnp.int32, (CHUNK, CHUNK), 0)
    r_i = lax.broadcasted_iota(jnp.int32, (CHUNK, CHUNK), 1)
    tri_lo = (r_i <= s_i).astype(F32)
    tri_hi = (r_i >= s_i).astype(F32)
    for c in range(tg // CHUNK):
        rows = slice(c * CHUNK, (c + 1) * CHUNK)
        lfc = lf[rows]
        pre = jnp.dot(tri_lo, lfc, preferred_element_type=F32, precision=lax.Precision.HIGHEST)
        suf = jnp.dot(tri_hi, lfc, preferred_element_type=F32, precision=lax.Precision.HIGHEST)
        b = jnp.where(is_fwd, pre, suf)
        u = ig[rows] - b
        gb_ref[rows, :] = b
        gu_ref[rows, :] = u
        gut_ref[:, rows] = u.T[: 8 * N_PAIRS]


def _gate_prep(gates, tg):
    t = gates.shape[0]
    return pl.pallas_call(
        functools.partial(_gate_kernel, tg=tg),
        grid=(t // tg,),
        in_specs=[pl.BlockSpec((tg, 2 * LANES), lambda i: (i, 0))],
        out_specs=[
            pl.BlockSpec((tg, LANES), lambda i: (i, 0)),
            pl.BlockSpec((tg, LANES), lambda i: (i, 0)),
            pl.BlockSpec((8 * N_PAIRS, tg), lambda i: (0, i)),
        ],
        out_shape=[
            jax.ShapeDtypeStruct((t, LANES), F32),
            jax.ShapeDtypeStruct((t, LANES), F32),
            jax.ShapeDtypeStruct((8 * N_PAIRS, t), F32),
        ],
        compiler_params=_cparams(("parallel",)),
        name="gate_prep",
    )(gates)


def _mlstm_kernel(*refs, seq, has_state, emit_state):
    it = iter(refs)
    q_ref, k_ref, v_ref, om_ref, gb_ref, gu_ref, gut_ref, gain_ref = (next(it) for _ in range(8))
    if has_state:
        c0_ref, n0_ref, m0_ref = next(it), next(it), next(it)
    out_ref = next(it)
    if emit_state:
        cout_ref, nout_ref, mout_ref = next(it), next(it), next(it)
    h_scr, cst_scr = next(it), next(it)

    b_id = pl.program_id(0)
    hp = pl.program_id(1)
    nc = seq // CHUNK
    L = CHUNK

    lane = lax.broadcasted_iota(jnp.int32, (1, LANES), 1)
    row128 = lax.broadcasted_iota(jnp.int32, (PAIR_ROWS, 1), 0)
    s_i = lax.broadcasted_iota(jnp.int32, (L, L), 0)
    r_i = lax.broadcasted_iota(jnp.int32, (L, L), 1)
    e0row = (lane == 0).astype(F32)
    e0blk = jnp.broadcast_to(e0row, (L, LANES)).astype(BF16)

    def pick_lane(x, j):
        return jnp.sum(jnp.where(lane == j, x, 0.0), axis=1, keepdims=True)

    m_init = []
    for d in range(2):
        for hh in range(2):
            if has_state:
                rowmask = (row128 // M_DK) == hh
                cst_scr[d * 2 + hh, :, :LANES] = jnp.where(rowmask, c0_ref[0, d, 0], 0.0)
                cst_scr[d * 2 + hh, :, LANES:] = jnp.where(rowmask, n0_ref[0, d, 0], 0.0) * e0row
                m0 = m0_ref[b_id * (2 * M_HEADS) + d * M_HEADS + hp * 2 + hh]
                m_init.append(jnp.full((1, 1), m0, F32))
            else:
                cst_scr[d * 2 + hh] = jnp.zeros((PAIR_ROWS, 2 * LANES), F32)
                m_init.append(jnp.full((1, 1), NEG_INIT, F32))

    def body(t, ms):
        new_ms = []
        for d in range(2):
            tri = (r_i <= s_i) if d == 0 else (r_i >= s_i)
            c = t if d == 0 else nc - 1 - t
            r0 = pl.multiple_of(c * L, L)
            q2 = q_ref[pl.ds(r0, L), :]
            k2 = k_ref[pl.ds(r0, L), :]
            gbc = gb_ref[pl.ds(r0, L), :]
            guc = gu_ref[pl.ds(r0, L), :]
            gend = gb_ref[pl.ds(r0 + (L - 1 if d == 0 else 0), 1), :]
            for hh in range(2):
                si = d * 2 + hh
                jl = hp * 8 + d * 2 + hh
                lm = (lane // M_DK) == hh
                qh = jnp.where(lm, q2, jnp.zeros_like(q2))
                kh = jnp.where(lm, k2, jnp.zeros_like(k2))
                vh = v_ref[pl.ds(r0, L), hh * LANES:(hh + 1) * LANES]
                vext = jnp.concatenate([vh, e0blk], axis=1)
                urow = gut_ref[d * 2 + hh:d * 2 + hh + 1, pl.ds(r0, L)]
                ucol = pick_lane(guc, jl)
                bcol = pick_lane(gbc, jl)
                g = pick_lane(gend, jl)
                m = ms[si]
                umat = jnp.where(tri, urow, -jnp.inf)
                cmu = jnp.max(umat, axis=1, keepdims=True)
                mm = jnp.maximum(m, cmu)
                w = jnp.exp(umat - mm)
                sqk = lax.dot_general(qh, kh, (((1,), (1,)), ((), ())), preferred_element_type=F32)
                p = (sqk * w).astype(BF16)
                intra = jnp.dot(p, vext, preferred_element_type=F32)
                cs = cst_scr[si]
                inter = jnp.dot(qh, cs.astype(BF16), preferred_element_type=F32)
                nd = intra + jnp.exp(m - mm) * inter
                num = nd[:, :LANES]
                den = nd[:, LANES:LANES + 1]
                hv = num / jnp.maximum(jnp.abs(den), jnp.exp(-bcol - mm))
                h_scr[d, pl.ds(r0, L), hh * LANES:(hh + 1) * LANES] = hv
                maxu = jnp.max(urow, axis=1, keepdims=True)
                m_new = g + jnp.maximum(m, maxu)
                wa = jnp.exp(g + ucol - m_new)
                wc = jnp.exp(g + m - m_new)
                kw = (kh.astype(F32) * wa).astype(BF16)
                upd = lax.dot_general(kw, vext, (((0,), (0,)), ((), ())), preferred_element_type=F32)
                cst_scr[si] = wc * cs + upd
                new_ms.append(m_new)
        return tuple(new_ms)

    ms_fin = lax.fori_loop(0, nc, body, tuple(m_init))

    def combine(c, carry):
        r0 = pl.multiple_of(c * L, L)
        for hh in range(2):
            cols = slice(hh * LANES, (hh + 1) * LANES)
            tot = h_scr[0, pl.ds(r0, L), cols] + h_scr[1, pl.ds(r0, L), cols]
            ms2 = jnp.mean(tot * tot, axis=1, keepdims=True)
            y = tot * lax.rsqrt(ms2 + MH_EPS) * gain_ref[:, cols]
            y = y * jax.nn.sigmoid(om_ref[pl.ds(r0, L), cols].astype(F32))
            out_ref[pl.ds(r0, L), cols] = y.astype(BF16)
        return carry

    lax.fori_loop(0, nc, combine, 0)

    if emit_state:
        for d in range(2):
            cout_ref[0, d, 0] = cst_scr[d * 2, :, :LANES] + cst_scr[d * 2 + 1, :, :LANES]
            nout_ref[0, d, 0] = cst_scr[d * 2, :, LANES:LANES + 1] + cst_scr[d * 2 + 1, :, LANES:LANES + 1]
            for hh in range(2):
                mout_ref[0, 0, d * 2 + hh:d * 2 + hh + 1, :] = jnp.broadcast_to(ms_fin[d * 2 + hh], (1, LANES))
        mout_ref[0, 0, 4:8, :] = jnp.zeros((4, LANES), F32)


def _mlstm(proj, gb, gu, gut, gain, row_off, nseq, seq, state0, emit_state):
    assert row_off % seq == 0
    rb = row_off // seq
    has_state = state0 is not None
    kernel = functools.partial(_mlstm_kernel, seq=seq, has_state=has_state, emit_state=emit_state)
    in_specs = [
        pl.BlockSpec((seq, PAIR_ROWS), lambda b, h: (rb + b, QM_OFF // PAIR_ROWS + h)),
        pl.BlockSpec((seq, PAIR_ROWS), lambda b, h: (rb + b, KM_OFF // PAIR_ROWS + h)),
        pl.BlockSpec((seq, 2 * M_DV), lambda b, h: (rb + b, VM_OFF // (2 * M_DV) + h)),
        pl.BlockSpec((seq, 2 * M_DV), lambda b, h: (rb + b, OM_OFF // (2 * M_DV) + h)),
        pl.BlockSpec((seq, LANES), lambda b, h: (rb + b, 0)),
        pl.BlockSpec((seq, LANES), lambda b, h: (rb + b, 0)),
        pl.BlockSpec((8, seq), lambda b, h: (h, rb + b)),
        pl.BlockSpec((1, 2 * M_DV), lambda b, h: (0, h)),
    ]
    args = [proj, proj, proj, proj, gb, gu, gut, gain]
    if has_state:
        c0, n0, m0 = state0
        in_specs += [
            pl.BlockSpec((1, 2, 1, PAIR_ROWS, M_DV), lambda b, h: (b, 0, h, 0, 0)),
            pl.BlockSpec((1, 2, 1, PAIR_ROWS, 1), lambda b, h: (b, 0, h, 0, 0)),
            pl.BlockSpec(memory_space=pltpu.SMEM),
        ]
        args += [c0, n0, m0]
    out_specs = [pl.BlockSpec((seq, 2 * M_DV), lambda b, h: (b, h))]
    out_shape = [jax.ShapeDtypeStruct((nseq * seq, M_HEADS * M_DV), BF16)]
    if emit_state:
        out_specs += [
            pl.BlockSpec((1, 2, 1, PAIR_ROWS, M_DV), lambda b, h: (b, 0, h, 0, 0)),
            pl.BlockSpec((1, 2, 1, PAIR_ROWS, 1), lambda b, h: (b, 0, h, 0, 0)),
            pl.BlockSpec((1, 1, 8, LANES), lambda b, h: (b, h, 0, 0)),
        ]
        out_shape += [
            jax.ShapeDtypeStruct((nseq, 2, N_PAIRS, PAIR_ROWS, M_DV), F32),
            jax.ShapeDtypeStruct((nseq, 2, N_PAIRS, PAIR_ROWS, 1), F32),
            jax.ShapeDtypeStruct((nseq, N_PAIRS, 8, LANES), F32),
        ]
    return pl.pallas_call(
        kernel,
        grid=(nseq, N_PAIRS),
        in_specs=in_specs,
        out_specs=out_specs,
        out_shape=out_shape,
        scratch_shapes=[pltpu.VMEM((2, seq, 2 * M_DV), F32), pltpu.VMEM((4, PAIR_ROWS, 2 * LANES), F32)],
        compiler_params=_cparams(("parallel", "parallel")),
        name="mlstm",
    )(*args)


def _attn_ctx_kernel(sink_ref, q_ref, k_ref, v_ref, o_ref, *, seq, nb):
    kvh = pl.program_id(1)
    scale = A_HD ** -0.5
    for s in range(nb):
        rows = slice(s * seq, (s + 1) * seq)
        k = k_ref[rows, :]
        v = v_ref[rows, :]
        for g in range(A_GROUP):
            cols = slice(g * A_HD, (g + 1) * A_HD)
            q = q_ref[rows, cols]
            sc = lax.dot_general(q, k, (((1,), (1,)), ((), ())), preferred_element_type=F32) * scale
            sk = sink_ref[kvh * A_GROUP + g]
            mx = jnp.maximum(jnp.max(sc, axis=1, keepdims=True), sk)
            p = jnp.exp(sc - mx)
            den = jnp.sum(p, axis=1, keepdims=True) + jnp.exp(sk - mx)
            o = jnp.dot(p.astype(BF16), v, preferred_element_type=F32) / den
            o_ref[rows, cols] = o.astype(BF16)


def _attn_ctx(proj, sink, nseq, seq, nb):
    gw = A_GROUP * A_HD
    return pl.pallas_call(
        functools.partial(_attn_ctx_kernel, seq=seq, nb=nb),
        grid=(nseq // nb, A_KV_HEADS),
        in_specs=[
            pl.BlockSpec(memory_space=pltpu.SMEM),
            pl.BlockSpec((nb * seq, gw), lambda b, h: (b, QA_OFF // gw + h)),
            pl.BlockSpec((nb * seq, A_HD), lambda b, h: (b, KA_OFF // A_HD + h)),
            pl.BlockSpec((nb * seq, A_HD), lambda b, h: (b, VA_OFF // A_HD + h)),
        ],
        out_specs=pl.BlockSpec((nb * seq, gw), lambda b, h: (b, h)),
        out_shape=jax.ShapeDtypeStruct((nseq * seq, A_HEADS * A_HD), BF16),
        compiler_params=_cparams(("parallel", "parallel")),
        name="attn_ctx",
    )(sink, proj, proj, proj)


def _rope_kernel(q_ref, k_ref, cos_ref, sa_ref, sb_ref, qo_ref, ko_ref):
    cos = cos_ref[...]
    sa = sa_ref[...]
    sb = sb_ref[...]

    def rot(x):
        return x * cos + pltpu.roll(x, LANES - A_HD // 4, 1) * sa + pltpu.roll(x, A_HD // 4, 1) * sb

    for h in range(A_HEADS):
        cols = slice(h * A_HD, (h + 1) * A_HD)
        qo_ref[:, cols] = rot(q_ref[:, cols].astype(F32)).astype(BF16)
    for h in range(A_KV_HEADS):
        cols = slice(h * A_HD, (h + 1) * A_HD)
        ko_ref[:, cols] = rot(k_ref[:, cols].astype(F32)).astype(BF16)


def _rope(proj, tables, row_off, nrows, seq, tr):
    cos, sa, sb = tables
    rb = row_off // tr
    nps = seq // tr
    qw = A_HEADS * A_HD
    kw = A_KV_HEADS * A_HD
    tab = pl.BlockSpec((tr, A_HD), lambda i: (i % nps, 0))
    return pl.pallas_call(
        _rope_kernel,
        grid=(nrows // tr,),
        in_specs=[
            pl.BlockSpec((tr, qw), lambda i: (rb + i, QA_OFF // qw)),
            pl.BlockSpec((tr, kw), lambda i: (rb + i, KA_OFF // kw)),
            tab, tab, tab,
        ],
        out_specs=[pl.BlockSpec((tr, qw), lambda i: (i, 0)), pl.BlockSpec((tr, kw), lambda i: (i, 0))],
        out_shape=[jax.ShapeDtypeStruct((nrows, qw), BF16), jax.ShapeDtypeStruct((nrows, kw), BF16)],
        compiler_params=_cparams(("parallel",)),
        name="rope",
    )(proj, proj, cos, sa, sb)


def _rope_tables(seq):
    half = A_HD // 2
    pos = np.arange(seq)
    row = (pos // GRID_W).astype(np.float32)
    col = (pos % GRID_W).astype(np.float32)
    inv = (ROPE_BASE ** (-np.arange(0, half, 2, dtype=np.float32) / half)).astype(np.float32)
    ang_r = row[:, None] * inv[None, :]
    ang_c = col[:, None] * inv[None, :]
    ang = np.concatenate([ang_r, ang_r, ang_c, ang_c], axis=1).astype(np.float32)
    cos = np.cos(ang).astype(np.float32)
    sin = np.sin(ang).astype(np.float32)
    first = (np.arange(A_HD) % half) < (half // 2)
    sa = np.where(first[None, :], -sin, 0.0).astype(np.float32)
    sb = np.where(first[None, :], 0.0, sin).astype(np.float32)
    return jnp.asarray(cos), jnp.asarray(sa), jnp.asarray(sb)


def _attn_lat_kernel(sink_ref, q_ref, kp_ref, kc_ref, kn_ref, vp_ref, vc_ref, vn_ref, ck_ref, cv_ref, o_ref,
                     *, nblk):
    i = pl.program_id(1)
    qb = WINDOW
    scale = A_HD ** -0.5
    r = lax.broadcasted_iota(jnp.int32, (A_GROUP * qb, 3 * qb), 0) % qb
    c = lax.broadcasted_iota(jnp.int32, (A_GROUP * qb, 3 * qb), 1)
    c_lo = jnp.where(i > 0, 0, qb)
    c_hi = jnp.where(i < nblk - 1, 3 * qb, 2 * qb)
    valid = (c >= r) & (c <= r + 2 * WINDOW) & (c >= c_lo) & (c < c_hi)
    hrow = lax.broadcasted_iota(jnp.int32, (A_GROUP * qb, 1), 0) // qb
    for kvh in range(A_KV_HEADS):
        kc = slice(kvh * A_HD, (kvh + 1) * A_HD)
        q0 = kvh * A_GROUP * A_HD
        q = jnp.concatenate([q_ref[:, q0 + g * A_HD:q0 + (g + 1) * A_HD] for g in range(A_GROUP)], axis=0)
        kwin = jnp.concatenate([kp_ref[:, kc], kc_ref[:, kc], kn_ref[:, kc]], axis=0)
        vwin = jnp.concatenate([vp_ref[:, kc], vc_ref[:, kc], vn_ref[:, kc]], axis=0)
        s_lat = lax.dot_general(q, kwin, (((1,), (1,)), ((), ())), preferred_element_type=F32) * scale
        s_ctx = lax.dot_general(q, ck_ref[0, kvh], (((1,), (1,)), ((), ())), preferred_element_type=F32) * scale
        s_lat = jnp.where(valid, s_lat, NEG_INIT)
        sk = jnp.zeros((A_GROUP * qb, 1), F32)
        for g in range(A_GROUP):
            sk = jnp.where(hrow == g, sink_ref[kvh * A_GROUP + g], sk)
        mx = jnp.maximum(jnp.maximum(jnp.max(s_lat, axis=1, keepdims=True),
                                     jnp.max(s_ctx, axis=1, keepdims=True)), sk)
        p_lat = jnp.exp(s_lat - mx)
        p_ctx = jnp.exp(s_ctx - mx)
        den = jnp.sum(p_lat, axis=1, keepdims=True) + jnp.sum(p_ctx, axis=1, keepdims=True) + jnp.exp(sk - mx)
        o = (jnp.dot(p_lat.astype(BF16), vwin, preferred_element_type=F32)
             + jnp.dot(p_ctx.astype(BF16), cv_ref[0, kvh], preferred_element_type=F32)) / den
        for g in range(A_GROUP):
            o_ref[:, q0 + g * A_HD:q0 + (g + 1) * A_HD] = o[g * qb:(g + 1) * qb].astype(BF16)


def _attn_lat(qr, kr, proj, ck, cv, sink, row_off, nseq, seq):
    qb = WINDOW
    nblk = seq // qb
    qw = A_HEADS * A_HD
    kw = A_KV_HEADS * A_HD
    rb = row_off // qb
    p_len = ck.shape[2]
    prev = lambda b, i: b * nblk + jnp.maximum(i - 1, 0)
    cur = lambda b, i: b * nblk + i
    nxt = lambda b, i: b * nblk + jnp.minimum(i + 1, nblk - 1)
    vcol = VA_OFF // kw
    return pl.pallas_call(
        functools.partial(_attn_lat_kernel, nblk=nblk),
        grid=(nseq, nblk),
        in_specs=[
            pl.BlockSpec(memory_space=pltpu.SMEM),
            pl.BlockSpec((qb, qw), lambda b, i: (cur(b, i), 0)),
            pl.BlockSpec((qb, kw), lambda b, i: (prev(b, i), 0)),
            pl.BlockSpec((qb, kw), lambda b, i: (cur(b, i), 0)),
            pl.BlockSpec((qb, kw), lambda b, i: (nxt(b, i), 0)),
            pl.BlockSpec((qb, kw), lambda b, i: (rb + prev(b, i), vcol)),
            pl.BlockSpec((qb, kw), lambda b, i: (rb + cur(b, i), vcol)),
            pl.BlockSpec((qb, kw), lambda b, i: (rb + nxt(b, i), vcol)),
            pl.BlockSpec((1, A_KV_HEADS, p_len, A_HD), lambda b, i: (b, 0, 0, 0)),
            pl.BlockSpec((1, A_KV_HEADS, p_len, A_HD), lambda b, i: (b, 0, 0, 0)),
        ],
        out_specs=pl.BlockSpec((qb, qw), lambda b, i: (cur(b, i), 0)),
        out_shape=jax.ShapeDtypeStruct((nseq * seq, qw), BF16),
        compiler_params=_cparams(("parallel", "parallel")),
        name="attn_lat",
    )(sink, qr, kr, kr, kr, proj, proj, proj, ck, cv)


def _outproj_kernel(*refs, alpha, na):
    tm = refs[0].shape[0]
    for s in range(tm // SUB_ROWS):
        _outproj_rows(slice(s * SUB_ROWS, (s + 1) * SUB_ROWS), *refs, alpha=alpha, na=na)


def _outproj_rows(rows, mmc_ref, mml_ref, mac_ref, mal_ref, w_ref, xa_ref, xb_ref, mod_ref, lng_ref, lnb_ref, wr_ref,
                  br_ref, x1_ref, xw_ref, bkt_ref, *, alpha, na):
    is_a = pl.program_id(0) < na
    d = xa_ref.shape[1]
    half = w_ref.shape[0] // 2
    mix_m = jnp.where(is_a, mmc_ref[rows, :], mml_ref[rows, :])
    mix_a = jnp.where(is_a, mac_ref[rows, :], mal_ref[rows, :])
    f = (jnp.dot(mix_m, w_ref[:half, :], preferred_element_type=F32)
         + jnp.dot(mix_a, w_ref[half:, :], preferred_element_type=F32))
    x = jnp.where(is_a, xa_ref[rows, :], xb_ref[rows, :])
    z = alpha * x + mod_ref[0, 2:3, :] * f
    x1 = _ln_plain(z) * lng_ref[...] + lnb_ref[...]
    x1_ref[rows, :] = x1
    h2 = _ln_plain(x1) * (1.0 + mod_ref[0, 4:5, :]) + mod_ref[0, 3:4, :]
    xw_ref[rows, :d] = h2
    logits = jnp.dot(h2.astype(BF16), wr_ref[...], preferred_element_type=F32)
    scores = jax.nn.sigmoid(logits)
    sel = scores + br_ref[...]
    sc_t = scores.T
    sel_t = sel.T
    sv = [sel_t[e:e + 1, :] for e in range(N_EXPERTS)]
    cv = [sc_t[e:e + 1, :] for e in range(N_EXPERTS)]
    gs = []
    for g in range(N_GROUPS):
        v = sv[4 * g:4 * g + 4]
        best = None
        for a, b in zip(PAIR_LO, PAIR_HI):
            ps = v[a] + v[b]
            best = ps if best is None else jnp.maximum(best, ps)
        gs.append(best)
    gmax = jnp.maximum(jnp.maximum(gs[0], gs[1]), jnp.maximum(gs[2], gs[3]))
    grp = jnp.full(gmax.shape, N_GROUPS - 1, jnp.int32)
    for g in range(N_GROUPS - 2, -1, -1):
        grp = jnp.where(gs[g] == gmax, g, grp)
    def pick(vals, k):
        out = vals[k]
        for g in range(1, N_GROUPS):
            out = jnp.where(grp == g, vals[4 * g + k], out)
        return out
    gv = [pick(sv, k) for k in range(EXPERTS_PER_GROUP)]
    gc = [pick(cv, k) for k in range(EXPERTS_PER_GROUP)]
    m1 = jnp.maximum(jnp.maximum(gv[0], gv[1]), jnp.maximum(gv[2], gv[3]))
    k0 = jnp.full(m1.shape, 3, jnp.int32)
    for k in range(2, -1, -1):
        k0 = jnp.where(gv[k] == m1, k, k0)
    gv2 = [jnp.where(k0 == k, -jnp.inf, gv[k]) for k in range(4)]
    m2 = jnp.maximum(jnp.maximum(gv2[0], gv2[1]), jnp.maximum(gv2[2], gv2[3]))
    k1 = jnp.full(m1.shape, 3, jnp.int32)
    for k in range(2, -1, -1):
        k1 = jnp.where(gv2[k] == m2, k, k1)
    lo = jnp.minimum(k0, k1)
    hi = jnp.maximum(k0, k1)
    pair = jnp.where(lo == 0, hi - 1, jnp.where(lo == 1, hi + 1, 5))
    bkt_ref[:, rows] = grp * 6 + pair
    s_lo = jnp.where(lo == 0, gc[0], jnp.where(lo == 1, gc[1], gc[2]))
    s_hi = jnp.where(hi == 1, gc[1], jnp.where(hi == 2, gc[2], gc[3]))
    tot = s_lo + s_hi
    wrow = lax.broadcasted_iota(jnp.int32, (LANES, tot.shape[1]), 0)
    w_t = jnp.where(wrow == 0, s_lo / tot, jnp.where(wrow == 1, s_hi / tot, 0.0))
    xw_ref[rows, d:] = w_t.T


def _outproj(mixes, w_out, xs, t, modl, cond_of_tile, ln_g, ln_b, w_router, b_router, alpha, tm):
    x_a, x_b, na, off_b = xs
    d = x_a.shape[1]
    hw = mixes[0].shape[1]
    row = lambda i: (i, 0)
    const = lambda i: (0, 0)
    mix_a, mix_b = _split_specs(tm, hw, na, 0)
    x_sa, x_sb = _split_specs(tm, d, na, off_b)
    return pl.pallas_call(
        functools.partial(_outproj_kernel, alpha=alpha, na=na),
        grid=(t // tm,),
        in_specs=[
            mix_a, mix_b, mix_a, mix_b,
            pl.BlockSpec((2 * hw, d), const, pipeline_mode=pl.Buffered(1)),
            x_sa, x_sb,
            pl.BlockSpec((1, 6, d), lambda i: (cond_of_tile(i), 0, 0)),
            pl.BlockSpec((1, d), const),
            pl.BlockSpec((1, d), const),
            pl.BlockSpec((d, LANES), const),
            pl.BlockSpec((1, LANES), const),
        ],
        out_specs=[
            pl.BlockSpec((tm, d), row),
            pl.BlockSpec((tm, d + LANES), row),
            pl.BlockSpec((1, tm), lambda i: (0, i)),
        ],
        out_shape=[
            jax.ShapeDtypeStruct((t, d), F32),
            jax.ShapeDtypeStruct((t, d + LANES), F32),
            jax.ShapeDtypeStruct((1, t), jnp.int32),
        ],
        compiler_params=_cparams(("parallel",)),
        name="outproj_route",
    )(mixes[0], mixes[1], mixes[2], mixes[3], w_out, x_a, x_b, modl, ln_g, ln_b, w_router, b_router)


ROW_DMA_UNROLL = 8


def _moe_kernel(te0_ref, te1_ref, tnv_ref, pg_ref, pgn_ref, ps_ref, x_hbm, wgu0_ref, wgu1_ref, wd0_ref, wd1_ref,
                y_hbm, xbuf, ybuf, gsem, ssem, *, tm_e, d, d_exp, n_tiles, t):
    g = pl.program_id(0)
    nv = tnv_ref[g]
    slot = lax.rem(g, 2)
    oslot = 1 - slot

    def start_rows(copy_of_row, idx_ref):
        def body(i, c):
            for u in range(ROW_DMA_UNROLL):
                r = i * ROW_DMA_UNROLL + u
                copy_of_row(r, idx_ref[0, 0, r]).start(priority=u % 2)
            return c
        lax.fori_loop(0, tm_e // ROW_DMA_UNROLL, body, 0)

    def start_gather(sl, idx_ref):
        start_rows(lambda r, idx: pltpu.make_async_copy(x_hbm.at[pl.ds(idx, 1)], xbuf.at[sl, pl.ds(r, 1)],
                                                        gsem.at[sl]), idx_ref)

    def start_scatter(sl, idx_ref):
        start_rows(lambda r, idx: pltpu.make_async_copy(ybuf.at[sl, pl.ds(r, 1)], y_hbm.at[pl.ds(idx, 1)],
                                                        ssem.at[sl]), idx_ref)

    def wait_gather(sl):
        pltpu.make_async_copy(x_hbm.at[pl.ds(0, tm_e)], xbuf.at[sl], gsem.at[sl]).wait()

    def wait_scatter(sl):
        pltpu.make_async_copy(ybuf.at[sl], y_hbm.at[pl.ds(0, tm_e)], ssem.at[sl]).wait()

    @pl.when(g == 0)
    def _():
        ybuf[1] = jnp.zeros((tm_e, d), F32)
        for k in range(2):
            pltpu.make_async_copy(ybuf.at[1], y_hbm.at[pl.ds(t + k * tm_e, tm_e)], ssem.at[1]).start()
        for k in range(2):
            wait_scatter(1)
        start_gather(0, pg_ref)

    @pl.when(nv > 0)
    def _():
        g_next = jnp.minimum(g + 1, n_tiles - 1)
        has_next = jnp.logical_and(g + 1 < n_tiles, tnv_ref[g_next] > 0)

        @pl.when(has_next)
        def _():
            start_gather(oslot, pgn_ref)

        wait_gather(slot)
        xb = xbuf[slot, :, :d].astype(BF16)
        y = None
        for k, (wgu_ref, wd_ref) in enumerate(((wgu0_ref, wd0_ref), (wgu1_ref, wd1_ref))):
            gu = jnp.dot(xb, wgu_ref[0], preferred_element_type=F32)
            gt = gu[:, :d_exp]
            a = (gt * jax.nn.sigmoid(gt)) * gu[:, d_exp:]
            ye = jnp.dot(a.astype(BF16), wd_ref[0], preferred_element_type=F32)
            ye = xbuf[slot, :, d + k:d + k + 1] * ye
            y = ye if y is None else y + ye
        ybuf[slot] = y
        start_scatter(slot, ps_ref)

        @pl.when(g > 0)
        def _():
            wait_scatter(oslot)

        @pl.when(jnp.logical_not(has_next))
        def _():
            wait_scatter(slot)


def _moe(xw, bkt, wgu, wd, tm_e):
    t, dw = xw.shape
    d = dw - LANES
    d_exp = wd.shape[1]
    n_tiles = t // tm_e + N_BUCKETS
    n_rows = n_tiles * tm_e
    ids = jnp.arange(N_BUCKETS, dtype=jnp.int32)
    onehot = (bkt[:, None] == ids[None, :]).astype(jnp.int32)
    counts = jnp.sum(onehot, axis=0)
    tiles_b = (counts + tm_e - 1) // tm_e
    tile_end = jnp.cumsum(tiles_b)
    row_start = (tile_end - tiles_b) * tm_e
    blk = tm_e
    oh3 = onehot.astype(F32).reshape(t // blk, blk, N_BUCKETS)
    tri = (jnp.arange(blk)[:, None] >= jnp.arange(blk)[None, :]).astype(F32)
    local = jnp.einsum("ij,bjk->bik", tri, oh3)
    before = jnp.cumsum(local[:, -1, :], axis=0) - local[:, -1, :]
    running = (local + before[:, None, :]).astype(jnp.int32).reshape(t, N_BUCKETS)
    rank = jnp.take_along_axis(running, bkt[:, None], axis=1)[:, 0] - 1
    pos = row_start[bkt] + rank
    tile_ids = jnp.arange(n_tiles, dtype=jnp.int32)
    used = tile_end[-1]
    tb = jnp.sum((tile_end[None, :] <= jnp.minimum(tile_ids, used - 1)[:, None]).astype(jnp.int32), axis=1)
    tile_in_b = tile_ids - (tile_end - tiles_b)[tb]
    tnv = jnp.where(tile_ids < used, jnp.clip(counts[tb] - tile_in_b * tm_e, 0, tm_e), 0).astype(jnp.int32)
    lo = jnp.asarray(PAIR_LO, jnp.int32)
    hi = jnp.asarray(PAIR_HI, jnp.int32)
    te0 = (tb // 6) * EXPERTS_PER_GROUP + lo[tb % 6]
    te1 = (tb // 6) * EXPERTS_PER_GROUP + hi[tb % 6]
    tok = jnp.arange(t, dtype=jnp.int32)
    slot_row = jnp.arange(n_rows, dtype=jnp.int32)
    dump = t + ((slot_row // tm_e) % 2) * tm_e + slot_row % tm_e
    perm_s = dump.at[pos].set(tok)
    perm_g = jnp.where(perm_s < t, perm_s, 0)
    pg3 = perm_g.reshape(n_tiles, 1, tm_e)
    ps3 = perm_s.reshape(n_tiles, 1, tm_e)

    grid_spec = pltpu.PrefetchScalarGridSpec(
        num_scalar_prefetch=3,
        grid=(n_tiles,),
        in_specs=[
            pl.BlockSpec((1, 1, tm_e), lambda g, *_: (g, 0, 0), memory_space=pltpu.SMEM),
            pl.BlockSpec((1, 1, tm_e), lambda g, *_: (jnp.minimum(g + 1, n_tiles - 1), 0, 0),
                         memory_space=pltpu.SMEM),
            pl.BlockSpec((1, 1, tm_e), lambda g, *_: (g, 0, 0), memory_space=pltpu.SMEM),
            pl.BlockSpec(memory_space=pl.ANY),
            pl.BlockSpec((1, d, 2 * d_exp), lambda g, e0, e1, tv: (e0[g], 0, 0)),
            pl.BlockSpec((1, d, 2 * d_exp), lambda g, e0, e1, tv: (e1[g], 0, 0)),
            pl.BlockSpec((1, d_exp, d), lambda g, e0, e1, tv: (e0[g], 0, 0)),
            pl.BlockSpec((1, d_exp, d), lambda g, e0, e1, tv: (e1[g], 0, 0)),
        ],
        out_specs=pl.BlockSpec(memory_space=pl.ANY),
        scratch_shapes=[
            pltpu.VMEM((2, tm_e, dw), F32),
            pltpu.VMEM((2, tm_e, d), F32),
            pltpu.SemaphoreType.DMA((2,)),
            pltpu.SemaphoreType.DMA((2,)),
        ],
    )
    y = pl.pallas_call(
        functools.partial(_moe_kernel, tm_e=tm_e, d=d, d_exp=d_exp, n_tiles=n_tiles, t=t),
        grid_spec=grid_spec,
        out_shape=jax.ShapeDtypeStruct((t + 2 * tm_e, d), F32),
        compiler_params=_cparams(("arbitrary",)),
        name="moe_experts",
    )(te0, te1, tnv, pg3, pg3, ps3, xw, wgu, wgu, wd, wd)
    return y


def _final_kernel(x1_ref, y_ref, mod_ref, g_ref, b_ref, *o_refs, alpha, na):
    z = alpha * x1_ref[...] + mod_ref[0, 5:6, :] * y_ref[...]
    out = _ln_plain(z) * g_ref[...] + b_ref[...]
    if len(o_refs) == 1:
        o_refs[0][...] = out
    else:
        @pl.when(pl.program_id(0) < na)
        def _():
            o_refs[0][...] = out

        @pl.when(pl.program_id(0) >= na)
        def _():
            o_refs[1][...] = out


def _final_ln(x1, y, modl, cond_of_tile, ln_g, ln_b, alpha, tm, split_at=None):
    t, d = x1.shape
    row = lambda i: (i, 0)
    const = lambda i: (0, 0)
    if split_at is None:
        out_specs = pl.BlockSpec((tm, d), row)
        out_shape = jax.ShapeDtypeStruct((t, d), F32)
    else:
        out_specs = list(_split_specs(tm, d, split_at, 0))
        out_shape = [jax.ShapeDtypeStruct((split_at * tm, d), F32), jax.ShapeDtypeStruct((t - split_at * tm, d), F32)]
    return pl.pallas_call(
        functools.partial(_final_kernel, alpha=alpha, na=split_at),
        grid=(t // tm,),
        in_specs=[
            pl.BlockSpec((tm, d), row),
            pl.BlockSpec((tm, d), row),
            pl.BlockSpec((1, 6, d), lambda i: (cond_of_tile(i), 0, 0)),
            pl.BlockSpec((1, d), const),
            pl.BlockSpec((1, d), const),
        ],
        out_specs=out_specs,
        out_shape=out_shape,
        compiler_params=_cparams(("arbitrary",)),
        name="final_ln",
    )(x1, y, modl, ln_g, ln_b)


def _gate_columns():
    src_i = np.zeros((8 * N_PAIRS,), np.int32)
    src_f = np.zeros((8 * N_PAIRS,), np.int32)
    for hp in range(N_PAIRS):
        for dd in range(2):
            for hh in range(2):
                j = hp * 8 + dd * 2 + hh
                head = hp * 2 + hh
                src_i[j] = dd * M_HEADS + head
                src_f[j] = (2 + dd) * M_HEADS + head
    return src_i, src_f


def kernel(x_prompt, x_sample, cache_k, cache_v, state_C, state_n, state_m, c, c_ctx, w_ada, b_ada, w_in, b_gate,
           mh_norm_g, attn_sink, w_out, ln1_g, ln1_b, ln2_g, ln2_b, w_router, b_router, w_exp_gate, w_exp_up,
           w_exp_down):
    batch, seq, d = x_prompt.shape
    dec_batch, dec_seq, _ = x_sample.shape
    depth = w_in.shape[0]
    n_ctx = batch * seq
    n_lat = dec_batch * dec_seq
    t = n_ctx + n_lat
    alpha = (2 * depth) ** 0.25
    tm = 512
    tm_e = 256
    assert n_ctx % tm == 0 and dec_seq % tm == 0 and seq % CHUNK == 0 and dec_seq % CHUNK == 0
    assert n_ctx % dec_seq == 0 and n_ctx % WINDOW == 0 and t % tm_e == 0

    n_ctx_tiles = n_ctx // tm
    tiles_per_seq = dec_seq // tm

    def cond_of_tile(i):
        return jnp.where(i < n_ctx_tiles, 0, 1 + (i - n_ctx_tiles) // tiles_per_seq)

    n_cond = 1 + dec_batch
    cond_rows = -(-n_cond // 8) * 8
    cond = jnp.concatenate([c_ctx[None, :], c, jnp.zeros((cond_rows - n_cond, d), F32)], axis=0)
    mod = _ada_mod(cond, w_ada, b_ada).reshape(depth, cond_rows, 6, d)

    xs = (x_prompt.reshape(n_ctx, d), x_sample.reshape(n_lat, d), n_ctx_tiles, 0)

    src_i, src_f = _gate_columns()
    rope_tab = _rope_tables(dec_seq)
    wr = jnp.pad(w_router, ((0, 0), (0, LANES - N_EXPERTS))).astype(BF16)
    br = jnp.pad(b_router, (0, LANES - N_EXPERTS)).reshape(1, LANES)

    ks, vs, cs, ns, ms = [], [], [], [], []
    for l in range(depth):
        wl = w_in[l]
        kscale = M_DK ** -0.5
        w_main = jnp.concatenate(
            [wl[:, :KM_OFF], wl[:, KM_OFF:VM_OFF] * kscale, wl[:, VM_OFF:GATE_SRC_OFF],
             wl[:, GATE_SRC_OFF + 4 * M_HEADS:]], axis=1).astype(BF16)
        wgt = wl[:, GATE_SRC_OFF:GATE_SRC_OFF + 4 * M_HEADS]
        zpad = jnp.zeros((d, LANES - 8 * N_PAIRS), F32)
        w_gate = jnp.concatenate([wgt[:, src_i], zpad, wgt[:, src_f], zpad], axis=1).astype(BF16)
        bpad = jnp.zeros((LANES - 8 * N_PAIRS,), F32)
        bg = jnp.concatenate([b_gate[l][src_i], bpad, b_gate[l][src_f], bpad]).reshape(1, 2 * LANES)
        modl = mod[l]

        proj, gates, kv = _inproj(xs, t, modl, cond_of_tile, w_main, w_gate, bg, tm)
        gb, gu, gut = _gate_prep(gates, 1024)

        gain = mh_norm_g[l].reshape(1, M_HEADS * M_DV)
        sink = attn_sink[l]
        hm_c, c_new, n_new, m_new = _mlstm(proj, gb, gu, gut, gain, 0, batch, seq, None, True)
        at_c = _attn_ctx(proj, sink, batch, seq, 4)
        c0 = state_C[:, l].reshape(dec_batch, 2, N_PAIRS, PAIR_ROWS, M_DV)
        n0 = state_n[:, l].reshape(dec_batch, 2, N_PAIRS, PAIR_ROWS, 1)
        m0 = state_m[:, l].reshape(dec_batch * 2 * M_HEADS)
        (hm_l,) = _mlstm(proj, gb, gu, gut, gain, n_ctx, dec_batch, dec_seq, (c0, n0, m0), False)
        qr, kr = _rope(proj, rope_tab, n_ctx, n_lat, dec_seq, 512)
        ck = jnp.transpose(cache_k[:, l], (0, 2, 1, 3)).astype(BF16)
        cv = jnp.transpose(cache_v[:, l], (0, 2, 1, 3)).astype(BF16)
        at_l = _attn_lat(qr, kr, proj, ck, cv, sink, n_ctx, dec_batch, dec_seq)

        x1, xw, bkt = _outproj((hm_c, hm_l, at_c, at_l), w_out[l].astype(BF16), xs, t, modl, cond_of_tile,
                               ln1_g[l].reshape(1, d), ln1_b[l].reshape(1, d), wr, br, alpha, tm)
        wgu = jnp.concatenate([w_exp_gate[l], w_exp_up[l]], axis=-1).astype(BF16)
        wd = w_exp_down[l].astype(BF16)
        y = _moe(xw, bkt[0], wgu, wd, tm_e)
        last = l == depth - 1
        x_new = _final_ln(x1, y, modl, cond_of_tile, ln2_g[l].reshape(1, d), ln2_b[l].reshape(1, d), alpha, tm,
                          split_at=n_ctx_tiles if last else None)
        if last:
            y_ctx, y_lat = x_new
        else:
            xs = (x_new, x_new, n_ctx_tiles, n_ctx_tiles)

        ks.append(kv[:n_ctx, :A_KV_HEADS * A_HD].reshape(batch, seq, A_KV_HEADS, A_HD))
        vs.append(kv[:n_ctx, A_KV_HEADS * A_HD:].reshape(batch, seq, A_KV_HEADS, A_HD))
        cs.append(c_new.reshape(batch, 2, M_HEADS, M_DK, M_DV))
        ns.append(n_new.reshape(batch, 2, M_HEADS, M_DK))
        m4 = m_new[:, :, :4, 0].reshape(batch, N_PAIRS, 2, 2)
        ms.append(jnp.transpose(m4, (0, 2, 1, 3)).reshape(batch, 2, M_HEADS))

    y_prompt = y_ctx.reshape(batch, seq, d)
    y_sample = y_lat.reshape(dec_batch, dec_seq, d)
    return (y_prompt, y_sample, jnp.stack(ks, 1), jnp.stack(vs, 1), jnp.stack(cs, 1), jnp.stack(ns, 1),
            jnp.stack(ms, 1))
```

```python
import functools

import jax
import jax.numpy as jnp
import numpy as np
from jax import lax
from jax.experimental import pallas as pl
from jax.experimental.pallas import tpu as pltpu

F32 = jnp.float32
BF16 = jnp.bfloat16

M_HEADS = 8
M_DK = 64
M_DV = 128
IGATE_CAP = 15.0
MH_EPS = 1e-6
NEG_INIT = -1e30
A_HEADS = 8
A_KV_HEADS = 2
A_GROUP = A_HEADS // A_KV_HEADS
A_HD = 128
WINDOW = 128
GRID_W = 64
ROPE_BASE = 10000.0
N_EXPERTS = 16
N_GROUPS = 4
EXPERTS_PER_GROUP = N_EXPERTS // N_GROUPS
LN_EPS = 1e-5

LANES = 128
VMEM_LIMIT = 56 * 1024 * 1024

QM_OFF = 0
KM_OFF = M_HEADS * M_DK
VM_OFF = KM_OFF + M_HEADS * M_DK
OM_OFF = VM_OFF + M_HEADS * M_DV
QA_OFF = OM_OFF + M_HEADS * M_DV
KA_OFF = QA_OFF + A_HEADS * A_HD
VA_OFF = KA_OFF + A_KV_HEADS * A_HD
MAIN_COLS = VA_OFF + A_KV_HEADS * A_HD
GATE_SRC_OFF = OM_OFF + M_HEADS * M_DV
N_PAIRS = M_HEADS // 2
PAIR_ROWS = 2 * M_DK
CHUNK = 128
SUB_ROWS = 256
N_BUCKETS = N_GROUPS * 6
PAIR_LO = (0, 0, 0, 1, 1, 2)
PAIR_HI = (1, 2, 3, 2, 3, 3)


def _cparams(sem):
    return pltpu.CompilerParams(dimension_semantics=sem, vmem_limit_bytes=VMEM_LIMIT)


def _ln_plain(x):
    mu = jnp.mean(x, axis=-1, keepdims=True)
    xc = x - mu
    return xc * lax.rsqrt(jnp.mean(xc * xc, axis=-1, keepdims=True) + LN_EPS)


def _ada_kernel(c_ref, w_ref, b_ref, o_ref):
    c = c_ref[...]
    s = (c * jax.nn.sigmoid(c)).astype(BF16)
    o_ref[0] = jnp.dot(s, w_ref[0].astype(BF16), preferred_element_type=F32) + b_ref[0]


def _ada_mod(cond, w_ada, b_ada):
    depth, d, n = w_ada.shape
    rows = cond.shape[0]
    tn = 1024
    return pl.pallas_call(
        _ada_kernel,
        grid=(depth, n // tn),
        in_specs=[
            pl.BlockSpec((rows, d), lambda l, j: (0, 0)),
            pl.BlockSpec((1, d, tn), lambda l, j: (l, 0, j)),
            pl.BlockSpec((1, 1, tn), lambda l, j: (l, 0, j)),
        ],
        out_specs=pl.BlockSpec((1, rows, tn), lambda l, j: (l, 0, j)),
        out_shape=jax.ShapeDtypeStruct((depth, rows, n), F32),
        compiler_params=_cparams(("parallel", "parallel")),
        name="ada_mod",
    )(cond, w_ada, b_ada.reshape(depth, 1, n))


def _split_specs(tm, width, na, off_b):
    spec_a = pl.BlockSpec((tm, width), lambda i, *_: (jnp.minimum(i, na - 1), 0))
    spec_b = pl.BlockSpec((tm, width), lambda i, *_: (jnp.maximum(i - na, 0) + off_b, 0))
    return spec_a, spec_b


def _inproj_kernel(xa_ref, xb_ref, mod_ref, w_ref, wg_ref, bg_ref, proj_ref, gates_ref, kv_ref, h_scr, *, tn, na):
    is_a = pl.program_id(0) < na
    nj = MAIN_COLS // tn
    for s in range(xa_ref.shape[0] // SUB_ROWS):
        rows = slice(s * SUB_ROWS, (s + 1) * SUB_ROWS)
        x = jnp.where(is_a, xa_ref[rows, :], xb_ref[rows, :])
        hn = _ln_plain(x)
        h = hn * (1.0 + mod_ref[0, 1:2, :]) + mod_ref[0, 0:1, :]
        hb = h.astype(BF16)
        h_scr[rows, :] = hb
        gates_ref[rows, :] = jnp.dot(hb, wg_ref[...], preferred_element_type=F32) + bg_ref[...]
        for j in range(nj):
            cols = slice(j * tn, (j + 1) * tn)
            acc = jnp.dot(h_scr[rows, :], w_ref[:, cols], preferred_element_type=F32)
            proj_ref[rows, cols] = acc.astype(BF16)
            if j == nj - 1:
                kv_ref[rows, :] = acc


def _inproj(xs, t, modl, cond_of_tile, w_main, w_gate, b_gate, tm):
    x_a, x_b, na, off_b = xs
    d = x_a.shape[1]
    tn = MAIN_COLS - KA_OFF
    assert MAIN_COLS % tn == 0
    spec_a, spec_b = _split_specs(tm, d, na, off_b)
    const = lambda i: (0, 0)
    resident = pl.Buffered(1)
    return pl.pallas_call(
        functools.partial(_inproj_kernel, tn=tn, na=na),
        grid=(t // tm,),
        in_specs=[
            spec_a,
            spec_b,
            pl.BlockSpec((1, 6, d), lambda i: (cond_of_tile(i), 0, 0)),
            pl.BlockSpec((d, MAIN_COLS), const, pipeline_mode=resident),
            pl.BlockSpec((d, 2 * LANES), const, pipeline_mode=resident),
            pl.BlockSpec((1, 2 * LANES), const, pipeline_mode=resident),
        ],
        out_specs=[
            pl.BlockSpec((tm, MAIN_COLS), lambda i: (i, 0)),
            pl.BlockSpec((tm, 2 * LANES), lambda i: (i, 0)),
            pl.BlockSpec((tm, tn), lambda i: (i, 0)),
        ],
        out_shape=[
            jax.ShapeDtypeStruct((t, MAIN_COLS), BF16),
            jax.ShapeDtypeStruct((t, 2 * LANES), F32),
            jax.ShapeDtypeStruct((t, tn), F32),
        ],
        scratch_shapes=[pltpu.VMEM((tm, d), BF16)],
        compiler_params=_cparams(("parallel",)),
        name="inproj",
    )(x_a, x_b, modl, w_main, w_gate, b_gate)


def _gate_kernel(g_ref, gb_ref, gu_ref, gut_ref, *, tg):
    gi = g_ref[:, :LANES]
    gf = g_ref[:, LANES:]
    ig = IGATE_CAP * jnp.tanh(gi / IGATE_CAP)
    lf = jax.nn.log_sigmoid(gf)
    lane = lax.broadcasted_iota(jnp.int32, (1, LANES), 1)
    is_fwd = (lane % 8) < 2
    s_i = lax.broadcasted_iota(jnp.int32, (CHUNK, CHUNK), 0)
    r_i = lax.broadcasted_iota(jnp.int32, (CHUNK, CHUNK), 1)
    tri_lo = (r_i <= s_i).astype(F32)
    tri_hi = (r_i >= s_i).astype(F32)
    for c in range(tg // CHUNK):
        rows = slice(c * CHUNK, (c + 1) * CHUNK)
        lfc = lf[rows]
        pre = jnp.dot(tri_lo, lfc, preferred_element_type=F32, precision=lax.Precision.HIGHEST)
        suf = jnp.dot(tri_hi, lfc, preferred_element_type=F32, precision=lax.Precision.HIGHEST)
        b = jnp.where(is_fwd, pre, suf)
        u = ig[rows] - b
        gb_ref[rows, :] = b
        gu_ref[rows, :] = u
        gut_ref[:, rows] = u.T[: 8 * N_PAIRS]


def _gate_prep(gates, tg):
    t = gates.shape[0]
    return pl.pallas_call(
        functools.partial(_gate_kernel, tg=tg),
        grid=(t // tg,),
        in_specs=[pl.BlockSpec((tg, 2 * LANES), lambda i: (i, 0))],
        out_specs=[
            pl.BlockSpec((tg, LANES), lambda i: (i, 0)),
            pl.BlockSpec((tg, LANES), lambda i: (i, 0)),
            pl.BlockSpec((8 * N_PAIRS, tg), lambda i: (0, i)),
        ],
        out_shape=[
            jax.ShapeDtypeStruct((t, LANES), F32),
            jax.ShapeDtypeStruct((t, LANES), F32),
            jax.ShapeDtypeStruct((8 * N_PAIRS, t), F32),
        ],
        compiler_params=_cparams(("parallel",)),
        name="gate_prep",
    )(gates)


def _mlstm_kernel(*refs, seq, has_state, emit_state):
    it = iter(refs)
    q_ref, k_ref, v_ref, om_ref, gb_ref, gu_ref, gut_ref, gain_ref = (next(it) for _ in range(8))
    if has_state:
        c0_ref, n0_ref, m0_ref = next(it), next(it), next(it)
    out_ref = next(it)
    if emit_state:
        cout_ref, nout_ref, mout_ref = next(it), next(it), next(it)
    h_scr, cst_scr = next(it), next(it)

    b_id = pl.program_id(0)
    hp = pl.program_id(1)
    nc = seq // CHUNK
    L = CHUNK

    lane = lax.broadcasted_iota(jnp.int32, (1, LANES), 1)
    row128 = lax.broadcasted_iota(jnp.int32, (PAIR_ROWS, 1), 0)
    s_i = lax.broadcasted_iota(jnp.int32, (L, L), 0)
    r_i = lax.broadcasted_iota(jnp.int32, (L, L), 1)
    e0row = (lane == 0).astype(F32)
    e0blk = jnp.broadcast_to(e0row, (L, LANES)).astype(BF16)

    def pick_lane(x, j):
        return jnp.sum(jnp.where(lane == j, x, 0.0), axis=1, keepdims=True)

    m_init = []
    for d in range(2):
        for hh in range(2):
            if has_state:
                rowmask = (row128 // M_DK) == hh
                cst_scr[d * 2 + hh, :, :LANES] = jnp.where(rowmask, c0_ref[0, d, 0], 0.0)
                cst_scr[d * 2 + hh, :, LANES:] = jnp.where(rowmask, n0_ref[0, d, 0], 0.0) * e0row
                m0 = m0_ref[b_id * (2 * M_HEADS) + d * M_HEADS + hp * 2 + hh]
                m_init.append(jnp.full((1, 1), m0, F32))
            else:
                cst_scr[d * 2 + hh] = jnp.zeros((PAIR_ROWS, 2 * LANES), F32)
                m_init.append(jnp.full((1, 1), NEG_INIT, F32))

    def body(t, ms):
        new_ms = []
        for d in range(2):
            tri = (r_i <= s_i) if d == 0 else (r_i >= s_i)
            c = t if d == 0 else nc - 1 - t
            r0 = pl.multiple_of(c * L, L)
            q2 = q_ref[pl.ds(r0, L), :]
            k2 = k_ref[pl.ds(r0, L), :]
            gbc = gb_ref[pl.ds(r0, L), :]
            guc = gu_ref[pl.ds(r0, L), :]
            gend = gb_ref[pl.ds(r0 + (L - 1 if d == 0 else 0), 1), :]
            for hh in range(2):
                si = d * 2 + hh
                jl = hp * 8 + d * 2 + hh
                lm = (lane // M_DK) == hh
                qh = jnp.where(lm, q2, jnp.zeros_like(q2))
                kh = jnp.where(lm, k2, jnp.zeros_like(k2))
                vh = v_ref[pl.ds(r0, L), hh * LANES:(hh + 1) * LANES]
                vext = jnp.concatenate([vh, e0blk], axis=1)
                urow = gut_ref[d * 2 + hh:d * 2 + hh + 1, pl.ds(r0, L)]
                ucol = pick_lane(guc, jl)
                bcol = pick_lane(gbc, jl)
                g = pick_lane(gend, jl)
                m = ms[si]
                umat = jnp.where(tri, urow, -jnp.inf)
                cmu = jnp.max(umat, axis=1, keepdims=True)
                mm = jnp.maximum(m, cmu)
                w = jnp.exp(umat - mm)
                sqk = lax.dot_general(qh, kh, (((1,), (1,)), ((), ())), preferred_element_type=F32)
                p = (sqk * w).astype(BF16)
                intra = jnp.dot(p, vext, preferred_element_type=F32)
                cs = cst_scr[si]
                inter = jnp.dot(qh, cs.astype(BF16), preferred_element_type=F32)
                nd = intra + jnp.exp(m - mm) * inter
                num = nd[:, :LANES]
                den = nd[:, LANES:LANES + 1]
                hv = num / jnp.maximum(jnp.abs(den), jnp.exp(-bcol - mm))
                h_scr[d, pl.ds(r0, L), hh * LANES:(hh + 1) * LANES] = hv
                maxu = jnp.max(urow, axis=1, keepdims=True)
                m_new = g + jnp.maximum(m, maxu)
                wa = jnp.exp(g + ucol - m_new)
                wc = jnp.exp(g + m - m_new)
                kw = (kh.astype(F32) * wa).astype(BF16)
                upd = lax.dot_general(kw, vext, (((0,), (0,)), ((), ())), preferred_element_type=F32)
                cst_scr[si] = wc * cs + upd
                new_ms.append(m_new)
        return tuple(new_ms)

    ms_fin = lax.fori_loop(0, nc, body, tuple(m_init))

    def combine(c, carry):
        r0 = pl.multiple_of(c * L, L)
        for hh in range(2):
            cols = slice(hh * LANES, (hh + 1) * LANES)
            tot = h_scr[0, pl.ds(r0, L), cols] + h_scr[1, pl.ds(r0, L), cols]
            ms2 = jnp.mean(tot * tot, axis=1, keepdims=True)
            y = tot * lax.rsqrt(ms2 + MH_EPS) * gain_ref[:, cols]
            y = y * jax.nn.sigmoid(om_ref[pl.ds(r0, L), cols].astype(F32))
            out_ref[pl.ds(r0, L), cols] = y.astype(BF16)
        return carry

    lax.fori_loop(0, nc, combine, 0)

    if emit_state:
        for d in range(2):
            cout_ref[0, d, 0] = cst_scr[d * 2, :, :LANES] + cst_scr[d * 2 + 1, :, :LANES]
            nout_ref[0, d, 0] = cst_scr[d * 2, :, LANES:LANES + 1] + cst_scr[d * 2 + 1, :, LANES:LANES + 1]
            for hh in range(2):
                mout_ref[0, 0, d * 2 + hh:d * 2 + hh + 1, :] = jnp.broadcast_to(ms_fin[d * 2 + hh], (1, LANES))
        mout_ref[0, 0, 4:8, :] = jnp.zeros((4, LANES), F32)


def _mlstm(proj, gb, gu, gut, gain, row_off, nseq, seq, state0, emit_state):
    assert row_off % seq == 0
    rb = row_off // seq
    has_state = state0 is not None
    kernel = functools.partial(_mlstm_kernel, seq=seq, has_state=has_state, emit_state=emit_state)
    in_specs = [
        pl.BlockSpec((seq, PAIR_ROWS), lambda b, h: (rb + b, QM_OFF // PAIR_ROWS + h)),
        pl.BlockSpec((seq, PAIR_ROWS), lambda b, h: (rb + b, KM_OFF // PAIR_ROWS + h)),
        pl.BlockSpec((seq, 2 * M_DV), lambda b, h: (rb + b, VM_OFF // (2 * M_DV) + h)),
        pl.BlockSpec((seq, 2 * M_DV), lambda b, h: (rb + b, OM_OFF // (2 * M_DV) + h)),
        pl.BlockSpec((seq, LANES), lambda b, h: (rb + b, 0)),
        pl.BlockSpec((seq, LANES), lambda b, h: (rb + b, 0)),
        pl.BlockSpec((8, seq), lambda b, h: (h, rb + b)),
        pl.BlockSpec((1, 2 * M_DV), lambda b, h: (0, h)),
    ]
    args = [proj, proj, proj, proj, gb, gu, gut, gain]
    if has_state:
        c0, n0, m0 = state0
        in_specs += [
            pl.BlockSpec((1, 2, 1, PAIR_ROWS, M_DV), lambda b, h: (b, 0, h, 0, 0)),
            pl.BlockSpec((1, 2, 1, PAIR_ROWS, 1), lambda b, h: (b, 0, h, 0, 0)),
            pl.BlockSpec(memory_space=pltpu.SMEM),
        ]
        args += [c0, n0, m0]
    out_specs = [pl.BlockSpec((seq, 2 * M_DV), lambda b, h: (b, h))]
    out_shape = [jax.ShapeDtypeStruct((nseq * seq, M_HEADS * M_DV), BF16)]
    if emit_state:
        out_specs += [
            pl.BlockSpec((1, 2, 1, PAIR_ROWS, M_DV), lambda b, h: (b, 0, h, 0, 0)),
            pl.BlockSpec((1, 2, 1, PAIR_ROWS, 1), lambda b, h: (b, 0, h, 0, 0)),
            pl.BlockSpec((1, 1, 8, LANES), lambda b, h: (b, h, 0, 0)),
        ]
        out_shape += [
            jax.ShapeDtypeStruct((nseq, 2, N_PAIRS, PAIR_ROWS, M_DV), F32),
            jax.ShapeDtypeStruct((nseq, 2, N_PAIRS, PAIR_ROWS, 1), F32),
            jax.ShapeDtypeStruct((nseq, N_PAIRS, 8, LANES), F32),
        ]
    return pl.pallas_call(
        kernel,
        grid=(nseq, N_PAIRS),
        in_specs=in_specs,
        out_specs=out_specs,
        out_shape=out_shape,
        scratch_shapes=[pltpu.VMEM((2, seq, 2 * M_DV), F32), pltpu.VMEM((4, PAIR_ROWS, 2 * LANES), F32)],
        compiler_params=_cparams(("parallel", "parallel")),
        name="mlstm",
    )(*args)


def _attn_ctx_kernel(sink_ref, q_ref, k_ref, v_ref, o_ref, *, seq, nb):
    kvh = pl.program_id(1)
    scale = A_HD ** -0.5
    for s in range(nb):
        rows = slice(s * seq, (s + 1) * seq)
        k = k_ref[rows, :]
        v = v_ref[rows, :]
        for g in range(A_GROUP):
            cols = slice(g * A_HD, (g + 1) * A_HD)
            q = q_ref[rows, cols]
            sc = lax.dot_general(q, k, (((1,), (1,)), ((), ())), preferred_element_type=F32) * scale
            sk = sink_ref[kvh * A_GROUP + g]
            mx = jnp.maximum(jnp.max(sc, axis=1, keepdims=True), sk)
            p = jnp.exp(sc - mx)
            den = jnp.sum(p, axis=1, keepdims=True) + jnp.exp(sk - mx)
            o = jnp.dot(p.astype(BF16), v, preferred_element_type=F32) / den
            o_ref[rows, cols] = o.astype(BF16)


def _attn_ctx(proj, sink, nseq, seq, nb):
    gw = A_GROUP * A_HD
    return pl.pallas_call(
        functools.partial(_attn_ctx_kernel, seq=seq, nb=nb),
        grid=(nseq // nb, A_KV_HEADS),
        in_specs=[
            pl.BlockSpec(memory_space=pltpu.SMEM),
            pl.BlockSpec((nb * seq, gw), lambda b, h: (b, QA_OFF // gw + h)),
            pl.BlockSpec((nb * seq, A_HD), lambda b, h: (b, KA_OFF // A_HD + h)),
            pl.BlockSpec((nb * seq, A_HD), lambda b, h: (b, VA_OFF // A_HD + h)),
        ],
        out_specs=pl.BlockSpec((nb * seq, gw), lambda b, h: (b, h)),
        out_shape=jax.ShapeDtypeStruct((nseq * seq, A_HEADS * A_HD), BF16),
        compiler_params=_cparams(("parallel", "parallel")),
        name="attn_ctx",
    )(sink, proj, proj, proj)


def _rope_kernel(q_ref, k_ref, cos_ref, sa_ref, sb_ref, qo_ref, ko_ref):
    cos = cos_ref[...]
    sa = sa_ref[...]
    sb = sb_ref[...]

    def rot(x):
        return x * cos + pltpu.roll(x, LANES - A_HD // 4, 1) * sa + pltpu.roll(x, A_HD // 4, 1) * sb

    for h in range(A_HEADS):
        cols = slice(h * A_HD, (h + 1) * A_HD)
        qo_ref[:, cols] = rot(q_ref[:, cols].astype(F32)).astype(BF16)
    for h in range(A_KV_HEADS):
        cols = slice(h * A_HD, (h + 1) * A_HD)
        ko_ref[:, cols] = rot(k_ref[:, cols].astype(F32)).astype(BF16)


def _rope(proj, tables, row_off, nrows, seq, tr):
    cos, sa, sb = tables
    rb = row_off // tr
    nps = seq // tr
    qw = A_HEADS * A_HD
    kw = A_KV_HEADS * A_HD
    tab = pl.BlockSpec((tr, A_HD), lambda i: (i % nps, 0))
    return pl.pallas_call(
        _rope_kernel,
        grid=(nrows // tr,),
        in_specs=[
            pl.BlockSpec((tr, qw), lambda i: (rb + i, QA_OFF // qw)),
            pl.BlockSpec((tr, kw), lambda i: (rb + i, KA_OFF // kw)),
            tab, tab, tab,
        ],
        out_specs=[pl.BlockSpec((tr, qw), lambda i: (i, 0)), pl.BlockSpec((tr, kw), lambda i: (i, 0))],
        out_shape=[jax.ShapeDtypeStruct((nrows, qw), BF16), jax.ShapeDtypeStruct((nrows, kw), BF16)],
        compiler_params=_cparams(("parallel",)),
        name="rope",
    )(proj, proj, cos, sa, sb)


def _rope_tables(seq):
    half = A_HD // 2
    pos = np.arange(seq)
    row = (pos // GRID_W).astype(np.float32)
    col = (pos % GRID_W).astype(np.float32)
    inv = (ROPE_BASE ** (-np.arange(0, half, 2, dtype=np.float32) / half)).astype(np.float32)
    ang_r = row[:, None] * inv[None, :]
    ang_c = col[:, None] * inv[None, :]
    ang = np.concatenate([ang_r, ang_r, ang_c, ang_c], axis=1).astype(np.float32)
    cos = np.cos(ang).astype(np.float32)
    sin = np.sin(ang).astype(np.float32)
    first = (np.arange(A_HD) % half) < (half // 2)
    sa = np.where(first[None, :], -sin, 0.0).astype(np.float32)
    sb = np.where(first[None, :], 0.0, sin).astype(np.float32)
    return jnp.asarray(cos), jnp.asarray(sa), jnp.asarray(sb)


def _attn_lat_kernel(sink_ref, q_ref, kp_ref, kc_ref, kn_ref, vp_ref, vc_ref, vn_ref, ck_ref, cv_ref, o_ref,
                     *, nblk):
    i = pl.program_id(1)
    qb = WINDOW
    scale = A_HD ** -0.5
    r = lax.broadcasted_iota(jnp.int32, (A_GROUP * qb, 3 * qb), 0) % qb
    c = lax.broadcasted_iota(jnp.int32, (A_GROUP * qb, 3 * qb), 1)
    c_lo = jnp.where(i > 0, 0, qb)
    c_hi = jnp.where(i < nblk - 1, 3 * qb, 2 * qb)
    valid = (c >= r) & (c <= r + 2 * WINDOW) & (c >= c_lo) & (c < c_hi)
    hrow = lax.broadcasted_iota(jnp.int32, (A_GROUP * qb, 1), 0) // qb
    for kvh in range(A_KV_HEADS):
        kc = slice(kvh * A_HD, (kvh + 1) * A_HD)
        q0 = kvh * A_GROUP * A_HD
        q = jnp.concatenate([q_ref[:, q0 + g * A_HD:q0 + (g + 1) * A_HD] for g in range(A_GROUP)], axis=0)
        kwin = jnp.concatenate([kp_ref[:, kc], kc_ref[:, kc], kn_ref[:, kc]], axis=0)
        vwin = jnp.concatenate([vp_ref[:, kc], vc_ref[:, kc], vn_ref[:, kc]], axis=0)
        s_lat = lax.dot_general(q, kwin, (((1,), (1,)), ((), ())), preferred_element_type=F32) * scale
        s_ctx = lax.dot_general(q, ck_ref[0, kvh], (((1,), (1,)), ((), ())), preferred_element_type=F32) * scale
        s_lat = jnp.where(valid, s_lat, NEG_INIT)
        sk = jnp.zeros((A_GROUP * qb, 1), F32)
        for g in range(A_GROUP):
            sk = jnp.where(hrow == g, sink_ref[kvh * A_GROUP + g], sk)
        mx = jnp.maximum(jnp.maximum(jnp.max(s_lat, axis=1, keepdims=True),
                                     jnp.max(s_ctx, axis=1, keepdims=True)), sk)
        p_lat = jnp.exp(s_lat - mx)
        p_ctx = jnp.exp(s_ctx - mx)
        den = jnp.sum(p_lat, axis=1, keepdims=True) + jnp.sum(p_ctx, axis=1, keepdims=True) + jnp.exp(sk - mx)
        o = (jnp.dot(p_lat.astype(BF16), vwin, preferred_element_type=F32)
             + jnp.dot(p_ctx.astype(BF16), cv_ref[0, kvh], preferred_element_type=F32)) / den
        for g in range(A_GROUP):
            o_ref[:, q0 + g * A_HD:q0 + (g + 1) * A_HD] = o[g * qb:(g + 1) * qb].astype(BF16)


def _attn_lat(qr, kr, proj, ck, cv, sink, row_off, nseq, seq):
    qb = WINDOW
    nblk = seq // qb
    qw = A_HEADS * A_HD
    kw = A_KV_HEADS * A_HD
    rb = row_off // qb
    p_len = ck.shape[2]
    prev = lambda b, i: b * nblk + jnp.maximum(i - 1, 0)
    cur = lambda b, i: b * nblk + i
    nxt = lambda b, i: b * nblk + jnp.minimum(i + 1, nblk - 1)
    vcol = VA_OFF // kw
    return pl.pallas_call(
        functools.partial(_attn_lat_kernel, nblk=nblk),
        grid=(nseq, nblk),
        in_specs=[
            pl.BlockSpec(memory_space=pltpu.SMEM),
            pl.BlockSpec((qb, qw), lambda b, i: (cur(b, i), 0)),
            pl.BlockSpec((qb, kw), lambda b, i: (prev(b, i), 0)),
            pl.BlockSpec((qb, kw), lambda b, i: (cur(b, i), 0)),
            pl.BlockSpec((qb, kw), lambda b, i: (nxt(b, i), 0)),
            pl.BlockSpec((qb, kw), lambda b, i: (rb + prev(b, i), vcol)),
            pl.BlockSpec((qb, kw), lambda b, i: (rb + cur(b, i), vcol)),
            pl.BlockSpec((qb, kw), lambda b, i: (rb + nxt(b, i), vcol)),
            pl.BlockSpec((1, A_KV_HEADS, p_len, A_HD), lambda b, i: (b, 0, 0, 0)),
            pl.BlockSpec((1, A_KV_HEADS, p_len, A_HD), lambda b, i: (b, 0, 0, 0)),
        ],
        out_specs=pl.BlockSpec((qb, qw), lambda b, i: (cur(b, i), 0)),
        out_shape=jax.ShapeDtypeStruct((nseq * seq, qw), BF16),
        compiler_params=_cparams(("parallel", "parallel")),
        name="attn_lat",
    )(sink, qr, kr, kr, kr, proj, proj, proj, ck, cv)


def _outproj_kernel(*refs, alpha, na):
    tm = refs[0].shape[0]
    for s in range(tm // SUB_ROWS):
        _outproj_rows(slice(s * SUB_ROWS, (s + 1) * SUB_ROWS), *refs, alpha=alpha, na=na)


def _outproj_rows(rows, mmc_ref, mml_ref, mac_ref, mal_ref, w_ref, xa_ref, xb_ref, mod_ref, lng_ref, lnb_ref, wr_ref,
                  br_ref, x1_ref, xw_ref, bkt_ref, *, alpha, na):
    is_a = pl.program_id(0) < na
    d = xa_ref.shape[1]
    half = w_ref.shape[0] // 2
    mix_m = jnp.where(is_a, mmc_ref[rows, :], mml_ref[rows, :])
    mix_a = jnp.where(is_a, mac_ref[rows, :], mal_ref[rows, :])
    f = (jnp.dot(mix_m, w_ref[:half, :], preferred_element_type=F32)
         + jnp.dot(mix_a, w_ref[half:, :], preferred_element_type=F32))
    x = jnp.where(is_a, xa_ref[rows, :], xb_ref[rows, :])
    z = alpha * x + mod_ref[0, 2:3, :] * f
    x1 = _ln_plain(z) * lng_ref[...] + lnb_ref[...]
    x1_ref[rows, :] = x1
    h2 = _ln_plain(x1) * (1.0 + mod_ref[0, 4:5, :]) + mod_ref[0, 3:4, :]
    xw_ref[rows, :d] = h2
    logits = jnp.dot(h2.astype(BF16), wr_ref[...], preferred_element_type=F32)
    scores = jax.nn.sigmoid(logits)
    sel = scores + br_ref[...]
    sc_t = scores.T
    sel_t = sel.T
    sv = [sel_t[e:e + 1, :] for e in range(N_EXPERTS)]
    cv = [sc_t[e:e + 1, :] for e in range(N_EXPERTS)]
    gs = []
    for g in range(N_GROUPS):
        v = sv[4 * g:4 * g + 4]
        best = None
        for a, b in zip(PAIR_LO, PAIR_HI):
            ps = v[a] + v[b]
            best = ps if best is None else jnp.maximum(best, ps)
        gs.append(best)
    gmax = jnp.maximum(jnp.maximum(gs[0], gs[1]), jnp.maximum(gs[2], gs[3]))
    grp = jnp.full(gmax.shape, N_GROUPS - 1, jnp.int32)
    for g in range(N_GROUPS - 2, -1, -1):
        grp = jnp.where(gs[g] == gmax, g, grp)
    def pick(vals, k):
        out = vals[k]
        for g in range(1, N_GROUPS):
            out = jnp.where(grp == g, vals[4 * g + k], out)
        return out
    gv = [pick(sv, k) for k in range(EXPERTS_PER_GROUP)]
    gc = [pick(cv, k) for k in range(EXPERTS_PER_GROUP)]
    m1 = jnp.maximum(jnp.maximum(gv[0], gv[1]), jnp.maximum(gv[2], gv[3]))
    k0 = jnp.full(m1.shape, 3, jnp.int32)
    for k in range(2, -1, -1):
        k0 = jnp.where(gv[k] == m1, k, k0)
    gv2 = [jnp.where(k0 == k, -jnp.inf, gv[k]) for k in range(4)]
    m2 = jnp.maximum(jnp.maximum(gv2[0], gv2[1]), jnp.maximum(gv2[2], gv2[3]))
    k1 = jnp.full(m1.shape, 3, jnp.int32)
    for k in range(2, -1, -1):
        k1 = jnp.where(gv2[k] == m2, k, k1)
    lo = jnp.minimum(k0, k1)
    hi = jnp.maximum(k0, k1)
    pair = jnp.where(lo == 0, hi - 1, jnp.where(lo == 1, hi + 1, 5))
    bkt_ref[:, rows] = grp * 6 + pair
    s_lo = jnp.where(lo == 0, gc[0], jnp.where(lo == 1, gc[1], gc[2]))
    s_hi = jnp.where(hi == 1, gc[1], jnp.where(hi == 2, gc[2], gc[3]))
    tot = s_lo + s_hi
    wrow = lax.broadcasted_iota(jnp.int32, (LANES, tot.shape[1]), 0)
    w_t = jnp.where(wrow == 0, s_lo / tot, jnp.where(wrow == 1, s_hi / tot, 0.0))
    xw_ref[rows, d:] = w_t.T


def _outproj(mixes, w_out, xs, t, modl, cond_of_tile, ln_g, ln_b, w_router, b_router, alpha, tm):
    x_a, x_b, na, off_b = xs
    d = x_a.shape[1]
    hw = mixes[0].shape[1]
    row = lambda i: (i, 0)
    const = lambda i: (0, 0)
    mix_a, mix_b = _split_specs(tm, hw, na, 0)
    x_sa, x_sb = _split_specs(tm, d, na, off_b)
    return pl.pallas_call(
        functools.partial(_outproj_kernel, alpha=alpha, na=na),
        grid=(t // tm,),
        in_specs=[
            mix_a, mix_b, mix_a, mix_b,
            pl.BlockSpec((2 * hw, d), const, pipeline_mode=pl.Buffered(1)),
            x_sa, x_sb,
            pl.BlockSpec((1, 6, d), lambda i: (cond_of_tile(i), 0, 0)),
            pl.BlockSpec((1, d), const),
            pl.BlockSpec((1, d), const),
            pl.BlockSpec((d, LANES), const),
            pl.BlockSpec((1, LANES), const),
        ],
        out_specs=[
            pl.BlockSpec((tm, d), row),
            pl.BlockSpec((tm, d + LANES), row),
            pl.BlockSpec((1, tm), lambda i: (0, i)),
        ],
        out_shape=[
            jax.ShapeDtypeStruct((t, d), F32),
            jax.ShapeDtypeStruct((t, d + LANES), F32),
            jax.ShapeDtypeStruct((1, t), jnp.int32),
        ],
        compiler_params=_cparams(("parallel",)),
        name="outproj_route",
    )(mixes[0], mixes[1], mixes[2], mixes[3], w_out, x_a, x_b, modl, ln_g, ln_b, w_router, b_router)


def _moe_kernel(te0_ref, te1_ref, tnv_ref, pg_ref, pgn_ref, ps_ref, x_hbm, wgu0_ref, wgu1_ref, wd0_ref, wd1_ref,
                y_hbm, xbuf, ybuf, gsem, ssem, *, tm_e, d, d_exp, n_tiles, t):
    g = pl.program_id(0)
    nv = tnv_ref[g]
    slot = lax.rem(g, 2)
    oslot = 1 - slot

    def start_rows(copy_of_row, idx_ref):
        for r in range(tm_e):
            copy_of_row(r, idx_ref[0, 0, r]).start(priority=r % 2)

    def start_gather(sl, idx_ref):
        start_rows(lambda r, idx: pltpu.make_async_copy(x_hbm.at[pl.ds(idx, 1)], xbuf.at[sl, pl.ds(r, 1)],
                                                        gsem.at[sl]), idx_ref)

    def start_scatter(sl, idx_ref):
        start_rows(lambda r, idx: pltpu.make_async_copy(ybuf.at[sl, pl.ds(r, 1)], y_hbm.at[pl.ds(idx, 1)],
                                                        ssem.at[sl]), idx_ref)

    def wait_gather(sl):
        pltpu.make_async_copy(x_hbm.at[pl.ds(0, tm_e)], xbuf.at[sl], gsem.at[sl]).wait()

    def wait_scatter(sl):
        pltpu.make_async_copy(ybuf.at[sl], y_hbm.at[pl.ds(0, tm_e)], ssem.at[sl]).wait()

    @pl.when(g == 0)
    def _():
        ybuf[1] = jnp.zeros((tm_e, d), F32)
        for k in range(2):
            pltpu.make_async_copy(ybuf.at[1], y_hbm.at[pl.ds(t + k * tm_e, tm_e)], ssem.at[1]).start()
        for k in range(2):
            wait_scatter(1)
        start_gather(0, pg_ref)

    @pl.when(nv > 0)
    def _():
        g_next = jnp.minimum(g + 1, n_tiles - 1)
        has_next = jnp.logical_and(g + 1 < n_tiles, tnv_ref[g_next] > 0)

        wait_gather(slot)
        start_gather(oslot, pgn_ref)
        xb = xbuf[slot, :, :d].astype(BF16)
        y = None
        for k, (wgu_ref, wd_ref) in enumerate(((wgu0_ref, wd0_ref), (wgu1_ref, wd1_ref))):
            gu = jnp.dot(xb, wgu_ref[0], preferred_element_type=F32)
            gt = gu[:, :d_exp]
            a = (gt * jax.nn.sigmoid(gt)) * gu[:, d_exp:]
            ye = jnp.dot(a.astype(BF16), wd_ref[0], preferred_element_type=F32)
            ye = xbuf[slot, :, d + k:d + k + 1] * ye
            y = ye if y is None else y + ye
        ybuf[slot] = y
        start_scatter(slot, ps_ref)

        @pl.when(g > 0)
        def _():
            wait_scatter(oslot)

        @pl.when(jnp.logical_not(has_next))
        def _():
            wait_scatter(slot)
            wait_gather(oslot)


def _moe(xw, bkt, wgu, wd, tm_e):
    t, dw = xw.shape
    d = dw - LANES
    d_exp = wd.shape[1]
    n_tiles = t // tm_e + N_BUCKETS
    n_rows = n_tiles * tm_e
    ids = jnp.arange(N_BUCKETS, dtype=jnp.int32)
    onehot = (bkt[:, None] == ids[None, :]).astype(jnp.int32)
    counts = jnp.sum(onehot, axis=0)
    tiles_b = (counts + tm_e - 1) // tm_e
    tile_end = jnp.cumsum(tiles_b)
    row_start = (tile_end - tiles_b) * tm_e
    blk = tm_e
    oh3 = onehot.astype(F32).reshape(t // blk, blk, N_BUCKETS)
    tri = (jnp.arange(blk)[:, None] >= jnp.arange(blk)[None, :]).astype(F32)
    local = jnp.einsum("ij,bjk->bik", tri, oh3)
    before = jnp.cumsum(local[:, -1, :], axis=0) - local[:, -1, :]
    running = (local + before[:, None, :]).astype(jnp.int32).reshape(t, N_BUCKETS)
    rank = jnp.take_along_axis(running, bkt[:, None], axis=1)[:, 0] - 1
    pos = row_start[bkt] + rank
    tile_ids = jnp.arange(n_tiles, dtype=jnp.int32)
    used = tile_end[-1]
    tb = jnp.sum((tile_end[None, :] <= jnp.minimum(tile_ids, used - 1)[:, None]).astype(jnp.int32), axis=1)
    tile_in_b = tile_ids - (tile_end - tiles_b)[tb]
    tnv = jnp.where(tile_ids < used, jnp.clip(counts[tb] - tile_in_b * tm_e, 0, tm_e), 0).astype(jnp.int32)
    lo = jnp.asarray(PAIR_LO, jnp.int32)
    hi = jnp.asarray(PAIR_HI, jnp.int32)
    te0 = (tb // 6) * EXPERTS_PER_GROUP + lo[tb % 6]
    te1 = (tb // 6) * EXPERTS_PER_GROUP + hi[tb % 6]
    tok = jnp.arange(t, dtype=jnp.int32)
    slot_row = jnp.arange(n_rows, dtype=jnp.int32)
    dump = t + ((slot_row // tm_e) % 2) * tm_e + slot_row % tm_e
    perm_s = dump.at[pos].set(tok)
    perm_g = jnp.where(perm_s < t, perm_s, 0)
    pg3 = perm_g.reshape(n_tiles, 1, tm_e)
    ps3 = perm_s.reshape(n_tiles, 1, tm_e)

    grid_spec = pltpu.PrefetchScalarGridSpec(
        num_scalar_prefetch=3,
        grid=(n_tiles,),
        in_specs=[
            pl.BlockSpec((1, 1, tm_e), lambda g, *_: (g, 0, 0), memory_space=pltpu.SMEM),
            pl.BlockSpec((1, 1, tm_e), lambda g, *_: (jnp.minimum(g + 1, n_tiles - 1), 0, 0),
                         memory_space=pltpu.SMEM),
            pl.BlockSpec((1, 1, tm_e), lambda g, *_: (g, 0, 0), memory_space=pltpu.SMEM),
            pl.BlockSpec(memory_space=pl.ANY),
            pl.BlockSpec((1, d, 2 * d_exp), lambda g, e0, e1, tv: (e0[g], 0, 0)),
            pl.BlockSpec((1, d, 2 * d_exp), lambda g, e0, e1, tv: (e1[g], 0, 0)),
            pl.BlockSpec((1, d_exp, d), lambda g, e0, e1, tv: (e0[g], 0, 0)),
            pl.BlockSpec((1, d_exp, d), lambda g, e0, e1, tv: (e1[g], 0, 0)),
        ],
        out_specs=pl.BlockSpec(memory_space=pl.ANY),
        scratch_shapes=[
            pltpu.VMEM((2, tm_e, dw), F32),
            pltpu.VMEM((2, tm_e, d), F32),
            pltpu.SemaphoreType.DMA((2,)),
            pltpu.SemaphoreType.DMA((2,)),
        ],
    )
    y = pl.pallas_call(
        functools.partial(_moe_kernel, tm_e=tm_e, d=d, d_exp=d_exp, n_tiles=n_tiles, t=t),
        grid_spec=grid_spec,
        out_shape=jax.ShapeDtypeStruct((t + 2 * tm_e, d), F32),
        compiler_params=_cparams(("arbitrary",)),
        name="moe_experts",
    )(te0, te1, tnv, pg3, pg3, ps3, xw, wgu, wgu, wd, wd)
    return y


def _final_kernel(x1_ref, y_ref, mod_ref, g_ref, b_ref, *o_refs, alpha, na):
    z = alpha * x1_ref[...] + mod_ref[0, 5:6, :] * y_ref[...]
    out = _ln_plain(z) * g_ref[...] + b_ref[...]
    if len(o_refs) == 1:
        o_refs[0][...] = out
    else:
        @pl.when(pl.program_id(0) < na)
        def _():
            o_refs[0][...] = out

        @pl.when(pl.program_id(0) >= na)
        def _():
            o_refs[1][...] = out


def _final_ln(x1, y, modl, cond_of_tile, ln_g, ln_b, alpha, tm, split_at=None):
    t, d = x1.shape
    row = lambda i: (i, 0)
    const = lambda i: (0, 0)
    if split_at is None:
        out_specs = pl.BlockSpec((tm, d), row)
        out_shape = jax.ShapeDtypeStruct((t, d), F32)
    else:
        out_specs = list(_split_specs(tm, d, split_at, 0))
        out_shape = [jax.ShapeDtypeStruct((split_at * tm, d), F32), jax.ShapeDtypeStruct((t - split_at * tm, d), F32)]
    return pl.pallas_call(
        functools.partial(_final_kernel, alpha=alpha, na=split_at),
        grid=(t // tm,),
        in_specs=[
            pl.BlockSpec((tm, d), row),
            pl.BlockSpec((tm, d), row),
            pl.BlockSpec((1, 6, d), lambda i: (cond_of_tile(i), 0, 0)),
            pl.BlockSpec((1, d), const),
            pl.BlockSpec((1, d), const),
        ],
        out_specs=out_specs,
        out_shape=out_shape,
        compiler_params=_cparams(("arbitrary",)),
        name="final_ln",
    )(x1, y, modl, ln_g, ln_b)


def _gate_columns():
    src_i = np.zeros((8 * N_PAIRS,), np.int32)
    src_f = np.zeros((8 * N_PAIRS,), np.int32)
    for hp in range(N_PAIRS):
        for dd in range(2):
            for hh in range(2):
                j = hp * 8 + dd * 2 + hh
                head = hp * 2 + hh
                src_i[j] = dd * M_HEADS + head
                src_f[j] = (2 + dd) * M_HEADS + head
    return src_i, src_f


def kernel(x_prompt, x_sample, cache_k, cache_v, state_C, state_n, state_m, c, c_ctx, w_ada, b_ada, w_in, b_gate,
           mh_norm_g, attn_sink, w_out, ln1_g, ln1_b, ln2_g, ln2_b, w_router, b_router, w_exp_gate, w_exp_up,
           w_exp_down):
    batch, seq, d = x_prompt.shape
    dec_batch, dec_seq, _ = x_sample.shape
    depth = w_in.shape[0]
    n_ctx = batch * seq
    n_lat = dec_batch * dec_seq
    t = n_ctx + n_lat
    alpha = (2 * depth) ** 0.25
    tm = 512
    tm_e = 256
    assert n_ctx % tm == 0 and dec_seq % tm == 0 and seq % CHUNK == 0 and dec_seq % CHUNK == 0
    assert n_ctx % dec_seq == 0 and n_ctx % WINDOW == 0 and t % tm_e == 0

    n_ctx_tiles = n_ctx // tm
    tiles_per_seq = dec_seq // tm

    def cond_of_tile(i):
        return jnp.where(i < n_ctx_tiles, 0, 1 + (i - n_ctx_tiles) // tiles_per_seq)

    n_cond = 1 + dec_batch
    cond_rows = -(-n_cond // 8) * 8
    cond = jnp.concatenate([c_ctx[None, :], c, jnp.zeros((cond_rows - n_cond, d), F32)], axis=0)
    mod = _ada_mod(cond, w_ada, b_ada).reshape(depth, cond_rows, 6, d)

    xs = (x_prompt.reshape(n_ctx, d), x_sample.reshape(n_lat, d), n_ctx_tiles, 0)

    src_i, src_f = _gate_columns()
    rope_tab = _rope_tables(dec_seq)
    wr = jnp.pad(w_router, ((0, 0), (0, LANES - N_EXPERTS))).astype(BF16)
    br = jnp.pad(b_router, (0, LANES - N_EXPERTS)).reshape(1, LANES)

    ks, vs, cs, ns, ms = [], [], [], [], []
    for l in range(depth):
        wl = w_in[l]
        kscale = M_DK ** -0.5
        w_main = jnp.concatenate(
            [wl[:, :KM_OFF], wl[:, KM_OFF:VM_OFF] * kscale, wl[:, VM_OFF:GATE_SRC_OFF],
             wl[:, GATE_SRC_OFF + 4 * M_HEADS:]], axis=1).astype(BF16)
        wgt = wl[:, GATE_SRC_OFF:GATE_SRC_OFF + 4 * M_HEADS]
        zpad = jnp.zeros((d, LANES - 8 * N_PAIRS), F32)
        w_gate = jnp.concatenate([wgt[:, src_i], zpad, wgt[:, src_f], zpad], axis=1).astype(BF16)
        bpad = jnp.zeros((LANES - 8 * N_PAIRS,), F32)
        bg = jnp.concatenate([b_gate[l][src_i], bpad, b_gate[l][src_f], bpad]).reshape(1, 2 * LANES)
        modl = mod[l]

        proj, gates, kv = _inproj(xs, t, modl, cond_of_tile, w_main, w_gate, bg, tm)
        gb, gu, gut = _gate_prep(gates, 1024)

        gain = mh_norm_g[l].reshape(1, M_HEADS * M_DV)
        sink = attn_sink[l]
        hm_c, c_new, n_new, m_new = _mlstm(proj, gb, gu, gut, gain, 0, batch, seq, None, True)
        at_c = _attn_ctx(proj, sink, batch, seq, 4)
        c0 = state_C[:, l].reshape(dec_batch, 2, N_PAIRS, PAIR_ROWS, M_DV)
        n0 = state_n[:, l].reshape(dec_batch, 2, N_PAIRS, PAIR_ROWS, 1)
        m0 = state_m[:, l].reshape(dec_batch * 2 * M_HEADS)
        (hm_l,) = _mlstm(proj, gb, gu, gut, gain, n_ctx, dec_batch, dec_seq, (c0, n0, m0), False)
        qr, kr = _rope(proj, rope_tab, n_ctx, n_lat, dec_seq, 512)
        ck = jnp.transpose(cache_k[:, l], (0, 2, 1, 3)).astype(BF16)
        cv = jnp.transpose(cache_v[:, l], (0, 2, 1, 3)).astype(BF16)
        at_l = _attn_lat(qr, kr, proj, ck, cv, sink, n_ctx, dec_batch, dec_seq)

        x1, xw, bkt = _outproj((hm_c, hm_l, at_c, at_l), w_out[l].astype(BF16), xs, t, modl, cond_of_tile,
                               ln1_g[l].reshape(1, d), ln1_b[l].reshape(1, d), wr, br, alpha, tm)
        wgu = jnp.concatenate([w_exp_gate[l], w_exp_up[l]], axis=-1).astype(BF16)
        wd = w_exp_down[l].astype(BF16)
        y = _moe(xw, bkt[0], wgu, wd, tm_e)
        last = l == depth - 1
        x_new = _final_ln(x1, y, modl, cond_of_tile, ln2_g[l].reshape(1, d), ln2_b[l].reshape(1, d), alpha, tm,
                          split_at=n_ctx_tiles if last else None)
        if last:
            y_ctx, y_lat = x_new
        else:
            xs = (x_new, x_new, n_ctx_tiles, n_ctx_tiles)

        ks.append(kv[:n_ctx, :A_KV_HEADS * A_HD].reshape(batch, seq, A_KV_HEADS, A_HD))
        vs.append(kv[:n_ctx, A_KV_HEADS * A_HD:].reshape(batch, seq, A_KV_HEADS, A_HD))
        cs.append(c_new.reshape(batch, 2, M_HEADS, M_DK, M_DV))
        ns.append(n_new.reshape(batch, 2, M_HEADS, M_DK))
        m4 = m_new[:, :, :4, 0].reshape(batch, N_PAIRS, 2, 2)
        ms.append(jnp.transpose(m4, (0, 2, 1, 3)).reshape(batch, 2, M_HEADS))

    y_prompt = y_ctx.reshape(batch, seq, d)
    y_sample = y_lat.reshape(dec_batch, dec_seq, d)
    return (y_prompt, y_sample, jnp.stack(ks, 1), jnp.stack(vs, 1), jnp.stack(cs, 1), jnp.stack(ns, 1),
            jnp.stack(ms, 1))
```

```python
import functools

import jax
import jax.numpy as jnp
import numpy as np
from jax import lax
from jax.experimental import pallas as pl
from jax.experimental.pallas import tpu as pltpu

F32 = jnp.float32
BF16 = jnp.bfloat16

M_HEADS = 8
M_DK = 64
M_DV = 128
IGATE_CAP = 15.0
MH_EPS = 1e-6
NEG_INIT = -1e30
A_HEADS = 8
A_KV_HEADS = 2
A_GROUP = A_HEADS // A_KV_HEADS
A_HD = 128
WINDOW = 128
GRID_W = 64
ROPE_BASE = 10000.0
N_EXPERTS = 16
N_GROUPS = 4
EXPERTS_PER_GROUP = N_EXPERTS // N_GROUPS
LN_EPS = 1e-5

LANES = 128
VMEM_LIMIT = 56 * 1024 * 1024

QM_OFF = 0
KM_OFF = M_HEADS * M_DK
VM_OFF = KM_OFF + M_HEADS * M_DK
OM_OFF = VM_OFF + M_HEADS * M_DV
QA_OFF = OM_OFF + M_HEADS * M_DV
KA_OFF = QA_OFF + A_HEADS * A_HD
VA_OFF = KA_OFF + A_KV_HEADS * A_HD
MAIN_COLS = VA_OFF + A_KV_HEADS * A_HD
GATE_SRC_OFF = OM_OFF + M_HEADS * M_DV
N_PAIRS = M_HEADS // 2
PAIR_ROWS = 2 * M_DK
CHUNK = 128
SUB_ROWS = 256
N_BUCKETS = N_GROUPS * 6
PAIR_LO = (0, 0, 0, 1, 1, 2)
PAIR_HI = (1, 2, 3, 2, 3, 3)


def _cparams(sem):
    return pltpu.CompilerParams(dimension_semantics=sem, vmem_limit_bytes=VMEM_LIMIT)


def _ln_plain(x):
    mu = jnp.mean(x, axis=-1, keepdims=True)
    xc = x - mu
    return xc * lax.rsqrt(jnp.mean(xc * xc, axis=-1, keepdims=True) + LN_EPS)


def _ada_kernel(c_ref, w_ref, b_ref, o_ref):
    c = c_ref[...]
    s = (c * jax.nn.sigmoid(c)).astype(BF16)
    o_ref[0] = jnp.dot(s, w_ref[0].astype(BF16), preferred_element_type=F32) + b_ref[0]


def _ada_mod(cond, w_ada, b_ada):
    depth, d, n = w_ada.shape
    rows = cond.shape[0]
    tn = 2048
    return pl.pallas_call(
        _ada_kernel,
        grid=(depth, n // tn),
        in_specs=[
            pl.BlockSpec((rows, d), lambda l, j: (0, 0)),
            pl.BlockSpec((1, d, tn), lambda l, j: (l, 0, j)),
            pl.BlockSpec((1, 1, tn), lambda l, j: (l, 0, j)),
        ],
        out_specs=pl.BlockSpec((1, rows, tn), lambda l, j: (l, 0, j)),
        out_shape=jax.ShapeDtypeStruct((depth, rows, n), F32),
        compiler_params=_cparams(("parallel", "parallel")),
        name="ada_mod",
    )(cond, w_ada, b_ada.reshape(depth, 1, n))


def _split_specs(tm, width, na, off_b):
    spec_a = pl.BlockSpec((tm, width), lambda i, *_: (jnp.minimum(i, na - 1), 0))
    spec_b = pl.BlockSpec((tm, width), lambda i, *_: (jnp.maximum(i - na, 0) + off_b, 0))
    return spec_a, spec_b


def _inproj_kernel(xa_ref, xb_ref, mod_ref, w_ref, wg_ref, bg_ref, proj_ref, gates_ref, kv_ref, h_scr, *, tn, na):
    is_a = pl.program_id(0) < na
    nj = MAIN_COLS // tn
    for s in range(xa_ref.shape[0] // SUB_ROWS):
        rows = slice(s * SUB_ROWS, (s + 1) * SUB_ROWS)
        x = jnp.where(is_a, xa_ref[rows, :], xb_ref[rows, :])
        hn = _ln_plain(x)
        h = hn * (1.0 + mod_ref[0, 1:2, :]) + mod_ref[0, 0:1, :]
        hb = h.astype(BF16)
        h_scr[rows, :] = hb
        gates_ref[rows, :] = jnp.dot(hb, wg_ref[...], preferred_element_type=F32) + bg_ref[...]
        for j in range(nj):
            cols = slice(j * tn, (j + 1) * tn)
            acc = jnp.dot(h_scr[rows, :], w_ref[0, :, cols], preferred_element_type=F32)
            proj_ref[rows, cols] = acc.astype(BF16)
            if j == nj - 1:
                kv_ref[rows, :] = acc


def _inproj(xs, t, modl, cond_of_tile, w_main, layer, w_gate, b_gate, tm):
    x_a, x_b, na, off_b = xs
    d = x_a.shape[1]
    tn = MAIN_COLS - KA_OFF
    assert MAIN_COLS % tn == 0
    spec_a, spec_b = _split_specs(tm, d, na, off_b)
    const = lambda i: (0, 0)
    resident = pl.Buffered(1)
    return pl.pallas_call(
        functools.partial(_inproj_kernel, tn=tn, na=na),
        grid=(t // tm,),
        in_specs=[
            spec_a,
            spec_b,
            pl.BlockSpec((1, 6, d), lambda i: (cond_of_tile(i), 0, 0)),
            pl.BlockSpec((1, d, MAIN_COLS), lambda i: (layer, 0, 0), pipeline_mode=resident),
            pl.BlockSpec((d, 2 * LANES), const, pipeline_mode=resident),
            pl.BlockSpec((1, 2 * LANES), const, pipeline_mode=resident),
        ],
        out_specs=[
            pl.BlockSpec((tm, MAIN_COLS), lambda i: (i, 0)),
            pl.BlockSpec((tm, 2 * LANES), lambda i: (i, 0)),
            pl.BlockSpec((tm, tn), lambda i: (i, 0)),
        ],
        out_shape=[
            jax.ShapeDtypeStruct((t, MAIN_COLS), BF16),
            jax.ShapeDtypeStruct((t, 2 * LANES), F32),
            jax.ShapeDtypeStruct((t, tn), F32),
        ],
        scratch_shapes=[pltpu.VMEM((tm, d), BF16)],
        compiler_params=_cparams(("parallel",)),
        name="inproj",
    )(x_a, x_b, modl, w_main, w_gate, b_gate)


def _gate_kernel(g_ref, gb_ref, gu_ref, gut_ref, *, tg):
    gi = g_ref[:, :LANES]
    gf = g_ref[:, LANES:]
    ig = IGATE_CAP * jnp.tanh(gi / IGATE_CAP)
    lf = jax.nn.log_sigmoid(gf)
    lane = lax.broadcasted_iota(jnp.int32, (1, LANES), 1)
    is_fwd = (lane % 8) < 2
    s_i = lax.broadcasted_iota(jnp.int32, (CHUNK, CHUNK), 0)
    r_i = lax.broadcasted_iota(jnp.int32, (CHUNK, CHUNK), 1)
    tri_lo = (r_i <= s_i).astype(F32)
    tri_hi = (r_i >= s_i).astype(F32)
    for c in range(tg // CHUNK):
        rows = slice(c * CHUNK, (c + 1) * CHUNK)
        lfc = lf[rows]
        pre = jnp.dot(tri_lo, lfc, preferred_element_type=F32, precision=lax.Precision.HIGHEST)
        suf = jnp.dot(tri_hi, lfc, preferred_element_type=F32, precision=lax.Precision.HIGHEST)
        b = jnp.where(is_fwd, pre, suf)
        u = ig[rows] - b
        gb_ref[rows, :] = b
        gu_ref[rows, :] = u
        gut_ref[:, rows] = u.T[: 8 * N_PAIRS]


def _gate_prep(gates, tg):
    t = gates.shape[0]
    return pl.pallas_call(
        functools.partial(_gate_kernel, tg=tg),
        grid=(t // tg,),
        in_specs=[pl.BlockSpec((tg, 2 * LANES), lambda i: (i, 0))],
        out_specs=[
            pl.BlockSpec((tg, LANES), lambda i: (i, 0)),
            pl.BlockSpec((tg, LANES), lambda i: (i, 0)),
            pl.BlockSpec((8 * N_PAIRS, tg), lambda i: (0, i)),
        ],
        out_shape=[
            jax.ShapeDtypeStruct((t, LANES), F32),
            jax.ShapeDtypeStruct((t, LANES), F32),
            jax.ShapeDtypeStruct((8 * N_PAIRS, t), F32),
        ],
        compiler_params=_cparams(("parallel",)),
        name="gate_prep",
    )(gates)


def _mlstm_kernel(*refs, seq, has_state, emit_state):
    it = iter(refs)
    q_ref, k_ref, v_ref, om_ref, gb_ref, gu_ref, gut_ref, gain_ref = (next(it) for _ in range(8))
    if has_state:
        c0_ref, n0_ref, m0_ref = next(it), next(it), next(it)
    out_ref = next(it)
    if emit_state:
        cout_ref, nout_ref, mout_ref = next(it), next(it), next(it)
    h_scr, cst_scr = next(it), next(it)

    b_id = pl.program_id(0)
    hp = pl.program_id(1)
    nc = seq // CHUNK
    L = CHUNK

    lane = lax.broadcasted_iota(jnp.int32, (1, LANES), 1)
    row128 = lax.broadcasted_iota(jnp.int32, (PAIR_ROWS, 1), 0)
    s_i = lax.broadcasted_iota(jnp.int32, (L, L), 0)
    r_i = lax.broadcasted_iota(jnp.int32, (L, L), 1)
    e0row = (lane == 0).astype(F32)
    e0blk = jnp.broadcast_to(e0row, (L, LANES)).astype(BF16)

    def pick_lane(x, j):
        return jnp.sum(jnp.where(lane == j, x, 0.0), axis=1, keepdims=True)

    m_init = []
    for d in range(2):
        for hh in range(2):
            if has_state:
                rowmask = (row128 // M_DK) == hh
                cst_scr[d * 2 + hh, :, :LANES] = jnp.where(rowmask, c0_ref[0, d, 0], 0.0)
                cst_scr[d * 2 + hh, :, LANES:] = jnp.where(rowmask, n0_ref[0, d, 0], 0.0) * e0row
                m0 = m0_ref[b_id * (2 * M_HEADS) + d * M_HEADS + hp * 2 + hh]
                m_init.append(jnp.full((1, 1), m0, F32))
            else:
                cst_scr[d * 2 + hh] = jnp.zeros((PAIR_ROWS, 2 * LANES), F32)
                m_init.append(jnp.full((1, 1), NEG_INIT, F32))

    def body(t, ms):
        new_ms = []
        for d in range(2):
            tri = (r_i <= s_i) if d == 0 else (r_i >= s_i)
            c = t if d == 0 else nc - 1 - t
            r0 = pl.multiple_of(c * L, L)
            q2 = q_ref[pl.ds(r0, L), :]
            k2 = k_ref[pl.ds(r0, L), :]
            gbc = gb_ref[pl.ds(r0, L), :]
            guc = gu_ref[pl.ds(r0, L), :]
            gend = gb_ref[pl.ds(r0 + (L - 1 if d == 0 else 0), 1), :]
            for hh in range(2):
                si = d * 2 + hh
                jl = hp * 8 + d * 2 + hh
                lm = (lane // M_DK) == hh
                qh = jnp.where(lm, q2, jnp.zeros_like(q2))
                kh = jnp.where(lm, k2, jnp.zeros_like(k2))
                vh = v_ref[pl.ds(r0, L), hh * LANES:(hh + 1) * LANES]
                vext = jnp.concatenate([vh, e0blk], axis=1)
                urow = gut_ref[d * 2 + hh:d * 2 + hh + 1, pl.ds(r0, L)]
                ucol = pick_lane(guc, jl)
                bcol = pick_lane(gbc, jl)
                g = pick_lane(gend, jl)
                m = ms[si]
                umat = jnp.where(tri, urow, -jnp.inf)
                cmu = jnp.max(umat, axis=1, keepdims=True)
                mm = jnp.maximum(m, cmu)
                w = jnp.exp(umat - mm)
                sqk = lax.dot_general(qh, kh, (((1,), (1,)), ((), ())), preferred_element_type=F32)
                p = (sqk * w).astype(BF16)
                intra = jnp.dot(p, vext, preferred_element_type=F32)
                cs = cst_scr[si]
                inter = jnp.dot(qh, cs.astype(BF16), preferred_element_type=F32)
                nd = intra + jnp.exp(m - mm) * inter
                num = nd[:, :LANES]
                den = nd[:, LANES:LANES + 1]
                hv = num / jnp.maximum(jnp.abs(den), jnp.exp(-bcol - mm))
                h_scr[d, pl.ds(r0, L), hh * LANES:(hh + 1) * LANES] = hv
                maxu = jnp.max(urow, axis=1, keepdims=True)
                m_new = g + jnp.maximum(m, maxu)
                wa = jnp.exp(g + ucol - m_new)
                wc = jnp.exp(g + m - m_new)
                kw = (kh.astype(F32) * wa).astype(BF16)
                upd = lax.dot_general(kw, vext, (((0,), (0,)), ((), ())), preferred_element_type=F32)
                cst_scr[si] = wc * cs + upd
                new_ms.append(m_new)
        return tuple(new_ms)

    ms_fin = lax.fori_loop(0, nc, body, tuple(m_init))

    def combine(c, carry):
        r0 = pl.multiple_of(c * L, L)
        for hh in range(2):
            cols = slice(hh * LANES, (hh + 1) * LANES)
            tot = h_scr[0, pl.ds(r0, L), cols] + h_scr[1, pl.ds(r0, L), cols]
            ms2 = jnp.mean(tot * tot, axis=1, keepdims=True)
            y = tot * lax.rsqrt(ms2 + MH_EPS) * gain_ref[:, cols]
            y = y * jax.nn.sigmoid(om_ref[pl.ds(r0, L), cols].astype(F32))
            out_ref[pl.ds(r0, L), cols] = y.astype(BF16)
        return carry

    lax.fori_loop(0, nc, combine, 0)

    if emit_state:
        for d in range(2):
            cout_ref[0, d, 0] = cst_scr[d * 2, :, :LANES] + cst_scr[d * 2 + 1, :, :LANES]
            nout_ref[0, d, 0] = cst_scr[d * 2, :, LANES:LANES + 1] + cst_scr[d * 2 + 1, :, LANES:LANES + 1]
            for hh in range(2):
                mout_ref[0, 0, d * 2 + hh:d * 2 + hh + 1, :] = jnp.broadcast_to(ms_fin[d * 2 + hh], (1, LANES))
        mout_ref[0, 0, 4:8, :] = jnp.zeros((4, LANES), F32)


def _mlstm(proj, gb, gu, gut, gain, row_off, nseq, seq, state0, emit_state):
    assert row_off % seq == 0
    rb = row_off // seq
    has_state = state0 is not None
    kernel = functools.partial(_mlstm_kernel, seq=seq, has_state=has_state, emit_state=emit_state)
    in_specs = [
        pl.BlockSpec((seq, PAIR_ROWS), lambda b, h: (rb + b, QM_OFF // PAIR_ROWS + h)),
        pl.BlockSpec((seq, PAIR_ROWS), lambda b, h: (rb + b, KM_OFF // PAIR_ROWS + h)),
        pl.BlockSpec((seq, 2 * M_DV), lambda b, h: (rb + b, VM_OFF // (2 * M_DV) + h)),
        pl.BlockSpec((seq, 2 * M_DV), lambda b, h: (rb + b, OM_OFF // (2 * M_DV) + h)),
        pl.BlockSpec((seq, LANES), lambda b, h: (rb + b, 0)),
        pl.BlockSpec((seq, LANES), lambda b, h: (rb + b, 0)),
        pl.BlockSpec((8, seq), lambda b, h: (h, rb + b)),
        pl.BlockSpec((1, 2 * M_DV), lambda b, h: (0, h)),
    ]
    args = [proj, proj, proj, proj, gb, gu, gut, gain]
    if has_state:
        c0, n0, m0 = state0
        in_specs += [
            pl.BlockSpec((1, 2, 1, PAIR_ROWS, M_DV), lambda b, h: (b, 0, h, 0, 0)),
            pl.BlockSpec((1, 2, 1, PAIR_ROWS, 1), lambda b, h: (b, 0, h, 0, 0)),
            pl.BlockSpec(memory_space=pltpu.SMEM),
        ]
        args += [c0, n0, m0]
    out_specs = [pl.BlockSpec((seq, 2 * M_DV), lambda b, h: (b, h))]
    out_shape = [jax.ShapeDtypeStruct((nseq * seq, M_HEADS * M_DV), BF16)]
    if emit_state:
        out_specs += [
            pl.BlockSpec((1, 2, 1, PAIR_ROWS, M_DV), lambda b, h: (b, 0, h, 0, 0)),
            pl.BlockSpec((1, 2, 1, PAIR_ROWS, 1), lambda b, h: (b, 0, h, 0, 0)),
            pl.BlockSpec((1, 1, 8, LANES), lambda b, h: (b, h, 0, 0)),
        ]
        out_shape += [
            jax.ShapeDtypeStruct((nseq, 2, N_PAIRS, PAIR_ROWS, M_DV), F32),
            jax.ShapeDtypeStruct((nseq, 2, N_PAIRS, PAIR_ROWS, 1), F32),
            jax.ShapeDtypeStruct((nseq, N_PAIRS, 8, LANES), F32),
        ]
    return pl.pallas_call(
        kernel,
        grid=(nseq, N_PAIRS),
        in_specs=in_specs,
        out_specs=out_specs,
        out_shape=out_shape,
        scratch_shapes=[pltpu.VMEM((2, seq, 2 * M_DV), F32), pltpu.VMEM((4, PAIR_ROWS, 2 * LANES), F32)],
        compiler_params=_cparams(("parallel", "parallel")),
        name="mlstm",
    )(*args)


def _attn_ctx_kernel(sink_ref, q_ref, k_ref, v_ref, o_ref, *, seq, nb):
    kvh = pl.program_id(1)
    scale = A_HD ** -0.5
    for s in range(nb):
        rows = slice(s * seq, (s + 1) * seq)
        k = k_ref[rows, :]
        v = v_ref[rows, :]
        for g in range(A_GROUP):
            cols = slice(g * A_HD, (g + 1) * A_HD)
            q = q_ref[rows, cols]
            sc = lax.dot_general(q, k, (((1,), (1,)), ((), ())), preferred_element_type=F32) * scale
            sk = sink_ref[kvh * A_GROUP + g]
            mx = jnp.maximum(jnp.max(sc, axis=1, keepdims=True), sk)
            p = jnp.exp(sc - mx)
            den = jnp.sum(p, axis=1, keepdims=True) + jnp.exp(sk - mx)
            o = jnp.dot(p.astype(BF16), v, preferred_element_type=F32) / den
            o_ref[rows, cols] = o.astype(BF16)


def _attn_ctx(proj, sink, nseq, seq, nb):
    gw = A_GROUP * A_HD
    return pl.pallas_call(
        functools.partial(_attn_ctx_kernel, seq=seq, nb=nb),
        grid=(nseq // nb, A_KV_HEADS),
        in_specs=[
            pl.BlockSpec(memory_space=pltpu.SMEM),
            pl.BlockSpec((nb * seq, gw), lambda b, h: (b, QA_OFF // gw + h)),
            pl.BlockSpec((nb * seq, A_HD), lambda b, h: (b, KA_OFF // A_HD + h)),
            pl.BlockSpec((nb * seq, A_HD), lambda b, h: (b, VA_OFF // A_HD + h)),
        ],
        out_specs=pl.BlockSpec((nb * seq, gw), lambda b, h: (b, h)),
        out_shape=jax.ShapeDtypeStruct((nseq * seq, A_HEADS * A_HD), BF16),
        compiler_params=_cparams(("parallel", "parallel")),
        name="attn_ctx",
    )(sink, proj, proj, proj)


def _rope_kernel(q_ref, k_ref, cos_ref, sa_ref, sb_ref, qo_ref, ko_ref):
    cos = cos_ref[...]
    sa = sa_ref[...]
    sb = sb_ref[...]

    def rot(x):
        return x * cos + pltpu.roll(x, LANES - A_HD // 4, 1) * sa + pltpu.roll(x, A_HD // 4, 1) * sb

    for h in range(A_HEADS):
        cols = slice(h * A_HD, (h + 1) * A_HD)
        qo_ref[:, cols] = rot(q_ref[:, cols].astype(F32)).astype(BF16)
    for h in range(A_KV_HEADS):
        cols = slice(h * A_HD, (h + 1) * A_HD)
        ko_ref[:, cols] = rot(k_ref[:, cols].astype(F32)).astype(BF16)


def _rope(proj, tables, row_off, nrows, seq, tr):
    cos, sa, sb = tables
    rb = row_off // tr
    nps = seq // tr
    qw = A_HEADS * A_HD
    kw = A_KV_HEADS * A_HD
    tab = pl.BlockSpec((tr, A_HD), lambda i: (i % nps, 0))
    return pl.pallas_call(
        _rope_kernel,
        grid=(nrows // tr,),
        in_specs=[
            pl.BlockSpec((tr, qw), lambda i: (rb + i, QA_OFF // qw)),
            pl.BlockSpec((tr, kw), lambda i: (rb + i, KA_OFF // kw)),
            tab, tab, tab,
        ],
        out_specs=[pl.BlockSpec((tr, qw), lambda i: (i, 0)), pl.BlockSpec((tr, kw), lambda i: (i, 0))],
        out_shape=[jax.ShapeDtypeStruct((nrows, qw), BF16), jax.ShapeDtypeStruct((nrows, kw), BF16)],
        compiler_params=_cparams(("parallel",)),
        name="rope",
    )(proj, proj, cos, sa, sb)


def _rope_tables(seq):
    half = A_HD // 2
    pos = np.arange(seq)
    row = (pos // GRID_W).astype(np.float32)
    col = (pos % GRID_W).astype(np.float32)
    inv = (ROPE_BASE ** (-np.arange(0, half, 2, dtype=np.float32) / half)).astype(np.float32)
    ang_r = row[:, None] * inv[None, :]
    ang_c = col[:, None] * inv[None, :]
    ang = np.concatenate([ang_r, ang_r, ang_c, ang_c], axis=1).astype(np.float32)
    cos = np.cos(ang).astype(np.float32)
    sin = np.sin(ang).astype(np.float32)
    first = (np.arange(A_HD) % half) < (half // 2)
    sa = np.where(first[None, :], -sin, 0.0).astype(np.float32)
    sb = np.where(first[None, :], 0.0, sin).astype(np.float32)
    return jnp.asarray(cos), jnp.asarray(sa), jnp.asarray(sb)


def _attn_lat_kernel(sink_ref, q_ref, kp_ref, kc_ref, kn_ref, vp_ref, vc_ref, vn_ref, ck_ref, cv_ref, o_ref,
                     *, nblk):
    i = pl.program_id(1)
    qb = WINDOW
    scale = A_HD ** -0.5
    r = lax.broadcasted_iota(jnp.int32, (A_GROUP * qb, 3 * qb), 0) % qb
    c = lax.broadcasted_iota(jnp.int32, (A_GROUP * qb, 3 * qb), 1)
    c_lo = jnp.where(i > 0, 0, qb)
    c_hi = jnp.where(i < nblk - 1, 3 * qb, 2 * qb)
    valid = (c >= r) & (c <= r + 2 * WINDOW) & (c >= c_lo) & (c < c_hi)
    hrow = lax.broadcasted_iota(jnp.int32, (A_GROUP * qb, 1), 0) // qb
    for kvh in range(A_KV_HEADS):
        kc = slice(kvh * A_HD, (kvh + 1) * A_HD)
        q0 = kvh * A_GROUP * A_HD
        q = jnp.concatenate([q_ref[:, q0 + g * A_HD:q0 + (g + 1) * A_HD] for g in range(A_GROUP)], axis=0)
        kwin = jnp.concatenate([kp_ref[:, kc], kc_ref[:, kc], kn_ref[:, kc]], axis=0)
        vwin = jnp.concatenate([vp_ref[:, kc], vc_ref[:, kc], vn_ref[:, kc]], axis=0)
        s_lat = lax.dot_general(q, kwin, (((1,), (1,)), ((), ())), preferred_element_type=F32) * scale
        s_ctx = lax.dot_general(q, ck_ref[0, kvh], (((1,), (1,)), ((), ())), preferred_element_type=F32) * scale
        s_lat = jnp.where(valid, s_lat, NEG_INIT)
        sk = jnp.zeros((A_GROUP * qb, 1), F32)
        for g in range(A_GROUP):
            sk = jnp.where(hrow == g, sink_ref[kvh * A_GROUP + g], sk)
        mx = jnp.maximum(jnp.maximum(jnp.max(s_lat, axis=1, keepdims=True),
                                     jnp.max(s_ctx, axis=1, keepdims=True)), sk)
        p_lat = jnp.exp(s_lat - mx)
        p_ctx = jnp.exp(s_ctx - mx)
        den = jnp.sum(p_lat, axis=1, keepdims=True) + jnp.sum(p_ctx, axis=1, keepdims=True) + jnp.exp(sk - mx)
        o = (jnp.dot(p_lat.astype(BF16), vwin, preferred_element_type=F32)
             + jnp.dot(p_ctx.astype(BF16), cv_ref[0, kvh], preferred_element_type=F32)) / den
        for g in range(A_GROUP):
            o_ref[:, q0 + g * A_HD:q0 + (g + 1) * A_HD] = o[g * qb:(g + 1) * qb].astype(BF16)


def _attn_lat(qr, kr, proj, ck, cv, sink, row_off, nseq, seq):
    qb = WINDOW
    nblk = seq // qb
    qw = A_HEADS * A_HD
    kw = A_KV_HEADS * A_HD
    rb = row_off // qb
    p_len = ck.shape[2]
    prev = lambda b, i: b * nblk + jnp.maximum(i - 1, 0)
    cur = lambda b, i: b * nblk + i
    nxt = lambda b, i: b * nblk + jnp.minimum(i + 1, nblk - 1)
    vcol = VA_OFF // kw
    return pl.pallas_call(
        functools.partial(_attn_lat_kernel, nblk=nblk),
        grid=(nseq, nblk),
        in_specs=[
            pl.BlockSpec(memory_space=pltpu.SMEM),
            pl.BlockSpec((qb, qw), lambda b, i: (cur(b, i), 0)),
            pl.BlockSpec((qb, kw), lambda b, i: (prev(b, i), 0)),
            pl.BlockSpec((qb, kw), lambda b, i: (cur(b, i), 0)),
            pl.BlockSpec((qb, kw), lambda b, i: (nxt(b, i), 0)),
            pl.BlockSpec((qb, kw), lambda b, i: (rb + prev(b, i), vcol)),
            pl.BlockSpec((qb, kw), lambda b, i: (rb + cur(b, i), vcol)),
            pl.BlockSpec((qb, kw), lambda b, i: (rb + nxt(b, i), vcol)),
            pl.BlockSpec((1, A_KV_HEADS, p_len, A_HD), lambda b, i: (b, 0, 0, 0)),
            pl.BlockSpec((1, A_KV_HEADS, p_len, A_HD), lambda b, i: (b, 0, 0, 0)),
        ],
        out_specs=pl.BlockSpec((qb, qw), lambda b, i: (cur(b, i), 0)),
        out_shape=jax.ShapeDtypeStruct((nseq * seq, qw), BF16),
        compiler_params=_cparams(("parallel", "parallel")),
        name="attn_lat",
    )(sink, qr, kr, kr, kr, proj, proj, proj, ck, cv)


def _outproj_kernel(*refs, alpha, na):
    tm = refs[0].shape[0]
    for s in range(tm // SUB_ROWS):
        _outproj_rows(slice(s * SUB_ROWS, (s + 1) * SUB_ROWS), *refs, alpha=alpha, na=na)


def _outproj_rows(rows, mmc_ref, mml_ref, mac_ref, mal_ref, w_ref, xa_ref, xb_ref, mod_ref, lng_ref, lnb_ref, wr_ref,
                  br_ref, x1_ref, xw_ref, bkt_ref, *, alpha, na):
    is_a = pl.program_id(0) < na
    d = xa_ref.shape[1]
    half = w_ref.shape[1] // 2
    mix_m = jnp.where(is_a, mmc_ref[rows, :], mml_ref[rows, :])
    mix_a = jnp.where(is_a, mac_ref[rows, :], mal_ref[rows, :])
    f = (jnp.dot(mix_m, w_ref[0, :half, :], preferred_element_type=F32)
         + jnp.dot(mix_a, w_ref[0, half:, :], preferred_element_type=F32))
    x = jnp.where(is_a, xa_ref[rows, :], xb_ref[rows, :])
    z = alpha * x + mod_ref[0, 2:3, :] * f
    x1 = _ln_plain(z) * lng_ref[...] + lnb_ref[...]
    x1_ref[rows, :] = x1
    h2 = _ln_plain(x1) * (1.0 + mod_ref[0, 4:5, :]) + mod_ref[0, 3:4, :]
    xw_ref[rows, :d] = h2
    logits = jnp.dot(h2.astype(BF16), wr_ref[...], preferred_element_type=F32)
    scores = jax.nn.sigmoid(logits)
    sel = scores + br_ref[...]
    sc_t = scores.T
    sel_t = sel.T
    sv = [sel_t[e:e + 1, :] for e in range(N_EXPERTS)]
    cv = [sc_t[e:e + 1, :] for e in range(N_EXPERTS)]
    gs = []
    for g in range(N_GROUPS):
        v = sv[4 * g:4 * g + 4]
        best = None
        for a, b in zip(PAIR_LO, PAIR_HI):
            ps = v[a] + v[b]
            best = ps if best is None else jnp.maximum(best, ps)
        gs.append(best)
    gmax = jnp.maximum(jnp.maximum(gs[0], gs[1]), jnp.maximum(gs[2], gs[3]))
    grp = jnp.full(gmax.shape, N_GROUPS - 1, jnp.int32)
    for g in range(N_GROUPS - 2, -1, -1):
        grp = jnp.where(gs[g] == gmax, g, grp)
    def pick(vals, k):
        out = vals[k]
        for g in range(1, N_GROUPS):
            out = jnp.where(grp == g, vals[4 * g + k], out)
        return out
    gv = [pick(sv, k) for k in range(EXPERTS_PER_GROUP)]
    gc = [pick(cv, k) for k in range(EXPERTS_PER_GROUP)]
    m1 = jnp.maximum(jnp.maximum(gv[0], gv[1]), jnp.maximum(gv[2], gv[3]))
    k0 = jnp.full(m1.shape, 3, jnp.int32)
    for k in range(2, -1, -1):
        k0 = jnp.where(gv[k] == m1, k, k0)
    gv2 = [jnp.where(k0 == k, -jnp.inf, gv[k]) for k in range(4)]
    m2 = jnp.maximum(jnp.maximum(gv2[0], gv2[1]), jnp.maximum(gv2[2], gv2[3]))
    k1 = jnp.full(m1.shape, 3, jnp.int32)
    for k in range(2, -1, -1):
        k1 = jnp.where(gv2[k] == m2, k, k1)
    lo = jnp.minimum(k0, k1)
    hi = jnp.maximum(k0, k1)
    pair = jnp.where(lo == 0, hi - 1, jnp.where(lo == 1, hi + 1, 5))
    bkt_ref[:, rows] = grp * 6 + pair
    s_lo = jnp.where(lo == 0, gc[0], jnp.where(lo == 1, gc[1], gc[2]))
    s_hi = jnp.where(hi == 1, gc[1], jnp.where(hi == 2, gc[2], gc[3]))
    tot = s_lo + s_hi
    wrow = lax.broadcasted_iota(jnp.int32, (LANES, tot.shape[1]), 0)
    w_t = jnp.where(wrow == 0, s_lo / tot, jnp.where(wrow == 1, s_hi / tot, 0.0))
    xw_ref[rows, d:] = w_t.T


def _outproj(mixes, w_out, layer, xs, t, modl, cond_of_tile, ln_g, ln_b, w_router, b_router, alpha, tm):
    x_a, x_b, na, off_b = xs
    d = x_a.shape[1]
    hw = mixes[0].shape[1]
    row = lambda i: (i, 0)
    const = lambda i: (0, 0)
    mix_a, mix_b = _split_specs(tm, hw, na, 0)
    x_sa, x_sb = _split_specs(tm, d, na, off_b)
    return pl.pallas_call(
        functools.partial(_outproj_kernel, alpha=alpha, na=na),
        grid=(t // tm,),
        in_specs=[
            mix_a, mix_b, mix_a, mix_b,
            pl.BlockSpec((1, 2 * hw, d), lambda i: (layer, 0, 0), pipeline_mode=pl.Buffered(1)),
            x_sa, x_sb,
            pl.BlockSpec((1, 6, d), lambda i: (cond_of_tile(i), 0, 0)),
            pl.BlockSpec((1, d), const),
            pl.BlockSpec((1, d), const),
            pl.BlockSpec((d, LANES), const),
            pl.BlockSpec((1, LANES), const),
        ],
        out_specs=[
            pl.BlockSpec((tm, d), row),
            pl.BlockSpec((tm, d + LANES), row),
            pl.BlockSpec((1, tm), lambda i: (0, i)),
        ],
        out_shape=[
            jax.ShapeDtypeStruct((t, d), F32),
            jax.ShapeDtypeStruct((t, d + LANES), F32),
            jax.ShapeDtypeStruct((1, t), jnp.int32),
        ],
        compiler_params=_cparams(("parallel",)),
        name="outproj_route",
    )(mixes[0], mixes[1], mixes[2], mixes[3], w_out, x_a, x_b, modl, ln_g, ln_b, w_router, b_router)


def _moe_kernel(te0_ref, te1_ref, tnv_ref, pg_ref, pgn_ref, ps_ref, x_hbm, wgu0_ref, wgu1_ref, wd0_ref, wd1_ref,
                y_hbm, xbuf, ybuf, gsem, ssem, *, tm_e, d, d_exp, n_tiles, t):
    g = pl.program_id(0)
    nv = tnv_ref[g]
    slot = lax.rem(g, 2)
    oslot = 1 - slot

    def start_rows(copy_of_row, idx_ref):
        for r in range(tm_e):
            copy_of_row(r, idx_ref[0, 0, r]).start(priority=r % 2)

    def start_gather(sl, idx_ref):
        start_rows(lambda r, idx: pltpu.make_async_copy(x_hbm.at[pl.ds(idx, 1)], xbuf.at[sl, pl.ds(r, 1)],
                                                        gsem.at[sl]), idx_ref)

    def start_scatter(sl, idx_ref):
        start_rows(lambda r, idx: pltpu.make_async_copy(ybuf.at[sl, pl.ds(r, 1)], y_hbm.at[pl.ds(idx, 1)],
                                                        ssem.at[sl]), idx_ref)

    def wait_gather(sl):
        pltpu.make_async_copy(x_hbm.at[pl.ds(0, tm_e)], xbuf.at[sl], gsem.at[sl]).wait()

    def wait_scatter(sl):
        pltpu.make_async_copy(ybuf.at[sl], y_hbm.at[pl.ds(0, tm_e)], ssem.at[sl]).wait()

    @pl.when(g == 0)
    def _():
        ybuf[1] = jnp.zeros((tm_e, d), F32)
        for k in range(2):
            pltpu.make_async_copy(ybuf.at[1], y_hbm.at[pl.ds(t + k * tm_e, tm_e)], ssem.at[1]).start()
        for k in range(2):
            wait_scatter(1)
        start_gather(0, pg_ref)

    @pl.when(nv > 0)
    def _():
        g_next = jnp.minimum(g + 1, n_tiles - 1)
        has_next = jnp.logical_and(g + 1 < n_tiles, tnv_ref[g_next] > 0)

        wait_gather(slot)
        start_gather(oslot, pgn_ref)
        xb = xbuf[slot, :, :d].astype(BF16)
        y = None
        for k, (wgu_ref, wd_ref) in enumerate(((wgu0_ref, wd0_ref), (wgu1_ref, wd1_ref))):
            gu = jnp.dot(xb, wgu_ref[0, 0], preferred_element_type=F32)
            gt = gu[:, :d_exp]
            a = (gt * jax.nn.sigmoid(gt)) * gu[:, d_exp:]
            ye = jnp.dot(a.astype(BF16), wd_ref[0, 0], preferred_element_type=F32)
            ye = xbuf[slot, :, d + k:d + k + 1] * ye
            y = ye if y is None else y + ye
        ybuf[slot] = y
        start_scatter(slot, ps_ref)

        @pl.when(g > 0)
        def _():
            wait_scatter(oslot)

        @pl.when(jnp.logical_not(has_next))
        def _():
            wait_scatter(slot)
            wait_gather(oslot)


def _moe(xw, bkt, wgu, wd, layer, tm_e):
    t, dw = xw.shape
    d = dw - LANES
    d_exp = wd.shape[2]
    n_tiles = t // tm_e + N_BUCKETS
    n_rows = n_tiles * tm_e
    ids = jnp.arange(N_BUCKETS, dtype=jnp.int32)
    onehot = (bkt[:, None] == ids[None, :]).astype(jnp.int32)
    counts = jnp.sum(onehot, axis=0)
    tiles_b = (counts + tm_e - 1) // tm_e
    tile_end = jnp.cumsum(tiles_b)
    row_start = (tile_end - tiles_b) * tm_e
    blk = tm_e
    oh3 = onehot.astype(F32).reshape(t // blk, blk, N_BUCKETS)
    tri = (jnp.arange(blk)[:, None] >= jnp.arange(blk)[None, :]).astype(F32)
    local = jnp.einsum("ij,bjk->bik", tri, oh3)
    before = jnp.cumsum(local[:, -1, :], axis=0) - local[:, -1, :]
    running = (local + before[:, None, :]).astype(jnp.int32).reshape(t, N_BUCKETS)
    rank = jnp.take_along_axis(running, bkt[:, None], axis=1)[:, 0] - 1
    pos = row_start[bkt] + rank
    tile_ids = jnp.arange(n_tiles, dtype=jnp.int32)
    used = tile_end[-1]
    tb = jnp.sum((tile_end[None, :] <= jnp.minimum(tile_ids, used - 1)[:, None]).astype(jnp.int32), axis=1)
    tile_in_b = tile_ids - (tile_end - tiles_b)[tb]
    tnv = jnp.where(tile_ids < used, jnp.clip(counts[tb] - tile_in_b * tm_e, 0, tm_e), 0).astype(jnp.int32)
    lo = jnp.asarray(PAIR_LO, jnp.int32)
    hi = jnp.asarray(PAIR_HI, jnp.int32)
    te0 = (tb // 6) * EXPERTS_PER_GROUP + lo[tb % 6]
    te1 = (tb // 6) * EXPERTS_PER_GROUP + hi[tb % 6]
    tok = jnp.arange(t, dtype=jnp.int32)
    slot_row = jnp.arange(n_rows, dtype=jnp.int32)
    dump = t + ((slot_row // tm_e) % 2) * tm_e + slot_row % tm_e
    perm_s = dump.at[pos].set(tok)
    perm_g = jnp.where(perm_s < t, perm_s, 0)
    pg3 = perm_g.reshape(n_tiles, 1, tm_e)
    ps3 = perm_s.reshape(n_tiles, 1, tm_e)

    grid_spec = pltpu.PrefetchScalarGridSpec(
        num_scalar_prefetch=3,
        grid=(n_tiles,),
        in_specs=[
            pl.BlockSpec((1, 1, tm_e), lambda g, *_: (g, 0, 0), memory_space=pltpu.SMEM),
            pl.BlockSpec((1, 1, tm_e), lambda g, *_: (jnp.minimum(g + 1, n_tiles - 1), 0, 0),
                         memory_space=pltpu.SMEM),
            pl.BlockSpec((1, 1, tm_e), lambda g, *_: (g, 0, 0), memory_space=pltpu.SMEM),
            pl.BlockSpec(memory_space=pl.ANY),
            pl.BlockSpec((1, 1, d, 2 * d_exp), lambda g, e0, e1, tv: (layer, e0[g], 0, 0)),
            pl.BlockSpec((1, 1, d, 2 * d_exp), lambda g, e0, e1, tv: (layer, e1[g], 0, 0)),
            pl.BlockSpec((1, 1, d_exp, d), lambda g, e0, e1, tv: (layer, e0[g], 0, 0)),
            pl.BlockSpec((1, 1, d_exp, d), lambda g, e0, e1, tv: (layer, e1[g], 0, 0)),
        ],
        out_specs=pl.BlockSpec(memory_space=pl.ANY),
        scratch_shapes=[
            pltpu.VMEM((2, tm_e, dw), F32),
            pltpu.VMEM((2, tm_e, d), F32),
            pltpu.SemaphoreType.DMA((2,)),
            pltpu.SemaphoreType.DMA((2,)),
        ],
    )
    y = pl.pallas_call(
        functools.partial(_moe_kernel, tm_e=tm_e, d=d, d_exp=d_exp, n_tiles=n_tiles, t=t),
        grid_spec=grid_spec,
        out_shape=jax.ShapeDtypeStruct((t + 2 * tm_e, d), F32),
        compiler_params=_cparams(("arbitrary",)),
        name="moe_experts",
    )(te0, te1, tnv, pg3, pg3, ps3, xw, wgu, wgu, wd, wd)
    return y


def _final_kernel(x1_ref, y_ref, mod_ref, g_ref, b_ref, *o_refs, alpha, na):
    z = alpha * x1_ref[...] + mod_ref[0, 5:6, :] * y_ref[...]
    out = _ln_plain(z) * g_ref[...] + b_ref[...]
    if len(o_refs) == 1:
        o_refs[0][...] = out
    else:
        @pl.when(pl.program_id(0) < na)
        def _():
            o_refs[0][...] = out

        @pl.when(pl.program_id(0) >= na)
        def _():
            o_refs[1][...] = out


def _final_ln(x1, y, modl, cond_of_tile, ln_g, ln_b, alpha, tm, split_at=None):
    t, d = x1.shape
    row = lambda i: (i, 0)
    const = lambda i: (0, 0)
    if split_at is None:
        out_specs = pl.BlockSpec((tm, d), row)
        out_shape = jax.ShapeDtypeStruct((t, d), F32)
    else:
        out_specs = list(_split_specs(tm, d, split_at, 0))
        out_shape = [jax.ShapeDtypeStruct((split_at * tm, d), F32), jax.ShapeDtypeStruct((t - split_at * tm, d), F32)]
    return pl.pallas_call(
        functools.partial(_final_kernel, alpha=alpha, na=split_at),
        grid=(t // tm,),
        in_specs=[
            pl.BlockSpec((tm, d), row),
            pl.BlockSpec((tm, d), row),
            pl.BlockSpec((1, 6, d), lambda i: (cond_of_tile(i), 0, 0)),
            pl.BlockSpec((1, d), const),
            pl.BlockSpec((1, d), const),
        ],
        out_specs=out_specs,
        out_shape=out_shape,
        compiler_params=_cparams(("arbitrary",)),
        name="final_ln",
    )(x1, y, modl, ln_g, ln_b)


def _gate_columns():
    src_i = np.zeros((8 * N_PAIRS,), np.int32)
    src_f = np.zeros((8 * N_PAIRS,), np.int32)
    for hp in range(N_PAIRS):
        for dd in range(2):
            for hh in range(2):
                j = hp * 8 + dd * 2 + hh
                head = hp * 2 + hh
                src_i[j] = dd * M_HEADS + head
                src_f[j] = (2 + dd) * M_HEADS + head
    return src_i, src_f


def kernel(x_prompt, x_sample, cache_k, cache_v, state_C, state_n, state_m, c, c_ctx, w_ada, b_ada, w_in, b_gate,
           mh_norm_g, attn_sink, w_out, ln1_g, ln1_b, ln2_g, ln2_b, w_router, b_router, w_exp_gate, w_exp_up,
           w_exp_down):
    batch, seq, d = x_prompt.shape
    dec_batch, dec_seq, _ = x_sample.shape
    depth = w_in.shape[0]
    n_ctx = batch * seq
    n_lat = dec_batch * dec_seq
    t = n_ctx + n_lat
    alpha = (2 * depth) ** 0.25
    tm = 512
    tm_e = 256
    assert n_ctx % tm == 0 and dec_seq % tm == 0 and seq % CHUNK == 0 and dec_seq % CHUNK == 0
    assert n_ctx % dec_seq == 0 and n_ctx % WINDOW == 0 and t % tm_e == 0

    n_ctx_tiles = n_ctx // tm
    tiles_per_seq = dec_seq // tm

    def cond_of_tile(i):
        return jnp.where(i < n_ctx_tiles, 0, 1 + (i - n_ctx_tiles) // tiles_per_seq)

    n_cond = 1 + dec_batch
    cond_rows = -(-n_cond // 8) * 8
    cond = jnp.concatenate([c_ctx[None, :], c, jnp.zeros((cond_rows - n_cond, d), F32)], axis=0)
    mod = _ada_mod(cond, w_ada, b_ada).reshape(depth, cond_rows, 6, d)

    xs = (x_prompt.reshape(n_ctx, d), x_sample.reshape(n_lat, d), n_ctx_tiles, 0)

    src_i, src_f = _gate_columns()
    rope_tab = _rope_tables(dec_seq)
    wr = jnp.pad(w_router, ((0, 0), (0, LANES - N_EXPERTS))).astype(BF16)
    br = jnp.pad(b_router, (0, LANES - N_EXPERTS)).reshape(1, LANES)

    kscale = M_DK ** -0.5
    w_main = jnp.concatenate(
        [w_in[:, :, :KM_OFF], w_in[:, :, KM_OFF:VM_OFF] * kscale, w_in[:, :, VM_OFF:GATE_SRC_OFF],
         w_in[:, :, GATE_SRC_OFF + 4 * M_HEADS:]], axis=2).astype(BF16)
    w_out_b = w_out.astype(BF16)
    wgu = jnp.concatenate([w_exp_gate, w_exp_up], axis=-1).astype(BF16)
    wd = w_exp_down.astype(BF16)

    ks, vs, cs, ns, ms = [], [], [], [], []
    for l in range(depth):
        wgt = w_in[l][:, GATE_SRC_OFF:GATE_SRC_OFF + 4 * M_HEADS]
        zpad = jnp.zeros((d, LANES - 8 * N_PAIRS), F32)
        w_gate = jnp.concatenate([wgt[:, src_i], zpad, wgt[:, src_f], zpad], axis=1).astype(BF16)
        bpad = jnp.zeros((LANES - 8 * N_PAIRS,), F32)
        bg = jnp.concatenate([b_gate[l][src_i], bpad, b_gate[l][src_f], bpad]).reshape(1, 2 * LANES)
        modl = mod[l]

        proj, gates, kv = _inproj(xs, t, modl, cond_of_tile, w_main, l, w_gate, bg, tm)
        gb, gu, gut = _gate_prep(gates, 1024)

        gain = mh_norm_g[l].reshape(1, M_HEADS * M_DV)
        sink = attn_sink[l]
        hm_c, c_new, n_new, m_new = _mlstm(proj, gb, gu, gut, gain, 0, batch, seq, None, True)
        at_c = _attn_ctx(proj, sink, batch, seq, 4)
        c0 = state_C[:, l].reshape(dec_batch, 2, N_PAIRS, PAIR_ROWS, M_DV)
        n0 = state_n[:, l].reshape(dec_batch, 2, N_PAIRS, PAIR_ROWS, 1)
        m0 = state_m[:, l].reshape(dec_batch * 2 * M_HEADS)
        (hm_l,) = _mlstm(proj, gb, gu, gut, gain, n_ctx, dec_batch, dec_seq, (c0, n0, m0), False)
        qr, kr = _rope(proj, rope_tab, n_ctx, n_lat, dec_seq, 512)
        ck = jnp.transpose(cache_k[:, l], (0, 2, 1, 3)).astype(BF16)
        cv = jnp.transpose(cache_v[:, l], (0, 2, 1, 3)).astype(BF16)
        at_l = _attn_lat(qr, kr, proj, ck, cv, sink, n_ctx, dec_batch, dec_seq)

        x1, xw, bkt = _outproj((hm_c, hm_l, at_c, at_l), w_out_b, l, xs, t, modl, cond_of_tile,
                               ln1_g[l].reshape(1, d), ln1_b[l].reshape(1, d), wr, br, alpha, tm)
        y = _moe(xw, bkt[0], wgu, wd, l, tm_e)
        last = l == depth - 1
        x_new = _final_ln(x1, y, modl, cond_of_tile, ln2_g[l].reshape(1, d), ln2_b[l].reshape(1, d), alpha, tm,
                          split_at=n_ctx_tiles if last else None)
        if last:
            y_ctx, y_lat = x_new
        else:
            xs = (x_new, x_new, n_ctx_tiles, n_ctx_tiles)

        ks.append(kv[:n_ctx, :A_KV_HEADS * A_HD].reshape(batch, seq, A_KV_HEADS, A_HD))
        vs.append(kv[:n_ctx, A_KV_HEADS * A_HD:].reshape(batch, seq, A_KV_HEADS, A_HD))
        cs.append(c_new.reshape(batch, 2, M_HEADS, M_DK, M_DV))
        ns.append(n_new.reshape(batch, 2, M_HEADS, M_DK))
        m4 = m_new[:, :, :4, 0].reshape(batch, N_PAIRS, 2, 2)
        ms.append(jnp.transpose(m4, (0, 2, 1, 3)).reshape(batch, 2, M_HEADS))

    y_prompt = y_ctx.reshape(batch, seq, d)
    y_sample = y_lat.reshape(dec_batch, dec_seq, d)
    return (y_prompt, y_sample, jnp.stack(ks, 1), jnp.stack(vs, 1), jnp.stack(cs, 1), jnp.stack(ns, 1),
            jnp.stack(ms, 1))
```

```python
import functools

import jax
import jax.numpy as jnp
import numpy as np
from jax import lax
from jax.experimental import pallas as pl
from jax.experimental.pallas import tpu as pltpu

F32 = jnp.float32
BF16 = jnp.bfloat16

M_HEADS = 8
M_DK = 64
M_DV = 128
IGATE_CAP = 15.0
MH_EPS = 1e-6
NEG_INIT = -1e30
A_HEADS = 8
A_KV_HEADS = 2
A_GROUP = A_HEADS // A_KV_HEADS
A_HD = 128
WINDOW = 128
GRID_W = 64
ROPE_BASE = 10000.0
N_EXPERTS = 16
N_GROUPS = 4
EXPERTS_PER_GROUP = N_EXPERTS // N_GROUPS
LN_EPS = 1e-5

LANES = 128
VMEM_LIMIT = 56 * 1024 * 1024

QM_OFF = 0
KM_OFF = M_HEADS * M_DK
VM_OFF = KM_OFF + M_HEADS * M_DK
OM_OFF = VM_OFF + M_HEADS * M_DV
QA_OFF = OM_OFF + M_HEADS * M_DV
KA_OFF = QA_OFF + A_HEADS * A_HD
VA_OFF = KA_OFF + A_KV_HEADS * A_HD
MAIN_COLS = VA_OFF + A_KV_HEADS * A_HD
GATE_SRC_OFF = OM_OFF + M_HEADS * M_DV
N_PAIRS = M_HEADS // 2
PAIR_ROWS = 2 * M_DK
CHUNK = 128
SUB_ROWS = 256
N_BUCKETS = N_GROUPS * 6
PAIR_LO = (0, 0, 0, 1, 1, 2)
PAIR_HI = (1, 2, 3, 2, 3, 3)


def _cparams(sem):
    return pltpu.CompilerParams(dimension_semantics=sem, vmem_limit_bytes=VMEM_LIMIT)


def _ln_plain(x):
    mu = jnp.mean(x, axis=-1, keepdims=True)
    xc = x - mu
    return xc * lax.rsqrt(jnp.mean(xc * xc, axis=-1, keepdims=True) + LN_EPS)


def _ada_kernel(c_ref, w_ref, b_ref, o_ref):
    c = c_ref[...]
    s = (c * jax.nn.sigmoid(c)).astype(BF16)
    o_ref[0] = jnp.dot(s, w_ref[0].astype(BF16), preferred_element_type=F32) + b_ref[0]


def _ada_mod(cond, w_ada, b_ada):
    depth, d, n = w_ada.shape
    rows = cond.shape[0]
    tn = 2048
    return pl.pallas_call(
        _ada_kernel,
        grid=(depth, n // tn),
        in_specs=[
            pl.BlockSpec((rows, d), lambda l, j: (0, 0)),
            pl.BlockSpec((1, d, tn), lambda l, j: (l, 0, j)),
            pl.BlockSpec((1, 1, tn), lambda l, j: (l, 0, j)),
        ],
        out_specs=pl.BlockSpec((1, rows, tn), lambda l, j: (l, 0, j)),
        out_shape=jax.ShapeDtypeStruct((depth, rows, n), F32),
        compiler_params=_cparams(("parallel", "parallel")),
        name="ada_mod",
    )(cond, w_ada, b_ada.reshape(depth, 1, n))


def _split_specs(tm, width, na, off_b):
    spec_a = pl.BlockSpec((tm, width), lambda i, *_: (jnp.minimum(i, na - 1), 0))
    spec_b = pl.BlockSpec((tm, width), lambda i, *_: (jnp.maximum(i - na, 0) + off_b, 0))
    return spec_a, spec_b


def _inproj_kernel(xa_ref, xb_ref, mod_ref, w_ref, wg_ref, bg_ref, proj_ref, gates_ref, kv_ref, h_scr, *, tn, na):
    is_a = pl.program_id(0) < na
    nj = MAIN_COLS // tn
    for s in range(xa_ref.shape[0] // SUB_ROWS):
        rows = slice(s * SUB_ROWS, (s + 1) * SUB_ROWS)
        x = jnp.where(is_a, xa_ref[rows, :], xb_ref[rows, :])
        hn = _ln_plain(x)
        h = hn * (1.0 + mod_ref[0, 1:2, :]) + mod_ref[0, 0:1, :]
        hb = h.astype(BF16)
        h_scr[rows, :] = hb
        gates_ref[rows, :] = jnp.dot(hb, wg_ref[...], preferred_element_type=F32) + bg_ref[...]
        for j in range(nj):
            cols = slice(j * tn, (j + 1) * tn)
            acc = jnp.dot(h_scr[rows, :], w_ref[0, :, cols], preferred_element_type=F32)
            proj_ref[rows, cols] = acc.astype(BF16)
            if j == nj - 1:
                kv_ref[rows, :] = acc


def _inproj(xs, t, modl, cond_of_tile, w_main, layer, w_gate, b_gate, tm):
    x_a, x_b, na, off_b = xs
    d = x_a.shape[1]
    tn = MAIN_COLS - KA_OFF
    assert MAIN_COLS % tn == 0
    spec_a, spec_b = _split_specs(tm, d, na, off_b)
    const = lambda i: (0, 0)
    resident = pl.Buffered(1)
    return pl.pallas_call(
        functools.partial(_inproj_kernel, tn=tn, na=na),
        grid=(t // tm,),
        in_specs=[
            spec_a,
            spec_b,
            pl.BlockSpec((1, 6, d), lambda i: (cond_of_tile(i), 0, 0)),
            pl.BlockSpec((1, d, MAIN_COLS), lambda i: (layer, 0, 0), pipeline_mode=resident),
            pl.BlockSpec((d, 2 * LANES), const, pipeline_mode=resident),
            pl.BlockSpec((1, 2 * LANES), const, pipeline_mode=resident),
        ],
        out_specs=[
            pl.BlockSpec((tm, MAIN_COLS), lambda i: (i, 0)),
            pl.BlockSpec((tm, 2 * LANES), lambda i: (i, 0)),
            pl.BlockSpec((tm, tn), lambda i: (i, 0)),
        ],
        out_shape=[
            jax.ShapeDtypeStruct((t, MAIN_COLS), BF16),
            jax.ShapeDtypeStruct((t, 2 * LANES), F32),
            jax.ShapeDtypeStruct((t, tn), F32),
        ],
        scratch_shapes=[pltpu.VMEM((tm, d), BF16)],
        compiler_params=_cparams(("parallel",)),
        name="inproj",
    )(x_a, x_b, modl, w_main, w_gate, b_gate)


def _gate_kernel(g_ref, gb_ref, gu_ref, gut_ref, *, tg):
    gi = g_ref[:, :LANES]
    gf = g_ref[:, LANES:]
    ig = IGATE_CAP * jnp.tanh(gi / IGATE_CAP)
    lf = jax.nn.log_sigmoid(gf)
    lane = lax.broadcasted_iota(jnp.int32, (1, LANES), 1)
    is_fwd = (lane % 8) < 2
    s_i = lax.broadcasted_iota(jnp.int32, (CHUNK, CHUNK), 0)
    r_i = lax.broadcasted_iota(jnp.int32, (CHUNK, CHUNK), 1)
    tri_lo = (r_i <= s_i).astype(F32)
    tri_hi = (r_i >= s_i).astype(F32)
    for c in range(tg // CHUNK):
        rows = slice(c * CHUNK, (c + 1) * CHUNK)
        lfc = lf[rows]
        pre = jnp.dot(tri_lo, lfc, preferred_element_type=F32, precision=lax.Precision.HIGHEST)
        suf = jnp.dot(tri_hi, lfc, preferred_element_type=F32, precision=lax.Precision.HIGHEST)
        b = jnp.where(is_fwd, pre, suf)
        u = ig[rows] - b
        gb_ref[rows, :] = b
        gu_ref[rows, :] = u
        gut_ref[:, rows] = u.T[: 8 * N_PAIRS]


def _gate_prep(gates, tg):
    t = gates.shape[0]
    return pl.pallas_call(
        functools.partial(_gate_kernel, tg=tg),
        grid=(t // tg,),
        in_specs=[pl.BlockSpec((tg, 2 * LANES), lambda i: (i, 0))],
        out_specs=[
            pl.BlockSpec((tg, LANES), lambda i: (i, 0)),
            pl.BlockSpec((tg, LANES), lambda i: (i, 0)),
            pl.BlockSpec((8 * N_PAIRS, tg), lambda i: (0, i)),
        ],
        out_shape=[
            jax.ShapeDtypeStruct((t, LANES), F32),
            jax.ShapeDtypeStruct((t, LANES), F32),
            jax.ShapeDtypeStruct((8 * N_PAIRS, t), F32),
        ],
        compiler_params=_cparams(("parallel",)),
        name="gate_prep",
    )(gates)


def _mlstm_kernel(*refs, seq, has_state, emit_state):
    it = iter(refs)
    q_ref, k_ref, v_ref, om_ref, gb_ref, gu_ref, gut_ref, gain_ref = (next(it) for _ in range(8))
    if has_state:
        c0_ref, n0_ref, m0_ref = next(it), next(it), next(it)
    out_ref = next(it)
    if emit_state:
        cout_ref, nout_ref, mout_ref = next(it), next(it), next(it)
    h_scr, cst_scr = next(it), next(it)

    b_id = pl.program_id(0)
    hp = pl.program_id(1)
    nc = seq // CHUNK
    L = CHUNK

    lane = lax.broadcasted_iota(jnp.int32, (1, LANES), 1)
    row128 = lax.broadcasted_iota(jnp.int32, (PAIR_ROWS, 1), 0)
    s_i = lax.broadcasted_iota(jnp.int32, (L, L), 0)
    r_i = lax.broadcasted_iota(jnp.int32, (L, L), 1)
    e0row = (lane == 0).astype(F32)
    e0blk = jnp.broadcast_to(e0row, (L, LANES)).astype(BF16)

    def pick_lane(x, j):
        return jnp.sum(jnp.where(lane == j, x, 0.0), axis=1, keepdims=True)

    m_init = []
    for d in range(2):
        for hh in range(2):
            if has_state:
                rowmask = (row128 // M_DK) == hh
                cst_scr[d * 2 + hh, :, :LANES] = jnp.where(rowmask, c0_ref[0, d, 0], 0.0)
                cst_scr[d * 2 + hh, :, LANES:] = jnp.where(rowmask, n0_ref[0, d, 0], 0.0) * e0row
                m0 = m0_ref[b_id * (2 * M_HEADS) + d * M_HEADS + hp * 2 + hh]
                m_init.append(jnp.full((1, 1), m0, F32))
            else:
                cst_scr[d * 2 + hh] = jnp.zeros((PAIR_ROWS, 2 * LANES), F32)
                m_init.append(jnp.full((1, 1), NEG_INIT, F32))

    def body(t, ms):
        new_ms = []
        for d in range(2):
            tri = (r_i <= s_i) if d == 0 else (r_i >= s_i)
            c = t if d == 0 else nc - 1 - t
            r0 = pl.multiple_of(c * L, L)
            q2 = q_ref[pl.ds(r0, L), :]
            k2 = k_ref[pl.ds(r0, L), :]
            gbc = gb_ref[pl.ds(r0, L), :]
            guc = gu_ref[pl.ds(r0, L), :]
            gend = gb_ref[pl.ds(r0 + (L - 1 if d == 0 else 0), 1), :]
            for hh in range(2):
                si = d * 2 + hh
                jl = hp * 8 + d * 2 + hh
                lm = (lane // M_DK) == hh
                qh = jnp.where(lm, q2, jnp.zeros_like(q2))
                kh = jnp.where(lm, k2, jnp.zeros_like(k2))
                vh = v_ref[pl.ds(r0, L), hh * LANES:(hh + 1) * LANES]
                vext = jnp.concatenate([vh, e0blk], axis=1)
                urow = gut_ref[d * 2 + hh:d * 2 + hh + 1, pl.ds(r0, L)]
                ucol = pick_lane(guc, jl)
                bcol = pick_lane(gbc, jl)
                g = pick_lane(gend, jl)
                m = ms[si]
                umat = jnp.where(tri, urow, -jnp.inf)
                cmu = jnp.max(umat, axis=1, keepdims=True)
                mm = jnp.maximum(m, cmu)
                w = jnp.exp(umat - mm)
                sqk = lax.dot_general(qh, kh, (((1,), (1,)), ((), ())), preferred_element_type=F32)
                p = (sqk * w).astype(BF16)
                intra = jnp.dot(p, vext, preferred_element_type=F32)
                cs = cst_scr[si]
                inter = jnp.dot(qh, cs.astype(BF16), preferred_element_type=F32)
                nd = intra + jnp.exp(m - mm) * inter
                num = nd[:, :LANES]
                den = nd[:, LANES:LANES + 1]
                hv = num / jnp.maximum(jnp.abs(den), jnp.exp(-bcol - mm))
                h_scr[d, pl.ds(r0, L), hh * LANES:(hh + 1) * LANES] = hv
                maxu = jnp.max(urow, axis=1, keepdims=True)
                m_new = g + jnp.maximum(m, maxu)
                wa = jnp.exp(g + ucol - m_new)
                wc = jnp.exp(g + m - m_new)
                kw = (kh.astype(F32) * wa).astype(BF16)
                upd = lax.dot_general(kw, vext, (((0,), (0,)), ((), ())), preferred_element_type=F32)
                cst_scr[si] = wc * cs + upd
                new_ms.append(m_new)
        return tuple(new_ms)

    ms_fin = lax.fori_loop(0, nc, body, tuple(m_init))

    def combine(c, carry):
        r0 = pl.multiple_of(c * L, L)
        for hh in range(2):
            cols = slice(hh * LANES, (hh + 1) * LANES)
            tot = h_scr[0, pl.ds(r0, L), cols] + h_scr[1, pl.ds(r0, L), cols]
            ms2 = jnp.mean(tot * tot, axis=1, keepdims=True)
            y = tot * lax.rsqrt(ms2 + MH_EPS) * gain_ref[:, cols]
            y = y * jax.nn.sigmoid(om_ref[pl.ds(r0, L), cols].astype(F32))
            out_ref[pl.ds(r0, L), cols] = y.astype(BF16)
        return carry

    lax.fori_loop(0, nc, combine, 0)

    if emit_state:
        for d in range(2):
            cout_ref[0, d, 0] = cst_scr[d * 2, :, :LANES] + cst_scr[d * 2 + 1, :, :LANES]
            nout_ref[0, d, 0] = cst_scr[d * 2, :, LANES:LANES + 1] + cst_scr[d * 2 + 1, :, LANES:LANES + 1]
            for hh in range(2):
                mout_ref[0, 0, d * 2 + hh:d * 2 + hh + 1, :] = jnp.broadcast_to(ms_fin[d * 2 + hh], (1, LANES))
        mout_ref[0, 0, 4:8, :] = jnp.zeros((4, LANES), F32)


def _mlstm(proj, gb, gu, gut, gain, row_off, nseq, seq, state0, emit_state):
    assert row_off % seq == 0
    rb = row_off // seq
    has_state = state0 is not None
    kernel = functools.partial(_mlstm_kernel, seq=seq, has_state=has_state, emit_state=emit_state)
    in_specs = [
        pl.BlockSpec((seq, PAIR_ROWS), lambda b, h: (rb + b, QM_OFF // PAIR_ROWS + h)),
        pl.BlockSpec((seq, PAIR_ROWS), lambda b, h: (rb + b, KM_OFF // PAIR_ROWS + h)),
        pl.BlockSpec((seq, 2 * M_DV), lambda b, h: (rb + b, VM_OFF // (2 * M_DV) + h)),
        pl.BlockSpec((seq, 2 * M_DV), lambda b, h: (rb + b, OM_OFF // (2 * M_DV) + h)),
        pl.BlockSpec((seq, LANES), lambda b, h: (rb + b, 0)),
        pl.BlockSpec((seq, LANES), lambda b, h: (rb + b, 0)),
        pl.BlockSpec((8, seq), lambda b, h: (h, rb + b)),
        pl.BlockSpec((1, 2 * M_DV), lambda b, h: (0, h)),
    ]
    args = [proj, proj, proj, proj, gb, gu, gut, gain]
    if has_state:
        c0, n0, m0 = state0
        in_specs += [
            pl.BlockSpec((1, 2, 1, PAIR_ROWS, M_DV), lambda b, h: (b, 0, h, 0, 0)),
            pl.BlockSpec((1, 2, 1, PAIR_ROWS, 1), lambda b, h: (b, 0, h, 0, 0)),
            pl.BlockSpec(memory_space=pltpu.SMEM),
        ]
        args += [c0, n0, m0]
    out_specs = [pl.BlockSpec((seq, 2 * M_DV), lambda b, h: (b, h))]
    out_shape = [jax.ShapeDtypeStruct((nseq * seq, M_HEADS * M_DV), BF16)]
    if emit_state:
        out_specs += [
            pl.BlockSpec((1, 2, 1, PAIR_ROWS, M_DV), lambda b, h: (b, 0, h, 0, 0)),
            pl.BlockSpec((1, 2, 1, PAIR_ROWS, 1), lambda b, h: (b, 0, h, 0, 0)),
            pl.BlockSpec((1, 1, 8, LANES), lambda b, h: (b, h, 0, 0)),
        ]
        out_shape += [
            jax.ShapeDtypeStruct((nseq, 2, N_PAIRS, PAIR_ROWS, M_DV), F32),
            jax.ShapeDtypeStruct((nseq, 2, N_PAIRS, PAIR_ROWS, 1), F32),
            jax.ShapeDtypeStruct((nseq, N_PAIRS, 8, LANES), F32),
        ]
    return pl.pallas_call(
        kernel,
        grid=(nseq, N_PAIRS),
        in_specs=in_specs,
        out_specs=out_specs,
        out_shape=out_shape,
        scratch_shapes=[pltpu.VMEM((2, seq, 2 * M_DV), F32), pltpu.VMEM((4, PAIR_ROWS, 2 * LANES), F32)],
        compiler_params=_cparams(("parallel", "parallel")),
        name="mlstm",
    )(*args)


def _attn_ctx_kernel(sink_ref, q_ref, k_ref, v_ref, o_ref, *, seq, nb):
    kvh = pl.program_id(1)
    scale = A_HD ** -0.5
    for s in range(nb):
        rows = slice(s * seq, (s + 1) * seq)
        k = k_ref[rows, :]
        v = v_ref[rows, :]
        for g in range(A_GROUP):
            cols = slice(g * A_HD, (g + 1) * A_HD)
            q = q_ref[rows, cols]
            sc = lax.dot_general(q, k, (((1,), (1,)), ((), ())), preferred_element_type=F32) * scale
            sk = sink_ref[kvh * A_GROUP + g]
            mx = jnp.maximum(jnp.max(sc, axis=1, keepdims=True), sk)
            p = jnp.exp(sc - mx)
            den = jnp.sum(p, axis=1, keepdims=True) + jnp.exp(sk - mx)
            o = jnp.dot(p.astype(BF16), v, preferred_element_type=F32) / den
            o_ref[rows, cols] = o.astype(BF16)


def _attn_ctx(proj, sink, nseq, seq, nb):
    gw = A_GROUP * A_HD
    return pl.pallas_call(
        functools.partial(_attn_ctx_kernel, seq=seq, nb=nb),
        grid=(nseq // nb, A_KV_HEADS),
        in_specs=[
            pl.BlockSpec(memory_space=pltpu.SMEM),
            pl.BlockSpec((nb * seq, gw), lambda b, h: (b, QA_OFF // gw + h)),
            pl.BlockSpec((nb * seq, A_HD), lambda b, h: (b, KA_OFF // A_HD + h)),
            pl.BlockSpec((nb * seq, A_HD), lambda b, h: (b, VA_OFF // A_HD + h)),
        ],
        out_specs=pl.BlockSpec((nb * seq, gw), lambda b, h: (b, h)),
        out_shape=jax.ShapeDtypeStruct((nseq * seq, A_HEADS * A_HD), BF16),
        compiler_params=_cparams(("parallel", "parallel")),
        name="attn_ctx",
    )(sink, proj, proj, proj)


def _rope_kernel(q_ref, k_ref, cos_ref, sa_ref, sb_ref, qo_ref, ko_ref):
    cos = cos_ref[...]
    sa = sa_ref[...]
    sb = sb_ref[...]

    def rot(x):
        return x * cos + pltpu.roll(x, LANES - A_HD // 4, 1) * sa + pltpu.roll(x, A_HD // 4, 1) * sb

    for h in range(A_HEADS):
        cols = slice(h * A_HD, (h + 1) * A_HD)
        qo_ref[:, cols] = rot(q_ref[:, cols].astype(F32)).astype(BF16)
    for h in range(A_KV_HEADS):
        cols = slice(h * A_HD, (h + 1) * A_HD)
        ko_ref[:, cols] = rot(k_ref[:, cols].astype(F32)).astype(BF16)


def _rope(proj, tables, row_off, nrows, seq, tr):
    cos, sa, sb = tables
    rb = row_off // tr
    nps = seq // tr
    qw = A_HEADS * A_HD
    kw = A_KV_HEADS * A_HD
    tab = pl.BlockSpec((tr, A_HD), lambda i: (i % nps, 0))
    return pl.pallas_call(
        _rope_kernel,
        grid=(nrows // tr,),
        in_specs=[
            pl.BlockSpec((tr, qw), lambda i: (rb + i, QA_OFF // qw)),
            pl.BlockSpec((tr, kw), lambda i: (rb + i, KA_OFF // kw)),
            tab, tab, tab,
        ],
        out_specs=[pl.BlockSpec((tr, qw), lambda i: (i, 0)), pl.BlockSpec((tr, kw), lambda i: (i, 0))],
        out_shape=[jax.ShapeDtypeStruct((nrows, qw), BF16), jax.ShapeDtypeStruct((nrows, kw), BF16)],
        compiler_params=_cparams(("parallel",)),
        name="rope",
    )(proj, proj, cos, sa, sb)


def _rope_tables(seq):
    half = A_HD // 2
    pos = np.arange(seq)
    row = (pos // GRID_W).astype(np.float32)
    col = (pos % GRID_W).astype(np.float32)
    inv = (ROPE_BASE ** (-np.arange(0, half, 2, dtype=np.float32) / half)).astype(np.float32)
    ang_r = row[:, None] * inv[None, :]
    ang_c = col[:, None] * inv[None, :]
    ang = np.concatenate([ang_r, ang_r, ang_c, ang_c], axis=1).astype(np.float32)
    cos = np.cos(ang).astype(np.float32)
    sin = np.sin(ang).astype(np.float32)
    first = (np.arange(A_HD) % half) < (half // 2)
    sa = np.where(first[None, :], -sin, 0.0).astype(np.float32)
    sb = np.where(first[None, :], 0.0, sin).astype(np.float32)
    return jnp.asarray(cos), jnp.asarray(sa), jnp.asarray(sb)


def _attn_lat_kernel(sink_ref, q_ref, kp_ref, kc_ref, kn_ref, vp_ref, vc_ref, vn_ref, ck_ref, cv_ref, o_ref,
                     *, nblk):
    i = pl.program_id(1)
    qb = WINDOW
    scale = A_HD ** -0.5
    r = lax.broadcasted_iota(jnp.int32, (A_GROUP * qb, 3 * qb), 0) % qb
    c = lax.broadcasted_iota(jnp.int32, (A_GROUP * qb, 3 * qb), 1)
    c_lo = jnp.where(i > 0, 0, qb)
    c_hi = jnp.where(i < nblk - 1, 3 * qb, 2 * qb)
    valid = (c >= r) & (c <= r + 2 * WINDOW) & (c >= c_lo) & (c < c_hi)
    hrow = lax.broadcasted_iota(jnp.int32, (A_GROUP * qb, 1), 0) // qb
    for kvh in range(A_KV_HEADS):
        kc = slice(kvh * A_HD, (kvh + 1) * A_HD)
        q0 = kvh * A_GROUP * A_HD
        q = jnp.concatenate([q_ref[:, q0 + g * A_HD:q0 + (g + 1) * A_HD] for g in range(A_GROUP)], axis=0)
        kwin = jnp.concatenate([kp_ref[:, kc], kc_ref[:, kc], kn_ref[:, kc]], axis=0)
        vwin = jnp.concatenate([vp_ref[:, kc], vc_ref[:, kc], vn_ref[:, kc]], axis=0)
        s_lat = lax.dot_general(q, kwin, (((1,), (1,)), ((), ())), preferred_element_type=F32) * scale
        s_ctx = lax.dot_general(q, ck_ref[0, kvh], (((1,), (1,)), ((), ())), preferred_element_type=F32) * scale
        s_lat = jnp.where(valid, s_lat, NEG_INIT)
        sk = jnp.zeros((A_GROUP * qb, 1), F32)
        for g in range(A_GROUP):
            sk = jnp.where(hrow == g, sink_ref[kvh * A_GROUP + g], sk)
        mx = jnp.maximum(jnp.maximum(jnp.max(s_lat, axis=1, keepdims=True),
                                     jnp.max(s_ctx, axis=1, keepdims=True)), sk)
        p_lat = jnp.exp(s_lat - mx)
        p_ctx = jnp.exp(s_ctx - mx)
        den = jnp.sum(p_lat, axis=1, keepdims=True) + jnp.sum(p_ctx, axis=1, keepdims=True) + jnp.exp(sk - mx)
        o = (jnp.dot(p_lat.astype(BF16), vwin, preferred_element_type=F32)
             + jnp.dot(p_ctx.astype(BF16), cv_ref[0, kvh], preferred_element_type=F32)) / den
        for g in range(A_GROUP):
            o_ref[:, q0 + g * A_HD:q0 + (g + 1) * A_HD] = o[g * qb:(g + 1) * qb].astype(BF16)


def _attn_lat(qr, kr, proj, ck, cv, sink, row_off, nseq, seq):
    qb = WINDOW
    nblk = seq // qb
    qw = A_HEADS * A_HD
    kw = A_KV_HEADS * A_HD
    rb = row_off // qb
    p_len = ck.shape[2]
    prev = lambda b, i: b * nblk + jnp.maximum(i - 1, 0)
    cur = lambda b, i: b * nblk + i
    nxt = lambda b, i: b * nblk + jnp.minimum(i + 1, nblk - 1)
    vcol = VA_OFF // kw
    return pl.pallas_call(
        functools.partial(_attn_lat_kernel, nblk=nblk),
        grid=(nseq, nblk),
        in_specs=[
            pl.BlockSpec(memory_space=pltpu.SMEM),
            pl.BlockSpec((qb, qw), lambda b, i: (cur(b, i), 0)),
            pl.BlockSpec((qb, kw), lambda b, i: (prev(b, i), 0)),
            pl.BlockSpec((qb, kw), lambda b, i: (cur(b, i), 0)),
            pl.BlockSpec((qb, kw), lambda b, i: (nxt(b, i), 0)),
            pl.BlockSpec((qb, kw), lambda b, i: (rb + prev(b, i), vcol)),
            pl.BlockSpec((qb, kw), lambda b, i: (rb + cur(b, i), vcol)),
            pl.BlockSpec((qb, kw), lambda b, i: (rb + nxt(b, i), vcol)),
            pl.BlockSpec((1, A_KV_HEADS, p_len, A_HD), lambda b, i: (b, 0, 0, 0)),
            pl.BlockSpec((1, A_KV_HEADS, p_len, A_HD), lambda b, i: (b, 0, 0, 0)),
        ],
        out_specs=pl.BlockSpec((qb, qw), lambda b, i: (cur(b, i), 0)),
        out_shape=jax.ShapeDtypeStruct((nseq * seq, qw), BF16),
        compiler_params=_cparams(("parallel", "parallel")),
        name="attn_lat",
    )(sink, qr, kr, kr, kr, proj, proj, proj, ck, cv)


def _outproj_kernel(*refs, alpha, na):
    tm = refs[0].shape[0]
    for s in range(tm // SUB_ROWS):
        _outproj_rows(slice(s * SUB_ROWS, (s + 1) * SUB_ROWS), *refs, alpha=alpha, na=na)


def _outproj_rows(rows, mmc_ref, mml_ref, mac_ref, mal_ref, w_ref, xa_ref, xb_ref, mod_ref, lng_ref, lnb_ref, wr_ref,
                  br_ref, x1_ref, xw_ref, bkt_ref, *, alpha, na):
    is_a = pl.program_id(0) < na
    d = xa_ref.shape[1]
    half = w_ref.shape[1] // 2
    mix_m = jnp.where(is_a, mmc_ref[rows, :], mml_ref[rows, :])
    mix_a = jnp.where(is_a, mac_ref[rows, :], mal_ref[rows, :])
    f = (jnp.dot(mix_m, w_ref[0, :half, :], preferred_element_type=F32)
         + jnp.dot(mix_a, w_ref[0, half:, :], preferred_element_type=F32))
    x = jnp.where(is_a, xa_ref[rows, :], xb_ref[rows, :])
    z = alpha * x + mod_ref[0, 2:3, :] * f
    x1 = _ln_plain(z) * lng_ref[...] + lnb_ref[...]
    x1_ref[rows, :] = x1
    h2 = _ln_plain(x1) * (1.0 + mod_ref[0, 4:5, :]) + mod_ref[0, 3:4, :]
    xw_ref[rows, :d] = h2
    logits = jnp.dot(h2.astype(BF16), wr_ref[...], preferred_element_type=F32)
    scores = jax.nn.sigmoid(logits)
    sel = scores + br_ref[...]
    sc_t = scores.T
    sel_t = sel.T
    sv = [sel_t[e:e + 1, :] for e in range(N_EXPERTS)]
    cv = [sc_t[e:e + 1, :] for e in range(N_EXPERTS)]
    gs = []
    for g in range(N_GROUPS):
        v = sv[4 * g:4 * g + 4]
        best = None
        for a, b in zip(PAIR_LO, PAIR_HI):
            ps = v[a] + v[b]
            best = ps if best is None else jnp.maximum(best, ps)
        gs.append(best)
    gmax = jnp.maximum(jnp.maximum(gs[0], gs[1]), jnp.maximum(gs[2], gs[3]))
    grp = jnp.full(gmax.shape, N_GROUPS - 1, jnp.int32)
    for g in range(N_GROUPS - 2, -1, -1):
        grp = jnp.where(gs[g] == gmax, g, grp)
    def pick(vals, k):
        out = vals[k]
        for g in range(1, N_GROUPS):
            out = jnp.where(grp == g, vals[4 * g + k], out)
        return out
    gv = [pick(sv, k) for k in range(EXPERTS_PER_GROUP)]
    gc = [pick(cv, k) for k in range(EXPERTS_PER_GROUP)]
    m1 = jnp.maximum(jnp.maximum(gv[0], gv[1]), jnp.maximum(gv[2], gv[3]))
    k0 = jnp.full(m1.shape, 3, jnp.int32)
    for k in range(2, -1, -1):
        k0 = jnp.where(gv[k] == m1, k, k0)
    gv2 = [jnp.where(k0 == k, -jnp.inf, gv[k]) for k in range(4)]
    m2 = jnp.maximum(jnp.maximum(gv2[0], gv2[1]), jnp.maximum(gv2[2], gv2[3]))
    k1 = jnp.full(m1.shape, 3, jnp.int32)
    for k in range(2, -1, -1):
        k1 = jnp.where(gv2[k] == m2, k, k1)
    lo = jnp.minimum(k0, k1)
    hi = jnp.maximum(k0, k1)
    pair = jnp.where(lo == 0, hi - 1, jnp.where(lo == 1, hi + 1, 5))
    bkt_ref[:, rows] = grp * 6 + pair
    s_lo = jnp.where(lo == 0, gc[0], jnp.where(lo == 1, gc[1], gc[2]))
    s_hi = jnp.where(hi == 1, gc[1], jnp.where(hi == 2, gc[2], gc[3]))
    tot = s_lo + s_hi
    wrow = lax.broadcasted_iota(jnp.int32, (LANES, tot.shape[1]), 0)
    w_t = jnp.where(wrow == 0, s_lo / tot, jnp.where(wrow == 1, s_hi / tot, 0.0))
    xw_ref[rows, d:] = w_t.T


def _outproj(mixes, w_out, layer, xs, t, modl, cond_of_tile, ln_g, ln_b, w_router, b_router, alpha, tm):
    x_a, x_b, na, off_b = xs
    d = x_a.shape[1]
    hw = mixes[0].shape[1]
    row = lambda i: (i, 0)
    const = lambda i: (0, 0)
    mix_a, mix_b = _split_specs(tm, hw, na, 0)
    x_sa, x_sb = _split_specs(tm, d, na, off_b)
    return pl.pallas_call(
        functools.partial(_outproj_kernel, alpha=alpha, na=na),
        grid=(t // tm,),
        in_specs=[
            mix_a, mix_b, mix_a, mix_b,
            pl.BlockSpec((1, 2 * hw, d), lambda i: (layer, 0, 0), pipeline_mode=pl.Buffered(1)),
            x_sa, x_sb,
            pl.BlockSpec((1, 6, d), lambda i: (cond_of_tile(i), 0, 0)),
            pl.BlockSpec((1, d), const),
            pl.BlockSpec((1, d), const),
            pl.BlockSpec((d, LANES), const),
            pl.BlockSpec((1, LANES), const),
        ],
        out_specs=[
            pl.BlockSpec((tm, d), row),
            pl.BlockSpec((tm, d + LANES), row),
            pl.BlockSpec((1, tm), lambda i: (0, i)),
        ],
        out_shape=[
            jax.ShapeDtypeStruct((t, d), F32),
            jax.ShapeDtypeStruct((t, d + LANES), F32),
            jax.ShapeDtypeStruct((1, t), jnp.int32),
        ],
        compiler_params=_cparams(("parallel",)),
        name="outproj_route",
    )(mixes[0], mixes[1], mixes[2], mixes[3], w_out, x_a, x_b, modl, ln_g, ln_b, w_router, b_router)


def _moe_kernel(te0_ref, te1_ref, tnv_ref, pg_ref, pgn_ref, ps_ref, x_hbm, wg0_ref, wg1_ref, wu0_ref, wu1_ref,
                wd0_ref, wd1_ref, y_hbm, xbuf, ybuf, gsem, ssem, *, tm_e, d, n_tiles, t):
    g = pl.program_id(0)
    nv = tnv_ref[g]
    slot = lax.rem(g, 2)
    oslot = 1 - slot

    def start_rows(copy_of_row, idx_ref, r0, r1):
        for r in range(r0, r1):
            copy_of_row(r, idx_ref[0, 0, r]).start(priority=r % 2)

    def start_gather(sl, idx_ref, r0=0, r1=tm_e):
        start_rows(lambda r, idx: pltpu.make_async_copy(x_hbm.at[pl.ds(idx, 1)], xbuf.at[sl, pl.ds(r, 1)],
                                                        gsem.at[sl]), idx_ref, r0, r1)

    def start_scatter(sl, idx_ref):
        start_rows(lambda r, idx: pltpu.make_async_copy(ybuf.at[sl, pl.ds(r, 1)], y_hbm.at[pl.ds(idx, 1)],
                                                        ssem.at[sl]), idx_ref, 0, tm_e)

    def wait_gather(sl):
        pltpu.make_async_copy(x_hbm.at[pl.ds(0, tm_e)], xbuf.at[sl], gsem.at[sl]).wait()

    def wait_scatter(sl):
        pltpu.make_async_copy(ybuf.at[sl], y_hbm.at[pl.ds(0, tm_e)], ssem.at[sl]).wait()

    @pl.when(g == 0)
    def _():
        ybuf[1] = jnp.zeros((tm_e, d), F32)
        for k in range(2):
            pltpu.make_async_copy(ybuf.at[1], y_hbm.at[pl.ds(t + k * tm_e, tm_e)], ssem.at[1]).start()
        for k in range(2):
            wait_scatter(1)
        start_gather(0, pg_ref)

    @pl.when(nv > 0)
    def _():
        g_next = jnp.minimum(g + 1, n_tiles - 1)
        has_next = jnp.logical_and(g + 1 < n_tiles, tnv_ref[g_next] > 0)

        wait_gather(slot)
        xb = xbuf[slot, :, :d].astype(BF16)
        q4 = tm_e // 4
        y = None
        for k, (wg_ref, wu_ref, wd_ref) in enumerate(((wg0_ref, wu0_ref, wd0_ref), (wg1_ref, wu1_ref, wd1_ref))):
            gt = jnp.dot(xb, wg_ref[0, 0], preferred_element_type=F32)
            start_gather(oslot, pgn_ref, (2 * k) * q4, (2 * k + 1) * q4)
            up = jnp.dot(xb, wu_ref[0, 0], preferred_element_type=F32)
            start_gather(oslot, pgn_ref, (2 * k + 1) * q4, (2 * k + 2) * q4)
            a = (gt * jax.nn.sigmoid(gt)) * up
            ye = jnp.dot(a.astype(BF16), wd_ref[0, 0], preferred_element_type=F32)
            ye = xbuf[slot, :, d + k:d + k + 1] * ye
            y = ye if y is None else y + ye
        ybuf[slot] = y
        start_scatter(slot, ps_ref)

        @pl.when(g > 0)
        def _():
            wait_scatter(oslot)

        @pl.when(jnp.logical_not(has_next))
        def _():
            wait_scatter(slot)
            wait_gather(oslot)


def _moe(xw, bkt, wg, wu, wd, layer, tm_e):
    t, dw = xw.shape
    d = dw - LANES
    d_exp = wd.shape[2]
    n_tiles = t // tm_e + N_BUCKETS
    n_rows = n_tiles * tm_e
    ids = jnp.arange(N_BUCKETS, dtype=jnp.int32)
    onehot = (bkt[:, None] == ids[None, :]).astype(jnp.int32)
    counts = jnp.sum(onehot, axis=0)
    tiles_b = (counts + tm_e - 1) // tm_e
    tile_end = jnp.cumsum(tiles_b)
    row_start = (tile_end - tiles_b) * tm_e
    blk = tm_e
    oh3 = onehot.astype(F32).reshape(t // blk, blk, N_BUCKETS)
    tri = (jnp.arange(blk)[:, None] >= jnp.arange(blk)[None, :]).astype(F32)
    local = jnp.einsum("ij,bjk->bik", tri, oh3)
    before = jnp.cumsum(local[:, -1, :], axis=0) - local[:, -1, :]
    running = (local + before[:, None, :]).astype(jnp.int32).reshape(t, N_BUCKETS)
    rank = jnp.take_along_axis(running, bkt[:, None], axis=1)[:, 0] - 1
    pos = row_start[bkt] + rank
    tile_ids = jnp.arange(n_tiles, dtype=jnp.int32)
    used = tile_end[-1]
    tb = jnp.sum((tile_end[None, :] <= jnp.minimum(tile_ids, used - 1)[:, None]).astype(jnp.int32), axis=1)
    tile_in_b = tile_ids - (tile_end - tiles_b)[tb]
    tnv = jnp.where(tile_ids < used, jnp.clip(counts[tb] - tile_in_b * tm_e, 0, tm_e), 0).astype(jnp.int32)
    lo = jnp.asarray(PAIR_LO, jnp.int32)
    hi = jnp.asarray(PAIR_HI, jnp.int32)
    te0 = (tb // 6) * EXPERTS_PER_GROUP + lo[tb % 6]
    te1 = (tb // 6) * EXPERTS_PER_GROUP + hi[tb % 6]
    tok = jnp.arange(t, dtype=jnp.int32)
    slot_row = jnp.arange(n_rows, dtype=jnp.int32)
    dump = t + ((slot_row // tm_e) % 2) * tm_e + slot_row % tm_e
    perm_s = dump.at[pos].set(tok)
    perm_g = jnp.where(perm_s < t, perm_s, 0)
    pg3 = perm_g.reshape(n_tiles, 1, tm_e)
    ps3 = perm_s.reshape(n_tiles, 1, tm_e)

    grid_spec = pltpu.PrefetchScalarGridSpec(
        num_scalar_prefetch=3,
        grid=(n_tiles,),
        in_specs=[
            pl.BlockSpec((1, 1, tm_e), lambda g, *_: (g, 0, 0), memory_space=pltpu.SMEM),
            pl.BlockSpec((1, 1, tm_e), lambda g, *_: (jnp.minimum(g + 1, n_tiles - 1), 0, 0),
                         memory_space=pltpu.SMEM),
            pl.BlockSpec((1, 1, tm_e), lambda g, *_: (g, 0, 0), memory_space=pltpu.SMEM),
            pl.BlockSpec(memory_space=pl.ANY),
            pl.BlockSpec((1, 1, d, d_exp), lambda g, e0, e1, tv: (layer, e0[g], 0, 0)),
            pl.BlockSpec((1, 1, d, d_exp), lambda g, e0, e1, tv: (layer, e1[g], 0, 0)),
            pl.BlockSpec((1, 1, d, d_exp), lambda g, e0, e1, tv: (layer, e0[g], 0, 0)),
            pl.BlockSpec((1, 1, d, d_exp), lambda g, e0, e1, tv: (layer, e1[g], 0, 0)),
            pl.BlockSpec((1, 1, d_exp, d), lambda g, e0, e1, tv: (layer, e0[g], 0, 0)),
            pl.BlockSpec((1, 1, d_exp, d), lambda g, e0, e1, tv: (layer, e1[g], 0, 0)),
        ],
        out_specs=pl.BlockSpec(memory_space=pl.ANY),
        scratch_shapes=[
            pltpu.VMEM((2, tm_e, dw), F32),
            pltpu.VMEM((2, tm_e, d), F32),
            pltpu.SemaphoreType.DMA((2,)),
            pltpu.SemaphoreType.DMA((2,)),
        ],
    )
    y = pl.pallas_call(
        functools.partial(_moe_kernel, tm_e=tm_e, d=d, n_tiles=n_tiles, t=t),
        grid_spec=grid_spec,
        out_shape=jax.ShapeDtypeStruct((t + 2 * tm_e, d), F32),
        compiler_params=_cparams(("arbitrary",)),
        name="moe_experts",
    )(te0, te1, tnv, pg3, pg3, ps3, xw, wg, wg, wu, wu, wd, wd)
    return y


def _final_kernel(x1_ref, y_ref, mod_ref, g_ref, b_ref, *o_refs, alpha, na):
    z = alpha * x1_ref[...] + mod_ref[0, 5:6, :] * y_ref[...]
    out = _ln_plain(z) * g_ref[...] + b_ref[...]
    if len(o_refs) == 1:
        o_refs[0][...] = out
    else:
        @pl.when(pl.program_id(0) < na)
        def _():
            o_refs[0][...] = out

        @pl.when(pl.program_id(0) >= na)
        def _():
            o_refs[1][...] = out


def _final_ln(x1, y, modl, cond_of_tile, ln_g, ln_b, alpha, tm, split_at=None):
    t, d = x1.shape
    row = lambda i: (i, 0)
    const = lambda i: (0, 0)
    if split_at is None:
        out_specs = pl.BlockSpec((tm, d), row)
        out_shape = jax.ShapeDtypeStruct((t, d), F32)
    else:
        out_specs = list(_split_specs(tm, d, split_at, 0))
        out_shape = [jax.ShapeDtypeStruct((split_at * tm, d), F32), jax.ShapeDtypeStruct((t - split_at * tm, d), F32)]
    return pl.pallas_call(
        functools.partial(_final_kernel, alpha=alpha, na=split_at),
        grid=(t // tm,),
        in_specs=[
            pl.BlockSpec((tm, d), row),
            pl.BlockSpec((tm, d), row),
            pl.BlockSpec((1, 6, d), lambda i: (cond_of_tile(i), 0, 0)),
            pl.BlockSpec((1, d), const),
            pl.BlockSpec((1, d), const),
        ],
        out_specs=out_specs,
        out_shape=out_shape,
        compiler_params=_cparams(("arbitrary",)),
        name="final_ln",
    )(x1, y, modl, ln_g, ln_b)


def _gate_columns():
    src_i = np.zeros((8 * N_PAIRS,), np.int32)
    src_f = np.zeros((8 * N_PAIRS,), np.int32)
    for hp in range(N_PAIRS):
        for dd in range(2):
            for hh in range(2):
                j = hp * 8 + dd * 2 + hh
                head = hp * 2 + hh
                src_i[j] = dd * M_HEADS + head
                src_f[j] = (2 + dd) * M_HEADS + head
    return src_i, src_f


def kernel(x_prompt, x_sample, cache_k, cache_v, state_C, state_n, state_m, c, c_ctx, w_ada, b_ada, w_in, b_gate,
           mh_norm_g, attn_sink, w_out, ln1_g, ln1_b, ln2_g, ln2_b, w_router, b_router, w_exp_gate, w_exp_up,
           w_exp_down):
    batch, seq, d = x_prompt.shape
    dec_batch, dec_seq, _ = x_sample.shape
    depth = w_in.shape[0]
    n_ctx = batch * seq
    n_lat = dec_batch * dec_seq
    t = n_ctx + n_lat
    alpha = (2 * depth) ** 0.25
    tm = 512
    tm_e = 256
    assert n_ctx % tm == 0 and dec_seq % tm == 0 and seq % CHUNK == 0 and dec_seq % CHUNK == 0
    assert n_ctx % dec_seq == 0 and n_ctx % WINDOW == 0 and t % tm_e == 0

    n_ctx_tiles = n_ctx // tm
    tiles_per_seq = dec_seq // tm

    def cond_of_tile(i):
        return jnp.where(i < n_ctx_tiles, 0, 1 + (i - n_ctx_tiles) // tiles_per_seq)

    n_cond = 1 + dec_batch
    cond_rows = -(-n_cond // 8) * 8
    cond = jnp.concatenate([c_ctx[None, :], c, jnp.zeros((cond_rows - n_cond, d), F32)], axis=0)
    mod = _ada_mod(cond, w_ada, b_ada).reshape(depth, cond_rows, 6, d)

    xs = (x_prompt.reshape(n_ctx, d), x_sample.reshape(n_lat, d), n_ctx_tiles, 0)

    src_i, src_f = _gate_columns()
    rope_tab = _rope_tables(dec_seq)
    wr = jnp.pad(w_router, ((0, 0), (0, LANES - N_EXPERTS))).astype(BF16)
    br = jnp.pad(b_router, (0, LANES - N_EXPERTS)).reshape(1, LANES)

    kscale = M_DK ** -0.5
    w_main = jnp.concatenate(
        [w_in[:, :, :KM_OFF], w_in[:, :, KM_OFF:VM_OFF] * kscale, w_in[:, :, VM_OFF:GATE_SRC_OFF],
         w_in[:, :, GATE_SRC_OFF + 4 * M_HEADS:]], axis=2).astype(BF16)
    w_out_b = w_out.astype(BF16)
    wg = w_exp_gate.astype(BF16)
    wu = w_exp_up.astype(BF16)
    wd = w_exp_down.astype(BF16)

    ks, vs, cs, ns, ms = [], [], [], [], []
    for l in range(depth):
        wgt = w_in[l][:, GATE_SRC_OFF:GATE_SRC_OFF + 4 * M_HEADS]
        zpad = jnp.zeros((d, LANES - 8 * N_PAIRS), F32)
        w_gate = jnp.concatenate([wgt[:, src_i], zpad, wgt[:, src_f], zpad], axis=1).astype(BF16)
        bpad = jnp.zeros((LANES - 8 * N_PAIRS,), F32)
        bg = jnp.concatenate([b_gate[l][src_i], bpad, b_gate[l][src_f], bpad]).reshape(1, 2 * LANES)
        modl = mod[l]

        proj, gates, kv = _inproj(xs, t, modl, cond_of_tile, w_main, l, w_gate, bg, tm)
        gb, gu, gut = _gate_prep(gates, 1024)

        gain = mh_norm_g[l].reshape(1, M_HEADS * M_DV)
        sink = attn_sink[l]
        hm_c, c_new, n_new, m_new = _mlstm(proj, gb, gu, gut, gain, 0, batch, seq, None, True)
        at_c = _attn_ctx(proj, sink, batch, seq, 4)
        c0 = state_C[:, l].reshape(dec_batch, 2, N_PAIRS, PAIR_ROWS, M_DV)
        n0 = state_n[:, l].reshape(dec_batch, 2, N_PAIRS, PAIR_ROWS, 1)
        m0 = state_m[:, l].reshape(dec_batch * 2 * M_HEADS)
        (hm_l,) = _mlstm(proj, gb, gu, gut, gain, n_ctx, dec_batch, dec_seq, (c0, n0, m0), False)
        qr, kr = _rope(proj, rope_tab, n_ctx, n_lat, dec_seq, 512)
        ck = jnp.transpose(cache_k[:, l], (0, 2, 1, 3)).astype(BF16)
        cv = jnp.transpose(cache_v[:, l], (0, 2, 1, 3)).astype(BF16)
        at_l = _attn_lat(qr, kr, proj, ck, cv, sink, n_ctx, dec_batch, dec_seq)

        x1, xw, bkt = _outproj((hm_c, hm_l, at_c, at_l), w_out_b, l, xs, t, modl, cond_of_tile,
                               ln1_g[l].reshape(1, d), ln1_b[l].reshape(1, d), wr, br, alpha, tm)
        y = _moe(xw, bkt[0], wg, wu, wd, l, tm_e)
        last = l == depth - 1
        x_new = _final_ln(x1, y, modl, cond_of_tile, ln2_g[l].reshape(1, d), ln2_b[l].reshape(1, d), alpha, tm,
                          split_at=n_ctx_tiles if last else None)
        if last:
            y_ctx, y_lat = x_new
        else:
            xs = (x_new, x_new, n_ctx_tiles, n_ctx_tiles)

        ks.append(kv[:n_ctx, :A_KV_HEADS * A_HD].reshape(batch, seq, A_KV_HEADS, A_HD))
        vs.append(kv[:n_ctx, A_KV_HEADS * A_HD:].reshape(batch, seq, A_KV_HEADS, A_HD))
        cs.append(c_new.reshape(batch, 2, M_HEADS, M_DK, M_DV))
        ns.append(n_new.reshape(batch, 2, M_HEADS, M_DK))
        m4 = m_new[:, :, :4, 0].reshape(batch, N_PAIRS, 2, 2)
        ms.append(jnp.transpose(m4, (0, 2, 1, 3)).reshape(batch, 2, M_HEADS))

    y_prompt = y_ctx.reshape(batch, seq, d)
    y_sample = y_lat.reshape(dec_batch, dec_seq, d)
    return (y_prompt, y_sample, jnp.stack(ks, 1), jnp.stack(vs, 1), jnp.stack(cs, 1), jnp.stack(ns, 1),
            jnp.stack(ms, 1))
```

```python
import functools

import jax
import jax.numpy as jnp
import numpy as np
from jax import lax
from jax.experimental import pallas as pl
from jax.experimental.pallas import tpu as pltpu

F32 = jnp.float32
BF16 = jnp.bfloat16

M_HEADS = 8
M_DK = 64
M_DV = 128
IGATE_CAP = 15.0
MH_EPS = 1e-6
NEG_INIT = -1e30
A_HEADS = 8
A_KV_HEADS = 2
A_GROUP = A_HEADS // A_KV_HEADS
A_HD = 128
WINDOW = 128
GRID_W = 64
ROPE_BASE = 10000.0
N_EXPERTS = 16
N_GROUPS = 4
EXPERTS_PER_GROUP = N_EXPERTS // N_GROUPS
LN_EPS = 1e-5

LANES = 128
VMEM_LIMIT = 56 * 1024 * 1024

QM_OFF = 0
KM_OFF = M_HEADS * M_DK
VM_OFF = KM_OFF + M_HEADS * M_DK
OM_OFF = VM_OFF + M_HEADS * M_DV
QA_OFF = OM_OFF + M_HEADS * M_DV
KA_OFF = QA_OFF + A_HEADS * A_HD
VA_OFF = KA_OFF + A_KV_HEADS * A_HD
MAIN_COLS = VA_OFF + A_KV_HEADS * A_HD
GATE_SRC_OFF = OM_OFF + M_HEADS * M_DV
N_PAIRS = M_HEADS // 2
PAIR_ROWS = 2 * M_DK
CHUNK = 128
SUB_ROWS = 256
N_BUCKETS = N_GROUPS * 6
PAIR_LO = (0, 0, 0, 1, 1, 2)
PAIR_HI = (1, 2, 3, 2, 3, 3)


def _cparams(sem):
    return pltpu.CompilerParams(dimension_semantics=sem, vmem_limit_bytes=VMEM_LIMIT)


def _ln_plain(x):
    mu = jnp.mean(x, axis=-1, keepdims=True)
    xc = x - mu
    return xc * lax.rsqrt(jnp.mean(xc * xc, axis=-1, keepdims=True) + LN_EPS)


def _ada_kernel(c_ref, w_ref, b_ref, o_ref):
    c = c_ref[...]
    s = (c * jax.nn.sigmoid(c)).astype(BF16)
    o_ref[0] = jnp.dot(s, w_ref[0].astype(BF16), preferred_element_type=F32) + b_ref[0]


def _ada_mod(cond, w_ada, b_ada):
    depth, d, n = w_ada.shape
    rows = cond.shape[0]
    tn = 2048
    return pl.pallas_call(
        _ada_kernel,
        grid=(depth, n // tn),
        in_specs=[
            pl.BlockSpec((rows, d), lambda l, j: (0, 0)),
            pl.BlockSpec((1, d, tn), lambda l, j: (l, 0, j)),
            pl.BlockSpec((1, 1, tn), lambda l, j: (l, 0, j)),
        ],
        out_specs=pl.BlockSpec((1, rows, tn), lambda l, j: (l, 0, j)),
        out_shape=jax.ShapeDtypeStruct((depth, rows, n), F32),
        compiler_params=_cparams(("parallel", "parallel")),
        name="ada_mod",
    )(cond, w_ada, b_ada.reshape(depth, 1, n))


def _split_specs(tm, width, na, off_b):
    spec_a = pl.BlockSpec((tm, width), lambda i, *_: (jnp.minimum(i, na - 1), 0))
    spec_b = pl.BlockSpec((tm, width), lambda i, *_: (jnp.maximum(i - na, 0) + off_b, 0))
    return spec_a, spec_b


def _inproj_kernel(xa_ref, xb_ref, mod_ref, w_ref, wg_ref, bg_ref, proj_ref, gates_ref, kv_ref, h_scr, *, tn, na):
    is_a = pl.program_id(0) < na
    nj = MAIN_COLS // tn
    for s in range(xa_ref.shape[0] // SUB_ROWS):
        rows = slice(s * SUB_ROWS, (s + 1) * SUB_ROWS)
        x = jnp.where(is_a, xa_ref[rows, :], xb_ref[rows, :])
        hn = _ln_plain(x)
        h = hn * (1.0 + mod_ref[0, 1:2, :]) + mod_ref[0, 0:1, :]
        hb = h.astype(BF16)
        h_scr[rows, :] = hb
        gates_ref[rows, :] = jnp.dot(hb, wg_ref[...], preferred_element_type=F32) + bg_ref[...]
        for j in range(nj):
            cols = slice(j * tn, (j + 1) * tn)
            acc = jnp.dot(h_scr[rows, :], w_ref[0, :, cols], preferred_element_type=F32)
            proj_ref[rows, cols] = acc.astype(BF16)
            if j == nj - 1:
                kv_ref[rows, :] = acc


def _inproj(xs, t, modl, cond_of_tile, w_main, layer, w_gate, b_gate, tm):
    x_a, x_b, na, off_b = xs
    d = x_a.shape[1]
    tn = MAIN_COLS - KA_OFF
    assert MAIN_COLS % tn == 0
    spec_a, spec_b = _split_specs(tm, d, na, off_b)
    const = lambda i: (0, 0)
    resident = pl.Buffered(1)
    return pl.pallas_call(
        functools.partial(_inproj_kernel, tn=tn, na=na),
        grid=(t // tm,),
        in_specs=[
            spec_a,
            spec_b,
            pl.BlockSpec((1, 6, d), lambda i: (cond_of_tile(i), 0, 0)),
            pl.BlockSpec((1, d, MAIN_COLS), lambda i: (layer, 0, 0), pipeline_mode=resident),
            pl.BlockSpec((d, 2 * LANES), const, pipeline_mode=resident),
            pl.BlockSpec((1, 2 * LANES), const, pipeline_mode=resident),
        ],
        out_specs=[
            pl.BlockSpec((tm, MAIN_COLS), lambda i: (i, 0)),
            pl.BlockSpec((tm, 2 * LANES), lambda i: (i, 0)),
            pl.BlockSpec((tm, tn), lambda i: (i, 0)),
        ],
        out_shape=[
            jax.ShapeDtypeStruct((t, MAIN_COLS), BF16),
            jax.ShapeDtypeStruct((t, 2 * LANES), F32),
            jax.ShapeDtypeStruct((t, tn), F32),
        ],
        scratch_shapes=[pltpu.VMEM((tm, d), BF16)],
        compiler_params=_cparams(("parallel",)),
        name="inproj",
    )(x_a, x_b, modl, w_main, w_gate, b_gate)


def _gate_kernel(g_ref, gb_ref, gu_ref, gut_ref, *, tg):
    gi = g_ref[:, :LANES]
    gf = g_ref[:, LANES:]
    ig = IGATE_CAP * jnp.tanh(gi / IGATE_CAP)
    lf = jax.nn.log_sigmoid(gf)
    lane = lax.broadcasted_iota(jnp.int32, (1, LANES), 1)
    is_fwd = (lane % 8) < 2
    s_i = lax.broadcasted_iota(jnp.int32, (CHUNK, CHUNK), 0)
    r_i = lax.broadcasted_iota(jnp.int32, (CHUNK, CHUNK), 1)
    tri_lo = (r_i <= s_i).astype(F32)
    tri_hi = (r_i >= s_i).astype(F32)
    for c in range(tg // CHUNK):
        rows = slice(c * CHUNK, (c + 1) * CHUNK)
        lfc = lf[rows]
        pre = jnp.dot(tri_lo, lfc, preferred_element_type=F32, precision=lax.Precision.HIGHEST)
        suf = jnp.dot(tri_hi, lfc, preferred_element_type=F32, precision=lax.Precision.HIGHEST)
        b = jnp.where(is_fwd, pre, suf)
        u = ig[rows] - b
        gb_ref[rows, :] = b
        gu_ref[rows, :] = u
        gut_ref[:, rows] = u.T[: 8 * N_PAIRS]


def _gate_prep(gates, tg):
    t = gates.shape[0]
    return pl.pallas_call(
        functools.partial(_gate_kernel, tg=tg),
        grid=(t // tg,),
        in_specs=[pl.BlockSpec((tg, 2 * LANES), lambda i: (i, 0))],
        out_specs=[
            pl.BlockSpec((tg, LANES), lambda i: (i, 0)),
            pl.BlockSpec((tg, LANES), lambda i: (i, 0)),
            pl.BlockSpec((8 * N_PAIRS, tg), lambda i: (0, i)),
        ],
        out_shape=[
            jax.ShapeDtypeStruct((t, LANES), F32),
            jax.ShapeDtypeStruct((t, LANES), F32),
            jax.ShapeDtypeStruct((8 * N_PAIRS, t), F32),
        ],
        compiler_params=_cparams(("parallel",)),
        name="gate_prep",
    )(gates)


def _mlstm_kernel(*refs, seq, has_state, emit_state):
    it = iter(refs)
    q_ref, k_ref, v_ref, om_ref, gb_ref, gu_ref, gut_ref, gain_ref = (next(it) for _ in range(8))
    if has_state:
        c0_ref, n0_ref, m0_ref = next(it), next(it), next(it)
    out_ref = next(it)
    if emit_state:
        cout_ref, nout_ref, mout_ref = next(it), next(it), next(it)
    h_scr, cst_scr = next(it), next(it)

    b_id = pl.program_id(0)
    hp = pl.program_id(1)
    nc = seq // CHUNK
    L = CHUNK

    lane = lax.broadcasted_iota(jnp.int32, (1, LANES), 1)
    row128 = lax.broadcasted_iota(jnp.int32, (PAIR_ROWS, 1), 0)
    s_i = lax.broadcasted_iota(jnp.int32, (L, L), 0)
    r_i = lax.broadcasted_iota(jnp.int32, (L, L), 1)
    e0row = (lane == 0).astype(F32)
    e0blk = jnp.broadcast_to(e0row, (L, LANES)).astype(BF16)

    def pick_lane(x, j):
        return jnp.sum(jnp.where(lane == j, x, 0.0), axis=1, keepdims=True)

    m_init = []
    for d in range(2):
        for hh in range(2):
            if has_state:
                rowmask = (row128 // M_DK) == hh
                cst_scr[d * 2 + hh, :, :LANES] = jnp.where(rowmask, c0_ref[0, d, 0], 0.0)
                cst_scr[d * 2 + hh, :, LANES:] = jnp.where(rowmask, n0_ref[0, d, 0], 0.0) * e0row
                m0 = m0_ref[b_id * (2 * M_HEADS) + d * M_HEADS + hp * 2 + hh]
                m_init.append(jnp.full((1, 1), m0, F32))
            else:
                cst_scr[d * 2 + hh] = jnp.zeros((PAIR_ROWS, 2 * LANES), F32)
                m_init.append(jnp.full((1, 1), NEG_INIT, F32))

    def body(t, ms):
        new_ms = []
        for d in range(2):
            tri = (r_i <= s_i) if d == 0 else (r_i >= s_i)
            c = t if d == 0 else nc - 1 - t
            r0 = pl.multiple_of(c * L, L)
            q2 = q_ref[pl.ds(r0, L), :]
            k2 = k_ref[pl.ds(r0, L), :]
            gbc = gb_ref[pl.ds(r0, L), :]
            guc = gu_ref[pl.ds(r0, L), :]
            gend = gb_ref[pl.ds(r0 + (L - 1 if d == 0 else 0), 1), :]
            for hh in range(2):
                si = d * 2 + hh
                jl = hp * 8 + d * 2 + hh
                lm = (lane // M_DK) == hh
                qh = jnp.where(lm, q2, jnp.zeros_like(q2))
                kh = jnp.where(lm, k2, jnp.zeros_like(k2))
                vh = v_ref[pl.ds(r0, L), hh * LANES:(hh + 1) * LANES]
                vext = jnp.concatenate([vh, e0blk], axis=1)
                urow = gut_ref[d * 2 + hh:d * 2 + hh + 1, pl.ds(r0, L)]
                ucol = pick_lane(guc, jl)
                bcol = pick_lane(gbc, jl)
                g = pick_lane(gend, jl)
                m = ms[si]
                umat = jnp.where(tri, urow, -jnp.inf)
                cmu = jnp.max(umat, axis=1, keepdims=True)
                mm = jnp.maximum(m, cmu)
                w = jnp.exp(umat - mm)
                sqk = lax.dot_general(qh, kh, (((1,), (1,)), ((), ())), preferred_element_type=F32)
                p = (sqk * w).astype(BF16)
                intra = jnp.dot(p, vext, preferred_element_type=F32)
                cs = cst_scr[si]
                inter = jnp.dot(qh, cs.astype(BF16), preferred_element_type=F32)
                nd = intra + jnp.exp(m - mm) * inter
                num = nd[:, :LANES]
                den = nd[:, LANES:LANES + 1]
                hv = num / jnp.maximum(jnp.abs(den), jnp.exp(-bcol - mm))
                h_scr[d, pl.ds(r0, L), hh * LANES:(hh + 1) * LANES] = hv
                maxu = jnp.max(urow, axis=1, keepdims=True)
                m_new = g + jnp.maximum(m, maxu)
                wa = jnp.exp(g + ucol - m_new)
                wc = jnp.exp(g + m - m_new)
                kw = (kh.astype(F32) * wa).astype(BF16)
                upd = lax.dot_general(kw, vext, (((0,), (0,)), ((), ())), preferred_element_type=F32)
                cst_scr[si] = wc * cs + upd
                new_ms.append(m_new)
        return tuple(new_ms)

    ms_fin = lax.fori_loop(0, nc, body, tuple(m_init))

    def combine(c, carry):
        r0 = pl.multiple_of(c * L, L)
        for hh in range(2):
            cols = slice(hh * LANES, (hh + 1) * LANES)
            tot = h_scr[0, pl.ds(r0, L), cols] + h_scr[1, pl.ds(r0, L), cols]
            ms2 = jnp.mean(tot * tot, axis=1, keepdims=True)
            y = tot * lax.rsqrt(ms2 + MH_EPS) * gain_ref[:, cols]
            y = y * jax.nn.sigmoid(om_ref[pl.ds(r0, L), cols].astype(F32))
            out_ref[pl.ds(r0, L), cols] = y.astype(BF16)
        return carry

    lax.fori_loop(0, nc, combine, 0)

    if emit_state:
        for d in range(2):
            cout_ref[0, d, 0] = cst_scr[d * 2, :, :LANES] + cst_scr[d * 2 + 1, :, :LANES]
            nout_ref[0, d, 0] = cst_scr[d * 2, :, LANES:LANES + 1] + cst_scr[d * 2 + 1, :, LANES:LANES + 1]
            for hh in range(2):
                mout_ref[0, 0, d * 2 + hh:d * 2 + hh + 1, :] = jnp.broadcast_to(ms_fin[d * 2 + hh], (1, LANES))
        mout_ref[0, 0, 4:8, :] = jnp.zeros((4, LANES), F32)


def _mlstm(proj, gb, gu, gut, gain, row_off, nseq, seq, state0, emit_state):
    assert row_off % seq == 0
    rb = row_off // seq
    has_state = state0 is not None
    kernel = functools.partial(_mlstm_kernel, seq=seq, has_state=has_state, emit_state=emit_state)
    in_specs = [
        pl.BlockSpec((seq, PAIR_ROWS), lambda b, h: (rb + b, QM_OFF // PAIR_ROWS + h)),
        pl.BlockSpec((seq, PAIR_ROWS), lambda b, h: (rb + b, KM_OFF // PAIR_ROWS + h)),
        pl.BlockSpec((seq, 2 * M_DV), lambda b, h: (rb + b, VM_OFF // (2 * M_DV) + h)),
        pl.BlockSpec((seq, 2 * M_DV), lambda b, h: (rb + b, OM_OFF // (2 * M_DV) + h)),
        pl.BlockSpec((seq, LANES), lambda b, h: (rb + b, 0)),
        pl.BlockSpec((seq, LANES), lambda b, h: (rb + b, 0)),
        pl.BlockSpec((8, seq), lambda b, h: (h, rb + b)),
        pl.BlockSpec((1, 2 * M_DV), lambda b, h: (0, h)),
    ]
    args = [proj, proj, proj, proj, gb, gu, gut, gain]
    if has_state:
        c0, n0, m0 = state0
        in_specs += [
            pl.BlockSpec((1, 2, 1, PAIR_ROWS, M_DV), lambda b, h: (b, 0, h, 0, 0)),
            pl.BlockSpec((1, 2, 1, PAIR_ROWS, 1), lambda b, h: (b, 0, h, 0, 0)),
            pl.BlockSpec(memory_space=pltpu.SMEM),
        ]
        args += [c0, n0, m0]
    out_specs = [pl.BlockSpec((seq, 2 * M_DV), lambda b, h: (b, h))]
    out_shape = [jax.ShapeDtypeStruct((nseq * seq, M_HEADS * M_DV), BF16)]
    if emit_state:
        out_specs += [
            pl.BlockSpec((1, 2, 1, PAIR_ROWS, M_DV), lambda b, h: (b, 0, h, 0, 0)),
            pl.BlockSpec((1, 2, 1, PAIR_ROWS, 1), lambda b, h: (b, 0, h, 0, 0)),
            pl.BlockSpec((1, 1, 8, LANES), lambda b, h: (b, h, 0, 0)),
        ]
        out_shape += [
            jax.ShapeDtypeStruct((nseq, 2, N_PAIRS, PAIR_ROWS, M_DV), F32),
            jax.ShapeDtypeStruct((nseq, 2, N_PAIRS, PAIR_ROWS, 1), F32),
            jax.ShapeDtypeStruct((nseq, N_PAIRS, 8, LANES), F32),
        ]
    return pl.pallas_call(
        kernel,
        grid=(nseq, N_PAIRS),
        in_specs=in_specs,
        out_specs=out_specs,
        out_shape=out_shape,
        scratch_shapes=[pltpu.VMEM((2, seq, 2 * M_DV), F32), pltpu.VMEM((4, PAIR_ROWS, 2 * LANES), F32)],
        compiler_params=_cparams(("parallel", "parallel")),
        name="mlstm",
    )(*args)


def _attn_ctx_kernel(sink_ref, q_ref, k_ref, v_ref, o_ref, *, seq, nb):
    kvh = pl.program_id(1)
    scale = A_HD ** -0.5
    for s in range(nb):
        rows = slice(s * seq, (s + 1) * seq)
        k = k_ref[rows, :]
        v = v_ref[rows, :]
        for g in range(A_GROUP):
            cols = slice(g * A_HD, (g + 1) * A_HD)
            q = q_ref[rows, cols]
            sc = lax.dot_general(q, k, (((1,), (1,)), ((), ())), preferred_element_type=F32) * scale
            sk = sink_ref[kvh * A_GROUP + g]
            mx = jnp.maximum(jnp.max(sc, axis=1, keepdims=True), sk)
            p = jnp.exp(sc - mx)
            den = jnp.sum(p, axis=1, keepdims=True) + jnp.exp(sk - mx)
            o = jnp.dot(p.astype(BF16), v, preferred_element_type=F32) / den
            o_ref[rows, cols] = o.astype(BF16)


def _attn_ctx(proj, sink, nseq, seq, nb):
    gw = A_GROUP * A_HD
    return pl.pallas_call(
        functools.partial(_attn_ctx_kernel, seq=seq, nb=nb),
        grid=(nseq // nb, A_KV_HEADS),
        in_specs=[
            pl.BlockSpec(memory_space=pltpu.SMEM),
            pl.BlockSpec((nb * seq, gw), lambda b, h: (b, QA_OFF // gw + h)),
            pl.BlockSpec((nb * seq, A_HD), lambda b, h: (b, KA_OFF // A_HD + h)),
            pl.BlockSpec((nb * seq, A_HD), lambda b, h: (b, VA_OFF // A_HD + h)),
        ],
        out_specs=pl.BlockSpec((nb * seq, gw), lambda b, h: (b, h)),
        out_shape=jax.ShapeDtypeStruct((nseq * seq, A_HEADS * A_HD), BF16),
        compiler_params=_cparams(("parallel", "parallel")),
        name="attn_ctx",
    )(sink, proj, proj, proj)


def _rope_kernel(q_ref, k_ref, cos_ref, sa_ref, sb_ref, qo_ref, ko_ref):
    cos = cos_ref[...]
    sa = sa_ref[...]
    sb = sb_ref[...]

    def rot(x):
        return x * cos + pltpu.roll(x, LANES - A_HD // 4, 1) * sa + pltpu.roll(x, A_HD // 4, 1) * sb

    for h in range(A_HEADS):
        cols = slice(h * A_HD, (h + 1) * A_HD)
        qo_ref[:, cols] = rot(q_ref[:, cols].astype(F32)).astype(BF16)
    for h in range(A_KV_HEADS):
        cols = slice(h * A_HD, (h + 1) * A_HD)
        ko_ref[:, cols] = rot(k_ref[:, cols].astype(F32)).astype(BF16)


def _rope(proj, tables, row_off, nrows, seq, tr):
    cos, sa, sb = tables
    rb = row_off // tr
    nps = seq // tr
    qw = A_HEADS * A_HD
    kw = A_KV_HEADS * A_HD
    tab = pl.BlockSpec((tr, A_HD), lambda i: (i % nps, 0))
    return pl.pallas_call(
        _rope_kernel,
        grid=(nrows // tr,),
        in_specs=[
            pl.BlockSpec((tr, qw), lambda i: (rb + i, QA_OFF // qw)),
            pl.BlockSpec((tr, kw), lambda i: (rb + i, KA_OFF // kw)),
            tab, tab, tab,
        ],
        out_specs=[pl.BlockSpec((tr, qw), lambda i: (i, 0)), pl.BlockSpec((tr, kw), lambda i: (i, 0))],
        out_shape=[jax.ShapeDtypeStruct((nrows, qw), BF16), jax.ShapeDtypeStruct((nrows, kw), BF16)],
        compiler_params=_cparams(("parallel",)),
        name="rope",
    )(proj, proj, cos, sa, sb)


def _rope_tables(seq):
    half = A_HD // 2
    pos = np.arange(seq)
    row = (pos // GRID_W).astype(np.float32)
    col = (pos % GRID_W).astype(np.float32)
    inv = (ROPE_BASE ** (-np.arange(0, half, 2, dtype=np.float32) / half)).astype(np.float32)
    ang_r = row[:, None] * inv[None, :]
    ang_c = col[:, None] * inv[None, :]
    ang = np.concatenate([ang_r, ang_r, ang_c, ang_c], axis=1).astype(np.float32)
    cos = np.cos(ang).astype(np.float32)
    sin = np.sin(ang).astype(np.float32)
    first = (np.arange(A_HD) % half) < (half // 2)
    sa = np.where(first[None, :], -sin, 0.0).astype(np.float32)
    sb = np.where(first[None, :], 0.0, sin).astype(np.float32)
    return jnp.asarray(cos), jnp.asarray(sa), jnp.asarray(sb)


def _attn_lat_kernel(sink_ref, q_ref, kp_ref, kc_ref, kn_ref, vp_ref, vc_ref, vn_ref, ck_ref, cv_ref, o_ref,
                     *, nblk):
    i = pl.program_id(1)
    qb = WINDOW
    scale = A_HD ** -0.5
    r = lax.broadcasted_iota(jnp.int32, (A_GROUP * qb, 3 * qb), 0) % qb
    c = lax.broadcasted_iota(jnp.int32, (A_GROUP * qb, 3 * qb), 1)
    c_lo = jnp.where(i > 0, 0, qb)
    c_hi = jnp.where(i < nblk - 1, 3 * qb, 2 * qb)
    valid = (c >= r) & (c <= r + 2 * WINDOW) & (c >= c_lo) & (c < c_hi)
    hrow = lax.broadcasted_iota(jnp.int32, (A_GROUP * qb, 1), 0) // qb
    for kvh in range(A_KV_HEADS):
        kc = slice(kvh * A_HD, (kvh + 1) * A_HD)
        q0 = kvh * A_GROUP * A_HD
        q = jnp.concatenate([q_ref[:, q0 + g * A_HD:q0 + (g + 1) * A_HD] for g in range(A_GROUP)], axis=0)
        kwin = jnp.concatenate([kp_ref[:, kc], kc_ref[:, kc], kn_ref[:, kc]], axis=0)
        vwin = jnp.concatenate([vp_ref[:, kc], vc_ref[:, kc], vn_ref[:, kc]], axis=0)
        s_lat = lax.dot_general(q, kwin, (((1,), (1,)), ((), ())), preferred_element_type=F32) * scale
        s_ctx = lax.dot_general(q, ck_ref[0, kvh], (((1,), (1,)), ((), ())), preferred_element_type=F32) * scale
        s_lat = jnp.where(valid, s_lat, NEG_INIT)
        sk = jnp.zeros((A_GROUP * qb, 1), F32)
        for g in range(A_GROUP):
            sk = jnp.where(hrow == g, sink_ref[kvh * A_GROUP + g], sk)
        mx = jnp.maximum(jnp.maximum(jnp.max(s_lat, axis=1, keepdims=True),
                                     jnp.max(s_ctx, axis=1, keepdims=True)), sk)
        p_lat = jnp.exp(s_lat - mx)
        p_ctx = jnp.exp(s_ctx - mx)
        den = jnp.sum(p_lat, axis=1, keepdims=True) + jnp.sum(p_ctx, axis=1, keepdims=True) + jnp.exp(sk - mx)
        o = (jnp.dot(p_lat.astype(BF16), vwin, preferred_element_type=F32)
             + jnp.dot(p_ctx.astype(BF16), cv_ref[0, kvh], preferred_element_type=F32)) / den
        for g in range(A_GROUP):
            o_ref[:, q0 + g * A_HD:q0 + (g + 1) * A_HD] = o[g * qb:(g + 1) * qb].astype(BF16)


def _attn_lat(qr, kr, proj, ck, cv, sink, row_off, nseq, seq):
    qb = WINDOW
    nblk = seq // qb
    qw = A_HEADS * A_HD
    kw = A_KV_HEADS * A_HD
    rb = row_off // qb
    p_len = ck.shape[2]
    prev = lambda b, i: b * nblk + jnp.maximum(i - 1, 0)
    cur = lambda b, i: b * nblk + i
    nxt = lambda b, i: b * nblk + jnp.minimum(i + 1, nblk - 1)
    vcol = VA_OFF // kw
    return pl.pallas_call(
        functools.partial(_attn_lat_kernel, nblk=nblk),
        grid=(nseq, nblk),
        in_specs=[
            pl.BlockSpec(memory_space=pltpu.SMEM),
            pl.BlockSpec((qb, qw), lambda b, i: (cur(b, i), 0)),
            pl.BlockSpec((qb, kw), lambda b, i: (prev(b, i), 0)),
            pl.BlockSpec((qb, kw), lambda b, i: (cur(b, i), 0)),
            pl.BlockSpec((qb, kw), lambda b, i: (nxt(b, i), 0)),
            pl.BlockSpec((qb, kw), lambda b, i: (rb + prev(b, i), vcol)),
            pl.BlockSpec((qb, kw), lambda b, i: (rb + cur(b, i), vcol)),
            pl.BlockSpec((qb, kw), lambda b, i: (rb + nxt(b, i), vcol)),
            pl.BlockSpec((1, A_KV_HEADS, p_len, A_HD), lambda b, i: (b, 0, 0, 0)),
            pl.BlockSpec((1, A_KV_HEADS, p_len, A_HD), lambda b, i: (b, 0, 0, 0)),
        ],
        out_specs=pl.BlockSpec((qb, qw), lambda b, i: (cur(b, i), 0)),
        out_shape=jax.ShapeDtypeStruct((nseq * seq, qw), BF16),
        compiler_params=_cparams(("parallel", "parallel")),
        name="attn_lat",
    )(sink, qr, kr, kr, kr, proj, proj, proj, ck, cv)


def _outproj_kernel(*refs, alpha, na):
    tm = refs[0].shape[0]
    for s in range(tm // SUB_ROWS):
        _outproj_rows(slice(s * SUB_ROWS, (s + 1) * SUB_ROWS), *refs, alpha=alpha, na=na)


def _outproj_rows(rows, mmc_ref, mml_ref, mac_ref, mal_ref, w_ref, xa_ref, xb_ref, mod_ref, lng_ref, lnb_ref, wr_ref,
                  br_ref, x1_ref, xw_ref, bkt_ref, *, alpha, na):
    is_a = pl.program_id(0) < na
    d = xa_ref.shape[1]
    half = w_ref.shape[1] // 2
    mix_m = jnp.where(is_a, mmc_ref[rows, :], mml_ref[rows, :])
    mix_a = jnp.where(is_a, mac_ref[rows, :], mal_ref[rows, :])
    f = (jnp.dot(mix_m, w_ref[0, :half, :], preferred_element_type=F32)
         + jnp.dot(mix_a, w_ref[0, half:, :], preferred_element_type=F32))
    x = jnp.where(is_a, xa_ref[rows, :], xb_ref[rows, :])
    z = alpha * x + mod_ref[0, 2:3, :] * f
    x1 = _ln_plain(z) * lng_ref[...] + lnb_ref[...]
    x1_ref[rows, :] = x1
    h2 = _ln_plain(x1) * (1.0 + mod_ref[0, 4:5, :]) + mod_ref[0, 3:4, :]
    xw_ref[rows, :d] = h2
    logits = jnp.dot(h2.astype(BF16), wr_ref[...], preferred_element_type=F32)
    scores = jax.nn.sigmoid(logits)
    sel = scores + br_ref[...]
    sc_t = scores.T
    sel_t = sel.T
    sv = [sel_t[e:e + 1, :] for e in range(N_EXPERTS)]
    cv = [sc_t[e:e + 1, :] for e in range(N_EXPERTS)]
    gs = []
    for g in range(N_GROUPS):
        v = sv[4 * g:4 * g + 4]
        best = None
        for a, b in zip(PAIR_LO, PAIR_HI):
            ps = v[a] + v[b]
            best = ps if best is None else jnp.maximum(best, ps)
        gs.append(best)
    gmax = jnp.maximum(jnp.maximum(gs[0], gs[1]), jnp.maximum(gs[2], gs[3]))
    grp = jnp.full(gmax.shape, N_GROUPS - 1, jnp.int32)
    for g in range(N_GROUPS - 2, -1, -1):
        grp = jnp.where(gs[g] == gmax, g, grp)
    def pick(vals, k):
        out = vals[k]
        for g in range(1, N_GROUPS):
            out = jnp.where(grp == g, vals[4 * g + k], out)
        return out
    gv = [pick(sv, k) for k in range(EXPERTS_PER_GROUP)]
    gc = [pick(cv, k) for k in range(EXPERTS_PER_GROUP)]
    m1 = jnp.maximum(jnp.maximum(gv[0], gv[1]), jnp.maximum(gv[2], gv[3]))
    k0 = jnp.full(m1.shape, 3, jnp.int32)
    for k in range(2, -1, -1):
        k0 = jnp.where(gv[k] == m1, k, k0)
    gv2 = [jnp.where(k0 == k, -jnp.inf, gv[k]) for k in range(4)]
    m2 = jnp.maximum(jnp.maximum(gv2[0], gv2[1]), jnp.maximum(gv2[2], gv2[3]))
    k1 = jnp.full(m1.shape, 3, jnp.int32)
    for k in range(2, -1, -1):
        k1 = jnp.where(gv2[k] == m2, k, k1)
    lo = jnp.minimum(k0, k1)
    hi = jnp.maximum(k0, k1)
    pair = jnp.where(lo == 0, hi - 1, jnp.where(lo == 1, hi + 1, 5))
    bkt_ref[:, rows] = grp * 6 + pair
    s_lo = jnp.where(lo == 0, gc[0], jnp.where(lo == 1, gc[1], gc[2]))
    s_hi = jnp.where(hi == 1, gc[1], jnp.where(hi == 2, gc[2], gc[3]))
    tot = s_lo + s_hi
    wrow = lax.broadcasted_iota(jnp.int32, (LANES, tot.shape[1]), 0)
    w_t = jnp.where(wrow == 0, s_lo / tot, jnp.where(wrow == 1, s_hi / tot, 0.0))
    xw_ref[rows, d:] = w_t.T


def _outproj(mixes, w_out, layer, xs, t, modl, cond_of_tile, ln_g, ln_b, w_router, b_router, alpha, tm):
    x_a, x_b, na, off_b = xs
    d = x_a.shape[1]
    hw = mixes[0].shape[1]
    row = lambda i: (i, 0)
    const = lambda i: (0, 0)
    mix_a, mix_b = _split_specs(tm, hw, na, 0)
    x_sa, x_sb = _split_specs(tm, d, na, off_b)
    return pl.pallas_call(
        functools.partial(_outproj_kernel, alpha=alpha, na=na),
        grid=(t // tm,),
        in_specs=[
            mix_a, mix_b, mix_a, mix_b,
            pl.BlockSpec((1, 2 * hw, d), lambda i: (layer, 0, 0), pipeline_mode=pl.Buffered(1)),
            x_sa, x_sb,
            pl.BlockSpec((1, 6, d), lambda i: (cond_of_tile(i), 0, 0)),
            pl.BlockSpec((1, d), const),
            pl.BlockSpec((1, d), const),
            pl.BlockSpec((d, LANES), const),
            pl.BlockSpec((1, LANES), const),
        ],
        out_specs=[
            pl.BlockSpec((tm, d), row),
            pl.BlockSpec((tm, d + LANES), row),
            pl.BlockSpec((1, tm), lambda i: (0, i)),
        ],
        out_shape=[
            jax.ShapeDtypeStruct((t, d), F32),
            jax.ShapeDtypeStruct((t, d + LANES), F32),
            jax.ShapeDtypeStruct((1, t), jnp.int32),
        ],
        compiler_params=_cparams(("parallel",)),
        name="outproj_route",
    )(mixes[0], mixes[1], mixes[2], mixes[3], w_out, x_a, x_b, modl, ln_g, ln_b, w_router, b_router)


def _moe_kernel(te0_ref, te1_ref, tnv_ref, pg_ref, pgn_ref, ps_ref, x_hbm, wg0_ref, wg1_ref, wu0_ref, wu1_ref,
                wd0_ref, wd1_ref, y_hbm, xbuf, ybuf, gsem, ssem, *, tm_e, d, n_tiles, t):
    g = pl.program_id(0)
    nv = tnv_ref[g]
    slot = lax.rem(g, 2)
    oslot = 1 - slot

    def start_rows(copy_of_row, idx_ref, r0, r1):
        for r in range(r0, r1):
            copy_of_row(r, idx_ref[0, 0, r]).start(priority=r % 2)

    def start_gather(sl, idx_ref, r0=0, r1=tm_e):
        start_rows(lambda r, idx: pltpu.make_async_copy(x_hbm.at[pl.ds(idx, 1)], xbuf.at[sl, pl.ds(r, 1)],
                                                        gsem.at[sl]), idx_ref, r0, r1)

    def start_scatter(sl, idx_ref):
        start_rows(lambda r, idx: pltpu.make_async_copy(ybuf.at[sl, pl.ds(r, 1)], y_hbm.at[pl.ds(idx, 1)],
                                                        ssem.at[sl]), idx_ref, 0, tm_e)

    def wait_gather(sl):
        pltpu.make_async_copy(x_hbm.at[pl.ds(0, tm_e)], xbuf.at[sl], gsem.at[sl]).wait()

    def wait_scatter(sl):
        pltpu.make_async_copy(ybuf.at[sl], y_hbm.at[pl.ds(0, tm_e)], ssem.at[sl]).wait()

    @pl.when(g == 0)
    def _():
        ybuf[1] = jnp.zeros((tm_e, d), F32)
        for k in range(2):
            pltpu.make_async_copy(ybuf.at[1], y_hbm.at[pl.ds(t + k * tm_e, tm_e)], ssem.at[1]).start()
        for k in range(2):
            wait_scatter(1)
        start_gather(0, pg_ref)

    @pl.when(nv > 0)
    def _():
        g_next = jnp.minimum(g + 1, n_tiles - 1)
        has_next = jnp.logical_and(g + 1 < n_tiles, tnv_ref[g_next] > 0)

        wait_gather(slot)
        start_gather(oslot, pgn_ref)
        xb = xbuf[slot, :, :d].astype(BF16)
        y = None
        for k, (wg_ref, wu_ref, wd_ref) in enumerate(((wg0_ref, wu0_ref, wd0_ref), (wg1_ref, wu1_ref, wd1_ref))):
            gt = jnp.dot(xb, wg_ref[0, 0], preferred_element_type=F32)
            up = jnp.dot(xb, wu_ref[0, 0], preferred_element_type=F32)
            a = (gt * jax.nn.sigmoid(gt)) * up
            ye = jnp.dot(a.astype(BF16), wd_ref[0, 0], preferred_element_type=F32)
            ye = xbuf[slot, :, d + k:d + k + 1] * ye
            y = ye if y is None else y + ye
        ybuf[slot] = y
        start_scatter(slot, ps_ref)

        @pl.when(g > 0)
        def _():
            wait_scatter(oslot)

        @pl.when(jnp.logical_not(has_next))
        def _():
            wait_scatter(slot)
            wait_gather(oslot)


def _moe(xw, bkt, wg, wu, wd, layer, tm_e):
    t, dw = xw.shape
    d = dw - LANES
    d_exp = wd.shape[2]
    n_tiles = t // tm_e + N_BUCKETS
    n_rows = n_tiles * tm_e
    ids = jnp.arange(N_BUCKETS, dtype=jnp.int32)
    onehot = (bkt[:, None] == ids[None, :]).astype(jnp.int32)
    counts = jnp.sum(onehot, axis=0)
    tiles_b = (counts + tm_e - 1) // tm_e
    tile_end = jnp.cumsum(tiles_b)
    row_start = (tile_end - tiles_b) * tm_e
    blk = tm_e
    oh3 = onehot.astype(F32).reshape(t // blk, blk, N_BUCKETS)
    tri = (jnp.arange(blk)[:, None] >= jnp.arange(blk)[None, :]).astype(F32)
    local = jnp.einsum("ij,bjk->bik", tri, oh3)
    before = jnp.cumsum(local[:, -1, :], axis=0) - local[:, -1, :]
    running = (local + before[:, None, :]).astype(jnp.int32).reshape(t, N_BUCKETS)
    rank = jnp.take_along_axis(running, bkt[:, None], axis=1)[:, 0] - 1
    pos = row_start[bkt] + rank
    tile_ids = jnp.arange(n_tiles, dtype=jnp.int32)
    used = tile_end[-1]
    tb = jnp.sum((tile_end[None, :] <= jnp.minimum(tile_ids, used - 1)[:, None]).astype(jnp.int32), axis=1)
    tile_in_b = tile_ids - (tile_end - tiles_b)[tb]
    tnv = jnp.where(tile_ids < used, jnp.clip(counts[tb] - tile_in_b * tm_e, 0, tm_e), 0).astype(jnp.int32)
    lo = jnp.asarray(PAIR_LO, jnp.int32)
    hi = jnp.asarray(PAIR_HI, jnp.int32)
    te0 = (tb // 6) * EXPERTS_PER_GROUP + lo[tb % 6]
    te1 = (tb // 6) * EXPERTS_PER_GROUP + hi[tb % 6]
    tok = jnp.arange(t, dtype=jnp.int32)
    slot_row = jnp.arange(n_rows, dtype=jnp.int32)
    dump = t + ((slot_row // tm_e) % 2) * tm_e + slot_row % tm_e
    perm_s = dump.at[pos].set(tok)
    perm_g = jnp.where(perm_s < t, perm_s, 0)
    pg3 = perm_g.reshape(n_tiles, 1, tm_e)
    ps3 = perm_s.reshape(n_tiles, 1, tm_e)

    grid_spec = pltpu.PrefetchScalarGridSpec(
        num_scalar_prefetch=3,
        grid=(n_tiles,),
        in_specs=[
            pl.BlockSpec((1, 1, tm_e), lambda g, *_: (g, 0, 0), memory_space=pltpu.SMEM),
            pl.BlockSpec((1, 1, tm_e), lambda g, *_: (jnp.minimum(g + 1, n_tiles - 1), 0, 0),
                         memory_space=pltpu.SMEM),
            pl.BlockSpec((1, 1, tm_e), lambda g, *_: (g, 0, 0), memory_space=pltpu.SMEM),
            pl.BlockSpec(memory_space=pl.ANY),
            pl.BlockSpec((1, 1, d, d_exp), lambda g, e0, e1, tv: (layer, e0[g], 0, 0)),
            pl.BlockSpec((1, 1, d, d_exp), lambda g, e0, e1, tv: (layer, e1[g], 0, 0)),
            pl.BlockSpec((1, 1, d, d_exp), lambda g, e0, e1, tv: (layer, e0[g], 0, 0)),
            pl.BlockSpec((1, 1, d, d_exp), lambda g, e0, e1, tv: (layer, e1[g], 0, 0)),
            pl.BlockSpec((1, 1, d_exp, d), lambda g, e0, e1, tv: (layer, e0[g], 0, 0)),
            pl.BlockSpec((1, 1, d_exp, d), lambda g, e0, e1, tv: (layer, e1[g], 0, 0)),
        ],
        out_specs=pl.BlockSpec(memory_space=pl.ANY),
        scratch_shapes=[
            pltpu.VMEM((2, tm_e, dw), F32),
            pltpu.VMEM((2, tm_e, d), F32),
            pltpu.SemaphoreType.DMA((2,)),
            pltpu.SemaphoreType.DMA((2,)),
        ],
    )
    y = pl.pallas_call(
        functools.partial(_moe_kernel, tm_e=tm_e, d=d, n_tiles=n_tiles, t=t),
        grid_spec=grid_spec,
        out_shape=jax.ShapeDtypeStruct((t + 2 * tm_e, d), F32),
        compiler_params=_cparams(("arbitrary",)),
        name="moe_experts",
    )(te0, te1, tnv, pg3, pg3, ps3, xw, wg, wg, wu, wu, wd, wd)
    return y


def _final_kernel(x1_ref, y_ref, mod_ref, g_ref, b_ref, *o_refs, alpha, na):
    z = alpha * x1_ref[...] + mod_ref[0, 5:6, :] * y_ref[...]
    out = _ln_plain(z) * g_ref[...] + b_ref[...]
    if len(o_refs) == 1:
        o_refs[0][...] = out
    else:
        @pl.when(pl.program_id(0) < na)
        def _():
            o_refs[0][...] = out

        @pl.when(pl.program_id(0) >= na)
        def _():
            o_refs[1][...] = out


def _final_ln(x1, y, modl, cond_of_tile, ln_g, ln_b, alpha, tm, split_at=None):
    t, d = x1.shape
    row = lambda i: (i, 0)
    const = lambda i: (0, 0)
    if split_at is None:
        out_specs = pl.BlockSpec((tm, d), row)
        out_shape = jax.ShapeDtypeStruct((t, d), F32)
    else:
        out_specs = list(_split_specs(tm, d, split_at, 0))
        out_shape = [jax.ShapeDtypeStruct((split_at * tm, d), F32), jax.ShapeDtypeStruct((t - split_at * tm, d), F32)]
    return pl.pallas_call(
        functools.partial(_final_kernel, alpha=alpha, na=split_at),
        grid=(t // tm,),
        in_specs=[
            pl.BlockSpec((tm, d), row),
            pl.BlockSpec((tm, d), row),
            pl.BlockSpec((1, 6, d), lambda i: (cond_of_tile(i), 0, 0)),
            pl.BlockSpec((1, d), const),
            pl.BlockSpec((1, d), const),
        ],
        out_specs=out_specs,
        out_shape=out_shape,
        compiler_params=_cparams(("arbitrary",)),
        name="final_ln",
    )(x1, y, modl, ln_g, ln_b)


def _gate_columns():
    src_i = np.zeros((8 * N_PAIRS,), np.int32)
    src_f = np.zeros((8 * N_PAIRS,), np.int32)
    for hp in range(N_PAIRS):
        for dd in range(2):
            for hh in range(2):
                j = hp * 8 + dd * 2 + hh
                head = hp * 2 + hh
                src_i[j] = dd * M_HEADS + head
                src_f[j] = (2 + dd) * M_HEADS + head
    return src_i, src_f


def kernel(x_prompt, x_sample, cache_k, cache_v, state_C, state_n, state_m, c, c_ctx, w_ada, b_ada, w_in, b_gate,
           mh_norm_g, attn_sink, w_out, ln1_g, ln1_b, ln2_g, ln2_b, w_router, b_router, w_exp_gate, w_exp_up,
           w_exp_down):
    batch, seq, d = x_prompt.shape
    dec_batch, dec_seq, _ = x_sample.shape
    depth = w_in.shape[0]
    n_ctx = batch * seq
    n_lat = dec_batch * dec_seq
    t = n_ctx + n_lat
    alpha = (2 * depth) ** 0.25
    tm = 512
    tm_e = 256
    assert n_ctx % tm == 0 and dec_seq % tm == 0 and seq % CHUNK == 0 and dec_seq % CHUNK == 0
    assert n_ctx % dec_seq == 0 and n_ctx % WINDOW == 0 and t % tm_e == 0

    n_ctx_tiles = n_ctx // tm
    tiles_per_seq = dec_seq // tm

    def cond_of_tile(i):
        return jnp.where(i < n_ctx_tiles, 0, 1 + (i - n_ctx_tiles) // tiles_per_seq)

    n_cond = 1 + dec_batch
    cond_rows = -(-n_cond // 8) * 8
    cond = jnp.concatenate([c_ctx[None, :], c, jnp.zeros((cond_rows - n_cond, d), F32)], axis=0)
    mod = _ada_mod(cond, w_ada, b_ada).reshape(depth, cond_rows, 6, d)

    xs = (x_prompt.reshape(n_ctx, d), x_sample.reshape(n_lat, d), n_ctx_tiles, 0)

    src_i, src_f = _gate_columns()
    rope_tab = _rope_tables(dec_seq)
    wr = jnp.pad(w_router, ((0, 0), (0, LANES - N_EXPERTS))).astype(BF16)
    br = jnp.pad(b_router, (0, LANES - N_EXPERTS)).reshape(1, LANES)

    kscale = M_DK ** -0.5
    w_main = jnp.concatenate(
        [w_in[:, :, :KM_OFF], w_in[:, :, KM_OFF:VM_OFF] * kscale, w_in[:, :, VM_OFF:GATE_SRC_OFF],
         w_in[:, :, GATE_SRC_OFF + 4 * M_HEADS:]], axis=2).astype(BF16)
    w_out_b = w_out.astype(BF16)
    wg = w_exp_gate.astype(BF16)
    wu = w_exp_up.astype(BF16)
    wd = w_exp_down.astype(BF16)

    ks, vs, cs, ns, ms = [], [], [], [], []
    for l in range(depth):
        wgt = w_in[l][:, GATE_SRC_OFF:GATE_SRC_OFF + 4 * M_HEADS]
        zpad = jnp.zeros((d, LANES - 8 * N_PAIRS), F32)
        w_gate = jnp.concatenate([wgt[:, src_i], zpad, wgt[:, src_f], zpad], axis=1).astype(BF16)
        bpad = jnp.zeros((LANES - 8 * N_PAIRS,), F32)
        bg = jnp.concatenate([b_gate[l][src_i], bpad, b_gate[l][src_f], bpad]).reshape(1, 2 * LANES)
        modl = mod[l]

        proj, gates, kv = _inproj(xs, t, modl, cond_of_tile, w_main, l, w_gate, bg, tm)
        gb, gu, gut = _gate_prep(gates, 1024)

        gain = mh_norm_g[l].reshape(1, M_HEADS * M_DV)
        sink = attn_sink[l]
        hm_c, c_new, n_new, m_new = _mlstm(proj, gb, gu, gut, gain, 0, batch, seq, None, True)
        at_c = _attn_ctx(proj, sink, batch, seq, 4)
        c0 = state_C[:, l].reshape(dec_batch, 2, N_PAIRS, PAIR_ROWS, M_DV)
        n0 = state_n[:, l].reshape(dec_batch, 2, N_PAIRS, PAIR_ROWS, 1)
        m0 = state_m[:, l].reshape(dec_batch * 2 * M_HEADS)
        (hm_l,) = _mlstm(proj, gb, gu, gut, gain, n_ctx, dec_batch, dec_seq, (c0, n0, m0), False)
        qr, kr = _rope(proj, rope_tab, n_ctx, n_lat, dec_seq, 512)
        ck = jnp.transpose(cache_k[:, l], (0, 2, 1, 3)).astype(BF16)
        cv = jnp.transpose(cache_v[:, l], (0, 2, 1, 3)).astype(BF16)
        at_l = _attn_lat(qr, kr, proj, ck, cv, sink, n_ctx, dec_batch, dec_seq)

        x1, xw, bkt = _outproj((hm_c, hm_l, at_c, at_l), w_out_b, l, xs, t, modl, cond_of_tile,
                               ln1_g[l].reshape(1, d), ln1_b[l].reshape(1, d), wr, br, alpha, tm)
        y = _moe(xw, bkt[0], wg, wu, wd, l, tm_e)
        last = l == depth - 1
        x_new = _final_ln(x1, y, modl, cond_of_tile, ln2_g[l].reshape(1, d), ln2_b[l].reshape(1, d), alpha, tm,
                          split_at=n_ctx_tiles if last else None)
        if last:
            y_ctx, y_lat = x_new
        else:
            xs = (x_new, x_new, n_ctx_tiles, n_ctx_tiles)

        ks.append(kv[:n_ctx, :A_KV_HEADS * A_HD].reshape(batch, seq, A_KV_HEADS, A_HD))
        vs.append(kv[:n_ctx, A_KV_HEADS * A_HD:].reshape(batch, seq, A_KV_HEADS, A_HD))
        cs.append(c_new.reshape(batch, 2, M_HEADS, M_DK, M_DV))
        ns.append(n_new.reshape(batch, 2, M_HEADS, M_DK))
        m4 = m_new[:, :, :4, 0].reshape(batch, N_PAIRS, 2, 2)
        ms.append(jnp.transpose(m4, (0, 2, 1, 3)).reshape(batch, 2, M_HEADS))

    y_prompt = y_ctx.reshape(batch, seq, d)
    y_sample = y_lat.reshape(dec_batch, dec_seq, d)
    return (y_prompt, y_sample, jnp.stack(ks, 1), jnp.stack(vs, 1), jnp.stack(cs, 1), jnp.stack(ns, 1),
            jnp.stack(ms, 1))
```

```python
import functools

import jax
import jax.numpy as jnp
import numpy as np
from jax import lax
from jax.experimental import pallas as pl
from jax.experimental.pallas import tpu as pltpu

F32 = jnp.float32
BF16 = jnp.bfloat16

M_HEADS = 8
M_DK = 64
M_DV = 128
IGATE_CAP = 15.0
MH_EPS = 1e-6
NEG_INIT = -1e30
A_HEADS = 8
A_KV_HEADS = 2
A_GROUP = A_HEADS // A_KV_HEADS
A_HD = 128
WINDOW = 128
GRID_W = 64
ROPE_BASE = 10000.0
N_EXPERTS = 16
N_GROUPS = 4
EXPERTS_PER_GROUP = N_EXPERTS // N_GROUPS
LN_EPS = 1e-5

LANES = 128
VMEM_LIMIT = 56 * 1024 * 1024

QM_OFF = 0
KM_OFF = M_HEADS * M_DK
VM_OFF = KM_OFF + M_HEADS * M_DK
OM_OFF = VM_OFF + M_HEADS * M_DV
QA_OFF = OM_OFF + M_HEADS * M_DV
KA_OFF = QA_OFF + A_HEADS * A_HD
VA_OFF = KA_OFF + A_KV_HEADS * A_HD
MAIN_COLS = VA_OFF + A_KV_HEADS * A_HD
GATE_SRC_OFF = OM_OFF + M_HEADS * M_DV
N_PAIRS = M_HEADS // 2
PAIR_ROWS = 2 * M_DK
CHUNK = 128
SUB_ROWS = 256
N_BUCKETS = N_GROUPS * 6
PAIR_LO = (0, 0, 0, 1, 1, 2)
PAIR_HI = (1, 2, 3, 2, 3, 3)


def _cparams(sem):
    return pltpu.CompilerParams(dimension_semantics=sem, vmem_limit_bytes=VMEM_LIMIT)


def _ln_plain(x):
    mu = jnp.mean(x, axis=-1, keepdims=True)
    xc = x - mu
    return xc * lax.rsqrt(jnp.mean(xc * xc, axis=-1, keepdims=True) + LN_EPS)


def _ada_kernel(c_ref, w_ref, b_ref, o_ref):
    c = c_ref[...]
    s = (c * jax.nn.sigmoid(c)).astype(BF16)
    o_ref[0] = jnp.dot(s, w_ref[0].astype(BF16), preferred_element_type=F32) + b_ref[0]


def _ada_mod(cond, w_ada, b_ada):
    depth, d, n = w_ada.shape
    rows = cond.shape[0]
    tn = 2048
    return pl.pallas_call(
        _ada_kernel,
        grid=(depth, n // tn),
        in_specs=[
            pl.BlockSpec((rows, d), lambda l, j: (0, 0)),
            pl.BlockSpec((1, d, tn), lambda l, j: (l, 0, j)),
            pl.BlockSpec((1, 1, tn), lambda l, j: (l, 0, j)),
        ],
        out_specs=pl.BlockSpec((1, rows, tn), lambda l, j: (l, 0, j)),
        out_shape=jax.ShapeDtypeStruct((depth, rows, n), F32),
        compiler_params=_cparams(("parallel", "parallel")),
        name="ada_mod",
    )(cond, w_ada, b_ada.reshape(depth, 1, n))


def _split_specs(tm, width, na, off_b):
    spec_a = pl.BlockSpec((tm, width), lambda i, *_: (jnp.minimum(i, na - 1), 0))
    spec_b = pl.BlockSpec((tm, width), lambda i, *_: (jnp.maximum(i - na, 0) + off_b, 0))
    return spec_a, spec_b


def _inproj_kernel(xa_ref, xb_ref, mod_ref, w_ref, wg_ref, bg_ref, proj_ref, gates_ref, kv_ref, h_scr, *, tn, na):
    is_a = pl.program_id(0) < na
    nj = MAIN_COLS // tn
    for s in range(xa_ref.shape[0] // SUB_ROWS):
        rows = slice(s * SUB_ROWS, (s + 1) * SUB_ROWS)
        x = jnp.where(is_a, xa_ref[rows, :], xb_ref[rows, :])
        hn = _ln_plain(x)
        h = hn * (1.0 + mod_ref[0, 1:2, :]) + mod_ref[0, 0:1, :]
        hb = h.astype(BF16)
        h_scr[rows, :] = hb
        gates_ref[rows, :] = jnp.dot(hb, wg_ref[...], preferred_element_type=F32) + bg_ref[...]
        for j in range(nj):
            cols = slice(j * tn, (j + 1) * tn)
            acc = jnp.dot(h_scr[rows, :], w_ref[0, :, cols], preferred_element_type=F32)
            proj_ref[rows, cols] = acc.astype(BF16)
            if j == nj - 1:
                kv_ref[rows, :] = acc


def _inproj(xs, t, modl, cond_of_tile, w_main, layer, w_gate, b_gate, tm):
    x_a, x_b, na, off_b = xs
    d = x_a.shape[1]
    tn = MAIN_COLS - KA_OFF
    assert MAIN_COLS % tn == 0
    spec_a, spec_b = _split_specs(tm, d, na, off_b)
    const = lambda i: (0, 0)
    resident = pl.Buffered(1)
    return pl.pallas_call(
        functools.partial(_inproj_kernel, tn=tn, na=na),
        grid=(t // tm,),
        in_specs=[
            spec_a,
            spec_b,
            pl.BlockSpec((1, 6, d), lambda i: (cond_of_tile(i), 0, 0)),
            pl.BlockSpec((1, d, MAIN_COLS), lambda i: (layer, 0, 0), pipeline_mode=resident),
            pl.BlockSpec((d, 2 * LANES), const, pipeline_mode=resident),
            pl.BlockSpec((1, 2 * LANES), const, pipeline_mode=resident),
        ],
        out_specs=[
            pl.BlockSpec((tm, MAIN_COLS), lambda i: (i, 0)),
            pl.BlockSpec((tm, 2 * LANES), lambda i: (i, 0)),
            pl.BlockSpec((tm, tn), lambda i: (i, 0)),
        ],
        out_shape=[
            jax.ShapeDtypeStruct((t, MAIN_COLS), BF16),
            jax.ShapeDtypeStruct((t, 2 * LANES), F32),
            jax.ShapeDtypeStruct((t, tn), F32),
        ],
        scratch_shapes=[pltpu.VMEM((tm, d), BF16)],
        compiler_params=_cparams(("parallel",)),
        name="inproj",
    )(x_a, x_b, modl, w_main, w_gate, b_gate)


def _gate_kernel(g_ref, gb_ref, gu_ref, gut_ref, *, tg):
    gi = g_ref[:, :LANES]
    gf = g_ref[:, LANES:]
    ig = IGATE_CAP * jnp.tanh(gi / IGATE_CAP)
    lf = jax.nn.log_sigmoid(gf)
    lane = lax.broadcasted_iota(jnp.int32, (1, LANES), 1)
    is_fwd = (lane % 8) < 2
    s_i = lax.broadcasted_iota(jnp.int32, (CHUNK, CHUNK), 0)
    r_i = lax.broadcasted_iota(jnp.int32, (CHUNK, CHUNK), 1)
    tri_lo = (r_i <= s_i).astype(F32)
    tri_hi = (r_i >= s_i).astype(F32)
    for c in range(tg // CHUNK):
        rows = slice(c * CHUNK, (c + 1) * CHUNK)
        lfc = lf[rows]
        pre = jnp.dot(tri_lo, lfc, preferred_element_type=F32, precision=lax.Precision.HIGHEST)
        suf = jnp.dot(tri_hi, lfc, preferred_element_type=F32, precision=lax.Precision.HIGHEST)
        b = jnp.where(is_fwd, pre, suf)
        u = ig[rows] - b
        gb_ref[rows, :] = b
        gu_ref[rows, :] = u
        gut_ref[:, rows] = u.T[: 8 * N_PAIRS]


def _gate_prep(gates, tg):
    t = gates.shape[0]
    return pl.pallas_call(
        functools.partial(_gate_kernel, tg=tg),
        grid=(t // tg,),
        in_specs=[pl.BlockSpec((tg, 2 * LANES), lambda i: (i, 0))],
        out_specs=[
            pl.BlockSpec((tg, LANES), lambda i: (i, 0)),
            pl.BlockSpec((tg, LANES), lambda i: (i, 0)),
            pl.BlockSpec((8 * N_PAIRS, tg), lambda i: (0, i)),
        ],
        out_shape=[
            jax.ShapeDtypeStruct((t, LANES), F32),
            jax.ShapeDtypeStruct((t, LANES), F32),
            jax.ShapeDtypeStruct((8 * N_PAIRS, t), F32),
        ],
        compiler_params=_cparams(("parallel",)),
        name="gate_prep",
    )(gates)


def _mlstm_kernel(*refs, seq, has_state, emit_state):
    it = iter(refs)
    q_ref, k_ref, v_ref, om_ref, gb_ref, gu_ref, gut_ref, gain_ref = (next(it) for _ in range(8))
    if has_state:
        c0_ref, n0_ref, m0_ref = next(it), next(it), next(it)
    out_ref = next(it)
    if emit_state:
        cout_ref, nout_ref, mout_ref = next(it), next(it), next(it)
    h_scr, cst_scr = next(it), next(it)

    b_id = pl.program_id(0)
    hp = pl.program_id(1)
    nc = seq // CHUNK
    L = CHUNK

    lane = lax.broadcasted_iota(jnp.int32, (1, LANES), 1)
    row128 = lax.broadcasted_iota(jnp.int32, (PAIR_ROWS, 1), 0)
    s_i = lax.broadcasted_iota(jnp.int32, (L, L), 0)
    r_i = lax.broadcasted_iota(jnp.int32, (L, L), 1)
    e0row = (lane == 0).astype(F32)
    e0blk = jnp.broadcast_to(e0row, (L, LANES)).astype(BF16)

    def pick_lane(x, j):
        return jnp.sum(jnp.where(lane == j, x, 0.0), axis=1, keepdims=True)

    m_init = []
    for d in range(2):
        for hh in range(2):
            if has_state:
                rowmask = (row128 // M_DK) == hh
                cst_scr[d * 2 + hh, :, :LANES] = jnp.where(rowmask, c0_ref[0, d, 0], 0.0)
                cst_scr[d * 2 + hh, :, LANES:] = jnp.where(rowmask, n0_ref[0, d, 0], 0.0) * e0row
                m0 = m0_ref[b_id * (2 * M_HEADS) + d * M_HEADS + hp * 2 + hh]
                m_init.append(jnp.full((1, 1), m0, F32))
            else:
                cst_scr[d * 2 + hh] = jnp.zeros((PAIR_ROWS, 2 * LANES), F32)
                m_init.append(jnp.full((1, 1), NEG_INIT, F32))

    def body(t, ms):
        new_ms = []
        for d in range(2):
            tri = (r_i <= s_i) if d == 0 else (r_i >= s_i)
            c = t if d == 0 else nc - 1 - t
            r0 = pl.multiple_of(c * L, L)
            q2 = q_ref[pl.ds(r0, L), :]
            k2 = k_ref[pl.ds(r0, L), :]
            gbc = gb_ref[pl.ds(r0, L), :]
            guc = gu_ref[pl.ds(r0, L), :]
            gend = gb_ref[pl.ds(r0 + (L - 1 if d == 0 else 0), 1), :]
            for hh in range(2):
                si = d * 2 + hh
                jl = hp * 8 + d * 2 + hh
                lm = (lane // M_DK) == hh
                qh = jnp.where(lm, q2, jnp.zeros_like(q2))
                kh = jnp.where(lm, k2, jnp.zeros_like(k2))
                vh = v_ref[pl.ds(r0, L), hh * LANES:(hh + 1) * LANES]
                vext = jnp.concatenate([vh, e0blk], axis=1)
                urow = gut_ref[d * 2 + hh:d * 2 + hh + 1, pl.ds(r0, L)]
                ucol = pick_lane(guc, jl)
                bcol = pick_lane(gbc, jl)
                g = pick_lane(gend, jl)
                m = ms[si]
                umat = jnp.where(tri, urow, -jnp.inf)
                cmu = jnp.max(umat, axis=1, keepdims=True)
                mm = jnp.maximum(m, cmu)
                w = jnp.exp(umat - mm)
                sqk = lax.dot_general(qh, kh, (((1,), (1,)), ((), ())), preferred_element_type=F32)
                p = (sqk * w).astype(BF16)
                intra = jnp.dot(p, vext, preferred_element_type=F32)
                cs = cst_scr[si]
                inter = jnp.dot(qh, cs.astype(BF16), preferred_element_type=F32)
                nd = intra + jnp.exp(m - mm) * inter
                num = nd[:, :LANES]
                den = nd[:, LANES:LANES + 1]
                hv = num / jnp.maximum(jnp.abs(den), jnp.exp(-bcol - mm))
                h_scr[d, pl.ds(r0, L), hh * LANES:(hh + 1) * LANES] = hv
                maxu = jnp.max(urow, axis=1, keepdims=True)
                m_new = g + jnp.maximum(m, maxu)
                wa = jnp.exp(g + ucol - m_new)
                wc = jnp.exp(g + m - m_new)
                kw = (kh.astype(F32) * wa).astype(BF16)
                upd = lax.dot_general(kw, vext, (((0,), (0,)), ((), ())), preferred_element_type=F32)
                cst_scr[si] = wc * cs + upd
                new_ms.append(m_new)
        return tuple(new_ms)

    ms_fin = lax.fori_loop(0, nc, body, tuple(m_init))

    def combine(c, carry):
        r0 = pl.multiple_of(c * L, L)
        for hh in range(2):
            cols = slice(hh * LANES, (hh + 1) * LANES)
            tot = h_scr[0, pl.ds(r0, L), cols] + h_scr[1, pl.ds(r0, L), cols]
            ms2 = jnp.mean(tot * tot, axis=1, keepdims=True)
            y = tot * lax.rsqrt(ms2 + MH_EPS) * gain_ref[:, cols]
            y = y * jax.nn.sigmoid(om_ref[pl.ds(r0, L), cols].astype(F32))
            out_ref[pl.ds(r0, L), cols] = y.astype(BF16)
        return carry

    lax.fori_loop(0, nc, combine, 0)

    if emit_state:
        for d in range(2):
            cout_ref[0, d, 0] = cst_scr[d * 2, :, :LANES] + cst_scr[d * 2 + 1, :, :LANES]
            nout_ref[0, d, 0] = cst_scr[d * 2, :, LANES:LANES + 1] + cst_scr[d * 2 + 1, :, LANES:LANES + 1]
            for hh in range(2):
                mout_ref[0, 0, d * 2 + hh:d * 2 + hh + 1, :] = jnp.broadcast_to(ms_fin[d * 2 + hh], (1, LANES))
        mout_ref[0, 0, 4:8, :] = jnp.zeros((4, LANES), F32)


def _mlstm(proj, gb, gu, gut, gain, row_off, nseq, seq, state0, emit_state):
    assert row_off % seq == 0
    rb = row_off // seq
    has_state = state0 is not None
    kernel = functools.partial(_mlstm_kernel, seq=seq, has_state=has_state, emit_state=emit_state)
    in_specs = [
        pl.BlockSpec((seq, PAIR_ROWS), lambda b, h: (rb + b, QM_OFF // PAIR_ROWS + h)),
        pl.BlockSpec((seq, PAIR_ROWS), lambda b, h: (rb + b, KM_OFF // PAIR_ROWS + h)),
        pl.BlockSpec((seq, 2 * M_DV), lambda b, h: (rb + b, VM_OFF // (2 * M_DV) + h)),
        pl.BlockSpec((seq, 2 * M_DV), lambda b, h: (rb + b, OM_OFF // (2 * M_DV) + h)),
        pl.BlockSpec((seq, LANES), lambda b, h: (rb + b, 0)),
        pl.BlockSpec((seq, LANES), lambda b, h: (rb + b, 0)),
        pl.BlockSpec((8, seq), lambda b, h: (h, rb + b)),
        pl.BlockSpec((1, 2 * M_DV), lambda b, h: (0, h)),
    ]
    args = [proj, proj, proj, proj, gb, gu, gut, gain]
    if has_state:
        c0, n0, m0 = state0
        in_specs += [
            pl.BlockSpec((1, 2, 1, PAIR_ROWS, M_DV), lambda b, h: (b, 0, h, 0, 0)),
            pl.BlockSpec((1, 2, 1, PAIR_ROWS, 1), lambda b, h: (b, 0, h, 0, 0)),
            pl.BlockSpec(memory_space=pltpu.SMEM),
        ]
        args += [c0, n0, m0]
    out_specs = [pl.BlockSpec((seq, 2 * M_DV), lambda b, h: (b, h))]
    out_shape = [jax.ShapeDtypeStruct((nseq * seq, M_HEADS * M_DV), BF16)]
    if emit_state:
        out_specs += [
            pl.BlockSpec((1, 2, 1, PAIR_ROWS, M_DV), lambda b, h: (b, 0, h, 0, 0)),
            pl.BlockSpec((1, 2, 1, PAIR_ROWS, 1), lambda b, h: (b, 0, h, 0, 0)),
            pl.BlockSpec((1, 1, 8, LANES), lambda b, h: (b, h, 0, 0)),
        ]
        out_shape += [
            jax.ShapeDtypeStruct((nseq, 2, N_PAIRS, PAIR_ROWS, M_DV), F32),
            jax.ShapeDtypeStruct((nseq, 2, N_PAIRS, PAIR_ROWS, 1), F32),
            jax.ShapeDtypeStruct((nseq, N_PAIRS, 8, LANES), F32),
        ]
    return pl.pallas_call(
        kernel,
        grid=(nseq, N_PAIRS),
        in_specs=in_specs,
        out_specs=out_specs,
        out_shape=out_shape,
        scratch_shapes=[pltpu.VMEM((2, seq, 2 * M_DV), F32), pltpu.VMEM((4, PAIR_ROWS, 2 * LANES), F32)],
        compiler_params=_cparams(("parallel", "parallel")),
        name="mlstm",
    )(*args)


def _attn_ctx_kernel(sink_ref, q_ref, k_ref, v_ref, o_ref, *, seq, nb):
    kvh = pl.program_id(1)
    scale = A_HD ** -0.5
    for s in range(nb):
        rows = slice(s * seq, (s + 1) * seq)
        k = k_ref[rows, :]
        v = v_ref[rows, :]
        for g in range(A_GROUP):
            cols = slice(g * A_HD, (g + 1) * A_HD)
            q = q_ref[rows, cols]
            sc = lax.dot_general(q, k, (((1,), (1,)), ((), ())), preferred_element_type=F32) * scale
            sk = sink_ref[kvh * A_GROUP + g]
            mx = jnp.maximum(jnp.max(sc, axis=1, keepdims=True), sk)
            p = jnp.exp(sc - mx)
            den = jnp.sum(p, axis=1, keepdims=True) + jnp.exp(sk - mx)
            o = jnp.dot(p.astype(BF16), v, preferred_element_type=F32) / den
            o_ref[rows, cols] = o.astype(BF16)


def _attn_ctx(proj, sink, nseq, seq, nb):
    gw = A_GROUP * A_HD
    return pl.pallas_call(
        functools.partial(_attn_ctx_kernel, seq=seq, nb=nb),
        grid=(nseq // nb, A_KV_HEADS),
        in_specs=[
            pl.BlockSpec(memory_space=pltpu.SMEM),
            pl.BlockSpec((nb * seq, gw), lambda b, h: (b, QA_OFF // gw + h)),
            pl.BlockSpec((nb * seq, A_HD), lambda b, h: (b, KA_OFF // A_HD + h)),
            pl.BlockSpec((nb * seq, A_HD), lambda b, h: (b, VA_OFF // A_HD + h)),
        ],
        out_specs=pl.BlockSpec((nb * seq, gw), lambda b, h: (b, h)),
        out_shape=jax.ShapeDtypeStruct((nseq * seq, A_HEADS * A_HD), BF16),
        compiler_params=_cparams(("parallel", "parallel")),
        name="attn_ctx",
    )(sink, proj, proj, proj)


def _rope_kernel(q_ref, k_ref, cos_ref, sa_ref, sb_ref, qo_ref, ko_ref):
    cos = cos_ref[...]
    sa = sa_ref[...]
    sb = sb_ref[...]

    def rot(x):
        return x * cos + pltpu.roll(x, LANES - A_HD // 4, 1) * sa + pltpu.roll(x, A_HD // 4, 1) * sb

    for h in range(A_HEADS):
        cols = slice(h * A_HD, (h + 1) * A_HD)
        qo_ref[:, cols] = rot(q_ref[:, cols].astype(F32)).astype(BF16)
    for h in range(A_KV_HEADS):
        cols = slice(h * A_HD, (h + 1) * A_HD)
        ko_ref[:, cols] = rot(k_ref[:, cols].astype(F32)).astype(BF16)


def _rope(proj, tables, row_off, nrows, seq, tr):
    cos, sa, sb = tables
    rb = row_off // tr
    nps = seq // tr
    qw = A_HEADS * A_HD
    kw = A_KV_HEADS * A_HD
    tab = pl.BlockSpec((tr, A_HD), lambda i: (i % nps, 0))
    return pl.pallas_call(
        _rope_kernel,
        grid=(nrows // tr,),
        in_specs=[
            pl.BlockSpec((tr, qw), lambda i: (rb + i, QA_OFF // qw)),
            pl.BlockSpec((tr, kw), lambda i: (rb + i, KA_OFF // kw)),
            tab, tab, tab,
        ],
        out_specs=[pl.BlockSpec((tr, qw), lambda i: (i, 0)), pl.BlockSpec((tr, kw), lambda i: (i, 0))],
        out_shape=[jax.ShapeDtypeStruct((nrows, qw), BF16), jax.ShapeDtypeStruct((nrows, kw), BF16)],
        compiler_params=_cparams(("parallel",)),
        name="rope",
    )(proj, proj, cos, sa, sb)


def _rope_tables(seq):
    half = A_HD // 2
    pos = np.arange(seq)
    row = (pos // GRID_W).astype(np.float32)
    col = (pos % GRID_W).astype(np.float32)
    inv = (ROPE_BASE ** (-np.arange(0, half, 2, dtype=np.float32) / half)).astype(np.float32)
    ang_r = row[:, None] * inv[None, :]
    ang_c = col[:, None] * inv[None, :]
    ang = np.concatenate([ang_r, ang_r, ang_c, ang_c], axis=1).astype(np.float32)
    cos = np.cos(ang).astype(np.float32)
    sin = np.sin(ang).astype(np.float32)
    first = (np.arange(A_HD) % half) < (half // 2)
    sa = np.where(first[None, :], -sin, 0.0).astype(np.float32)
    sb = np.where(first[None, :], 0.0, sin).astype(np.float32)
    return jnp.asarray(cos), jnp.asarray(sa), jnp.asarray(sb)


def _attn_lat_kernel(sink_ref, q_ref, kp_ref, kc_ref, kn_ref, vp_ref, vc_ref, vn_ref, ck_ref, cv_ref, o_ref,
                     *, nblk):
    i = pl.program_id(1)
    qb = WINDOW
    scale = A_HD ** -0.5
    r = lax.broadcasted_iota(jnp.int32, (A_GROUP * qb, 3 * qb), 0) % qb
    c = lax.broadcasted_iota(jnp.int32, (A_GROUP * qb, 3 * qb), 1)
    c_lo = jnp.where(i > 0, 0, qb)
    c_hi = jnp.where(i < nblk - 1, 3 * qb, 2 * qb)
    valid = (c >= r) & (c <= r + 2 * WINDOW) & (c >= c_lo) & (c < c_hi)
    hrow = lax.broadcasted_iota(jnp.int32, (A_GROUP * qb, 1), 0) // qb
    for kvh in range(A_KV_HEADS):
        kc = slice(kvh * A_HD, (kvh + 1) * A_HD)
        q0 = kvh * A_GROUP * A_HD
        q = jnp.concatenate([q_ref[:, q0 + g * A_HD:q0 + (g + 1) * A_HD] for g in range(A_GROUP)], axis=0)
        kwin = jnp.concatenate([kp_ref[:, kc], kc_ref[:, kc], kn_ref[:, kc]], axis=0)
        vwin = jnp.concatenate([vp_ref[:, kc], vc_ref[:, kc], vn_ref[:, kc]], axis=0)
        s_lat = lax.dot_general(q, kwin, (((1,), (1,)), ((), ())), preferred_element_type=F32) * scale
        s_ctx = lax.dot_general(q, ck_ref[0, kvh], (((1,), (1,)), ((), ())), preferred_element_type=F32) * scale
        s_lat = jnp.where(valid, s_lat, NEG_INIT)
        sk = jnp.zeros((A_GROUP * qb, 1), F32)
        for g in range(A_GROUP):
            sk = jnp.where(hrow == g, sink_ref[kvh * A_GROUP + g], sk)
        mx = jnp.maximum(jnp.maximum(jnp.max(s_lat, axis=1, keepdims=True),
                                     jnp.max(s_ctx, axis=1, keepdims=True)), sk)
        p_lat = jnp.exp(s_lat - mx)
        p_ctx = jnp.exp(s_ctx - mx)
        den = jnp.sum(p_lat, axis=1, keepdims=True) + jnp.sum(p_ctx, axis=1, keepdims=True) + jnp.exp(sk - mx)
        o = (jnp.dot(p_lat.astype(BF16), vwin, preferred_element_type=F32)
             + jnp.dot(p_ctx.astype(BF16), cv_ref[0, kvh], preferred_element_type=F32)) / den
        for g in range(A_GROUP):
            o_ref[:, q0 + g * A_HD:q0 + (g + 1) * A_HD] = o[g * qb:(g + 1) * qb].astype(BF16)


def _attn_lat(qr, kr, proj, ck, cv, sink, row_off, nseq, seq):
    qb = WINDOW
    nblk = seq // qb
    qw = A_HEADS * A_HD
    kw = A_KV_HEADS * A_HD
    rb = row_off // qb
    p_len = ck.shape[2]
    prev = lambda b, i: b * nblk + jnp.maximum(i - 1, 0)
    cur = lambda b, i: b * nblk + i
    nxt = lambda b, i: b * nblk + jnp.minimum(i + 1, nblk - 1)
    vcol = VA_OFF // kw
    return pl.pallas_call(
        functools.partial(_attn_lat_kernel, nblk=nblk),
        grid=(nseq, nblk),
        in_specs=[
            pl.BlockSpec(memory_space=pltpu.SMEM),
            pl.BlockSpec((qb, qw), lambda b, i: (cur(b, i), 0)),
            pl.BlockSpec((qb, kw), lambda b, i: (prev(b, i), 0)),
            pl.BlockSpec((qb, kw), lambda b, i: (cur(b, i), 0)),
            pl.BlockSpec((qb, kw), lambda b, i: (nxt(b, i), 0)),
            pl.BlockSpec((qb, kw), lambda b, i: (rb + prev(b, i), vcol)),
            pl.BlockSpec((qb, kw), lambda b, i: (rb + cur(b, i), vcol)),
            pl.BlockSpec((qb, kw), lambda b, i: (rb + nxt(b, i), vcol)),
            pl.BlockSpec((1, A_KV_HEADS, p_len, A_HD), lambda b, i: (b, 0, 0, 0)),
            pl.BlockSpec((1, A_KV_HEADS, p_len, A_HD), lambda b, i: (b, 0, 0, 0)),
        ],
        out_specs=pl.BlockSpec((qb, qw), lambda b, i: (cur(b, i), 0)),
        out_shape=jax.ShapeDtypeStruct((nseq * seq, qw), BF16),
        compiler_params=_cparams(("parallel", "parallel")),
        name="attn_lat",
    )(sink, qr, kr, kr, kr, proj, proj, proj, ck, cv)


def _outproj_kernel(*refs, alpha, na):
    tm = refs[0].shape[0]
    for s in range(tm // SUB_ROWS):
        _outproj_rows(slice(s * SUB_ROWS, (s + 1) * SUB_ROWS), *refs, alpha=alpha, na=na)


def _outproj_rows(rows, mmc_ref, mml_ref, mac_ref, mal_ref, w_ref, xa_ref, xb_ref, mod_ref, lng_ref, lnb_ref, wr_ref,
                  br_ref, x1_ref, xw_ref, bkt_ref, *, alpha, na):
    is_a = pl.program_id(0) < na
    d = xa_ref.shape[1]
    half = w_ref.shape[1] // 2
    mix_m = jnp.where(is_a, mmc_ref[rows, :], mml_ref[rows, :])
    mix_a = jnp.where(is_a, mac_ref[rows, :], mal_ref[rows, :])
    f = (jnp.dot(mix_m, w_ref[0, :half, :], preferred_element_type=F32)
         + jnp.dot(mix_a, w_ref[0, half:, :], preferred_element_type=F32))
    x = jnp.where(is_a, xa_ref[rows, :], xb_ref[rows, :])
    z = alpha * x + mod_ref[0, 2:3, :] * f
    x1 = _ln_plain(z) * lng_ref[...] + lnb_ref[...]
    x1_ref[rows, :] = x1
    h2 = _ln_plain(x1) * (1.0 + mod_ref[0, 4:5, :]) + mod_ref[0, 3:4, :]
    xw_ref[rows, :d] = h2
    logits = jnp.dot(h2.astype(BF16), wr_ref[...], preferred_element_type=F32)
    scores = jax.nn.sigmoid(logits)
    sel = scores + br_ref[...]
    sc_t = scores.T
    sel_t = sel.T
    sv = [sel_t[e:e + 1, :] for e in range(N_EXPERTS)]
    cv = [sc_t[e:e + 1, :] for e in range(N_EXPERTS)]
    gs = []
    for g in range(N_GROUPS):
        v = sv[4 * g:4 * g + 4]
        best = None
        for a, b in zip(PAIR_LO, PAIR_HI):
            ps = v[a] + v[b]
            best = ps if best is None else jnp.maximum(best, ps)
        gs.append(best)
    gmax = jnp.maximum(jnp.maximum(gs[0], gs[1]), jnp.maximum(gs[2], gs[3]))
    grp = jnp.full(gmax.shape, N_GROUPS - 1, jnp.int32)
    for g in range(N_GROUPS - 2, -1, -1):
        grp = jnp.where(gs[g] == gmax, g, grp)
    def pick(vals, k):
        out = vals[k]
        for g in range(1, N_GROUPS):
            out = jnp.where(grp == g, vals[4 * g + k], out)
        return out
    gv = [pick(sv, k) for k in range(EXPERTS_PER_GROUP)]
    gc = [pick(cv, k) for k in range(EXPERTS_PER_GROUP)]
    m1 = jnp.maximum(jnp.maximum(gv[0], gv[1]), jnp.maximum(gv[2], gv[3]))
    k0 = jnp.full(m1.shape, 3, jnp.int32)
    for k in range(2, -1, -1):
        k0 = jnp.where(gv[k] == m1, k, k0)
    gv2 = [jnp.where(k0 == k, -jnp.inf, gv[k]) for k in range(4)]
    m2 = jnp.maximum(jnp.maximum(gv2[0], gv2[1]), jnp.maximum(gv2[2], gv2[3]))
    k1 = jnp.full(m1.shape, 3, jnp.int32)
    for k in range(2, -1, -1):
        k1 = jnp.where(gv2[k] == m2, k, k1)
    lo = jnp.minimum(k0, k1)
    hi = jnp.maximum(k0, k1)
    pair = jnp.where(lo == 0, hi - 1, jnp.where(lo == 1, hi + 1, 5))
    bkt_ref[:, rows] = grp * 6 + pair
    s_lo = jnp.where(lo == 0, gc[0], jnp.where(lo == 1, gc[1], gc[2]))
    s_hi = jnp.where(hi == 1, gc[1], jnp.where(hi == 2, gc[2], gc[3]))
    tot = s_lo + s_hi
    wrow = lax.broadcasted_iota(jnp.int32, (LANES, tot.shape[1]), 0)
    w_t = jnp.where(wrow == 0, s_lo / tot, jnp.where(wrow == 1, s_hi / tot, 0.0))
    xw_ref[rows, d:] = w_t.T


def _outproj(mixes, w_out, layer, xs, t, modl, cond_of_tile, ln_g, ln_b, w_router, b_router, alpha, tm):
    x_a, x_b, na, off_b = xs
    d = x_a.shape[1]
    hw = mixes[0].shape[1]
    row = lambda i: (i, 0)
    const = lambda i: (0, 0)
    mix_a, mix_b = _split_specs(tm, hw, na, 0)
    x_sa, x_sb = _split_specs(tm, d, na, off_b)
    return pl.pallas_call(
        functools.partial(_outproj_kernel, alpha=alpha, na=na),
        grid=(t // tm,),
        in_specs=[
            mix_a, mix_b, mix_a, mix_b,
            pl.BlockSpec((1, 2 * hw, d), lambda i: (layer, 0, 0), pipeline_mode=pl.Buffered(1)),
            x_sa, x_sb,
            pl.BlockSpec((1, 6, d), lambda i: (cond_of_tile(i), 0, 0)),
            pl.BlockSpec((1, d), const),
            pl.BlockSpec((1, d), const),
            pl.BlockSpec((d, LANES), const),
            pl.BlockSpec((1, LANES), const),
        ],
        out_specs=[
            pl.BlockSpec((tm, d), row),
            pl.BlockSpec((tm, d + LANES), row),
            pl.BlockSpec((1, tm), lambda i: (0, i)),
        ],
        out_shape=[
            jax.ShapeDtypeStruct((t, d), F32),
            jax.ShapeDtypeStruct((t, d + LANES), F32),
            jax.ShapeDtypeStruct((1, t), jnp.int32),
        ],
        compiler_params=_cparams(("parallel",)),
        name="outproj_route",
    )(mixes[0], mixes[1], mixes[2], mixes[3], w_out, x_a, x_b, modl, ln_g, ln_b, w_router, b_router)


def _moe_kernel(te0_ref, te1_ref, tnv_ref, pg_ref, pgn_ref, ps_ref, x_hbm, wg0_ref, wg1_ref, wu0_ref, wu1_ref,
                wd0_ref, wd1_ref, y_hbm, xbuf, ybuf, gsem, ssem, *, tm_e, d, n_tiles, t):
    g = pl.program_id(0)
    nv = tnv_ref[g]
    slot = lax.rem(g, 2)
    oslot = 1 - slot

    def start_rows(copy_of_row, idx_ref, priority_of_row):
        for r in range(tm_e):
            copy_of_row(r, idx_ref[0, 0, r]).start(priority=priority_of_row(r))

    def start_gather(sl, idx_ref):
        start_rows(lambda r, idx: pltpu.make_async_copy(x_hbm.at[pl.ds(idx, 1)], xbuf.at[sl, pl.ds(r, 1)],
                                                        gsem.at[sl]), idx_ref, lambda r: 1)

    def start_scatter(sl, idx_ref):
        start_rows(lambda r, idx: pltpu.make_async_copy(ybuf.at[sl, pl.ds(r, 1)], y_hbm.at[pl.ds(idx, 1)],
                                                        ssem.at[sl]), idx_ref, lambda r: r % 2)

    def wait_gather(sl):
        pltpu.make_async_copy(x_hbm.at[pl.ds(0, tm_e)], xbuf.at[sl], gsem.at[sl]).wait()

    def wait_scatter(sl):
        pltpu.make_async_copy(ybuf.at[sl], y_hbm.at[pl.ds(0, tm_e)], ssem.at[sl]).wait()

    @pl.when(g == 0)
    def _():
        ybuf[1] = jnp.zeros((tm_e, d), F32)
        for k in range(2):
            pltpu.make_async_copy(ybuf.at[1], y_hbm.at[pl.ds(t + k * tm_e, tm_e)], ssem.at[1]).start()
        for k in range(2):
            wait_scatter(1)
        start_gather(0, pg_ref)

    @pl.when(nv > 0)
    def _():
        g_next = jnp.minimum(g + 1, n_tiles - 1)
        has_next = jnp.logical_and(g + 1 < n_tiles, tnv_ref[g_next] > 0)

        wait_gather(slot)
        start_gather(oslot, pgn_ref)
        xb = xbuf[slot, :, :d].astype(BF16)
        y = None
        for k, (wg_ref, wu_ref, wd_ref) in enumerate(((wg0_ref, wu0_ref, wd0_ref), (wg1_ref, wu1_ref, wd1_ref))):
            gt = jnp.dot(xb, wg_ref[0, 0], preferred_element_type=F32)
            up = jnp.dot(xb, wu_ref[0, 0], preferred_element_type=F32)
            a = (gt * jax.nn.sigmoid(gt)) * up
            ye = jnp.dot(a.astype(BF16), wd_ref[0, 0], preferred_element_type=F32)
            ye = xbuf[slot, :, d + k:d + k + 1] * ye
            y = ye if y is None else y + ye
        ybuf[slot] = y
        start_scatter(slot, ps_ref)

        @pl.when(g > 0)
        def _():
            wait_scatter(oslot)

        @pl.when(jnp.logical_not(has_next))
        def _():
            wait_scatter(slot)
            wait_gather(oslot)


def _moe(xw, bkt, wg, wu, wd, layer, tm_e):
    t, dw = xw.shape
    d = dw - LANES
    d_exp = wd.shape[2]
    n_tiles = t // tm_e + N_BUCKETS
    n_rows = n_tiles * tm_e
    ids = jnp.arange(N_BUCKETS, dtype=jnp.int32)
    onehot = (bkt[:, None] == ids[None, :]).astype(jnp.int32)
    counts = jnp.sum(onehot, axis=0)
    tiles_b = (counts + tm_e - 1) // tm_e
    tile_end = jnp.cumsum(tiles_b)
    row_start = (tile_end - tiles_b) * tm_e
    blk = tm_e
    oh3 = onehot.astype(F32).reshape(t // blk, blk, N_BUCKETS)
    tri = (jnp.arange(blk)[:, None] >= jnp.arange(blk)[None, :]).astype(F32)
    local = jnp.einsum("ij,bjk->bik", tri, oh3)
    before = jnp.cumsum(local[:, -1, :], axis=0) - local[:, -1, :]
    running = (local + before[:, None, :]).astype(jnp.int32).reshape(t, N_BUCKETS)
    rank = jnp.take_along_axis(running, bkt[:, None], axis=1)[:, 0] - 1
    pos = row_start[bkt] + rank
    tile_ids = jnp.arange(n_tiles, dtype=jnp.int32)
    used = tile_end[-1]
    tb = jnp.sum((tile_end[None, :] <= jnp.minimum(tile_ids, used - 1)[:, None]).astype(jnp.int32), axis=1)
    tile_in_b = tile_ids - (tile_end - tiles_b)[tb]
    tnv = jnp.where(tile_ids < used, jnp.clip(counts[tb] - tile_in_b * tm_e, 0, tm_e), 0).astype(jnp.int32)
    lo = jnp.asarray(PAIR_LO, jnp.int32)
    hi = jnp.asarray(PAIR_HI, jnp.int32)
    te0 = (tb // 6) * EXPERTS_PER_GROUP + lo[tb % 6]
    te1 = (tb // 6) * EXPERTS_PER_GROUP + hi[tb % 6]
    tok = jnp.arange(t, dtype=jnp.int32)
    slot_row = jnp.arange(n_rows, dtype=jnp.int32)
    dump = t + ((slot_row // tm_e) % 2) * tm_e + slot_row % tm_e
    perm_s = dump.at[pos].set(tok)
    perm_g = jnp.where(perm_s < t, perm_s, 0)
    pg3 = perm_g.reshape(n_tiles, 1, tm_e)
    ps3 = perm_s.reshape(n_tiles, 1, tm_e)

    grid_spec = pltpu.PrefetchScalarGridSpec(
        num_scalar_prefetch=3,
        grid=(n_tiles,),
        in_specs=[
            pl.BlockSpec((1, 1, tm_e), lambda g, *_: (g, 0, 0), memory_space=pltpu.SMEM),
            pl.BlockSpec((1, 1, tm_e), lambda g, *_: (jnp.minimum(g + 1, n_tiles - 1), 0, 0),
                         memory_space=pltpu.SMEM),
            pl.BlockSpec((1, 1, tm_e), lambda g, *_: (g, 0, 0), memory_space=pltpu.SMEM),
            pl.BlockSpec(memory_space=pl.ANY),
            pl.BlockSpec((1, 1, d, d_exp), lambda g, e0, e1, tv: (layer, e0[g], 0, 0)),
            pl.BlockSpec((1, 1, d, d_exp), lambda g, e0, e1, tv: (layer, e1[g], 0, 0)),
            pl.BlockSpec((1, 1, d, d_exp), lambda g, e0, e1, tv: (layer, e0[g], 0, 0)),
            pl.BlockSpec((1, 1, d, d_exp), lambda g, e0, e1, tv: (layer, e1[g], 0, 0)),
            pl.BlockSpec((1, 1, d_exp, d), lambda g, e0, e1, tv: (layer, e0[g], 0, 0)),
            pl.BlockSpec((1, 1, d_exp, d), lambda g, e0, e1, tv: (layer, e1[g], 0, 0)),
        ],
        out_specs=pl.BlockSpec(memory_space=pl.ANY),
        scratch_shapes=[
            pltpu.VMEM((2, tm_e, dw), F32),
            pltpu.VMEM((2, tm_e, d), F32),
            pltpu.SemaphoreType.DMA((2,)),
            pltpu.SemaphoreType.DMA((2,)),
        ],
    )
    y = pl.pallas_call(
        functools.partial(_moe_kernel, tm_e=tm_e, d=d, n_tiles=n_tiles, t=t),
        grid_spec=grid_spec,
        out_shape=jax.ShapeDtypeStruct((t + 2 * tm_e, d), F32),
        compiler_params=_cparams(("arbitrary",)),
        name="moe_experts",
    )(te0, te1, tnv, pg3, pg3, ps3, xw, wg, wg, wu, wu, wd, wd)
    return y


def _final_kernel(x1_ref, y_ref, mod_ref, g_ref, b_ref, *o_refs, alpha, na):
    z = alpha * x1_ref[...] + mod_ref[0, 5:6, :] * y_ref[...]
    out = _ln_plain(z) * g_ref[...] + b_ref[...]
    if len(o_refs) == 1:
        o_refs[0][...] = out
    else:
        @pl.when(pl.program_id(0) < na)
        def _():
            o_refs[0][...] = out

        @pl.when(pl.program_id(0) >= na)
        def _():
            o_refs[1][...] = out


def _final_ln(x1, y, modl, cond_of_tile, ln_g, ln_b, alpha, tm, split_at=None):
    t, d = x1.shape
    row = lambda i: (i, 0)
    const = lambda i: (0, 0)
    if split_at is None:
        out_specs = pl.BlockSpec((tm, d), row)
        out_shape = jax.ShapeDtypeStruct((t, d), F32)
    else:
        out_specs = list(_split_specs(tm, d, split_at, 0))
        out_shape = [jax.ShapeDtypeStruct((split_at * tm, d), F32), jax.ShapeDtypeStruct((t - split_at * tm, d), F32)]
    return pl.pallas_call(
        functools.partial(_final_kernel, alpha=alpha, na=split_at),
        grid=(t // tm,),
        in_specs=[
            pl.BlockSpec((tm, d), row),
            pl.BlockSpec((tm, d), row),
            pl.BlockSpec((1, 6, d), lambda i: (cond_of_tile(i), 0, 0)),
            pl.BlockSpec((1, d), const),
            pl.BlockSpec((1, d), const),
        ],
        out_specs=out_specs,
        out_shape=out_shape,
        compiler_params=_cparams(("arbitrary",)),
        name="final_ln",
    )(x1, y, modl, ln_g, ln_b)


def _gate_columns():
    src_i = np.zeros((8 * N_PAIRS,), np.int32)
    src_f = np.zeros((8 * N_PAIRS,), np.int32)
    for hp in range(N_PAIRS):
        for dd in range(2):
            for hh in range(2):
                j = hp * 8 + dd * 2 + hh
                head = hp * 2 + hh
                src_i[j] = dd * M_HEADS + head
                src_f[j] = (2 + dd) * M_HEADS + head
    return src_i, src_f


def kernel(x_prompt, x_sample, cache_k, cache_v, state_C, state_n, state_m, c, c_ctx, w_ada, b_ada, w_in, b_gate,
           mh_norm_g, attn_sink, w_out, ln1_g, ln1_b, ln2_g, ln2_b, w_router, b_router, w_exp_gate, w_exp_up,
           w_exp_down):
    batch, seq, d = x_prompt.shape
    dec_batch, dec_seq, _ = x_sample.shape
    depth = w_in.shape[0]
    n_ctx = batch * seq
    n_lat = dec_batch * dec_seq
    t = n_ctx + n_lat
    alpha = (2 * depth) ** 0.25
    tm = 512
    tm_e = 256
    assert n_ctx % tm == 0 and dec_seq % tm == 0 and seq % CHUNK == 0 and dec_seq % CHUNK == 0
    assert n_ctx % dec_seq == 0 and n_ctx % WINDOW == 0 and t % tm_e == 0

    n_ctx_tiles = n_ctx // tm
    tiles_per_seq = dec_seq // tm

    def cond_of_tile(i):
        return jnp.where(i < n_ctx_tiles, 0, 1 + (i - n_ctx_tiles) // tiles_per_seq)

    n_cond = 1 + dec_batch
    cond_rows = -(-n_cond // 8) * 8
    cond = jnp.concatenate([c_ctx[None, :], c, jnp.zeros((cond_rows - n_cond, d), F32)], axis=0)
    mod = _ada_mod(cond, w_ada, b_ada).reshape(depth, cond_rows, 6, d)

    xs = (x_prompt.reshape(n_ctx, d), x_sample.reshape(n_lat, d), n_ctx_tiles, 0)

    src_i, src_f = _gate_columns()
    rope_tab = _rope_tables(dec_seq)
    wr = jnp.pad(w_router, ((0, 0), (0, LANES - N_EXPERTS))).astype(BF16)
    br = jnp.pad(b_router, (0, LANES - N_EXPERTS)).reshape(1, LANES)

    kscale = M_DK ** -0.5
    w_main = jnp.concatenate(
        [w_in[:, :, :KM_OFF], w_in[:, :, KM_OFF:VM_OFF] * kscale, w_in[:, :, VM_OFF:GATE_SRC_OFF],
         w_in[:, :, GATE_SRC_OFF + 4 * M_HEADS:]], axis=2).astype(BF16)
    w_out_b = w_out.astype(BF16)
    wg = w_exp_gate.astype(BF16)
    wu = w_exp_up.astype(BF16)
    wd = w_exp_down.astype(BF16)

    ks, vs, cs, ns, ms = [], [], [], [], []
    for l in range(depth):
        wgt = w_in[l][:, GATE_SRC_OFF:GATE_SRC_OFF + 4 * M_HEADS]
        zpad = jnp.zeros((d, LANES - 8 * N_PAIRS), F32)
        w_gate = jnp.concatenate([wgt[:, src_i], zpad, wgt[:, src_f], zpad], axis=1).astype(BF16)
        bpad = jnp.zeros((LANES - 8 * N_PAIRS,), F32)
        bg = jnp.concatenate([b_gate[l][src_i], bpad, b_gate[l][src_f], bpad]).reshape(1, 2 * LANES)
        modl = mod[l]

        proj, gates, kv = _inproj(xs, t, modl, cond_of_tile, w_main, l, w_gate, bg, tm)
        gb, gu, gut = _gate_prep(gates, 1024)

        gain = mh_norm_g[l].reshape(1, M_HEADS * M_DV)
        sink = attn_sink[l]
        hm_c, c_new, n_new, m_new = _mlstm(proj, gb, gu, gut, gain, 0, batch, seq, None, True)
        at_c = _attn_ctx(proj, sink, batch, seq, 4)
        c0 = state_C[:, l].reshape(dec_batch, 2, N_PAIRS, PAIR_ROWS, M_DV)
        n0 = state_n[:, l].reshape(dec_batch, 2, N_PAIRS, PAIR_ROWS, 1)
        m0 = state_m[:, l].reshape(dec_batch * 2 * M_HEADS)
        (hm_l,) = _mlstm(proj, gb, gu, gut, gain, n_ctx, dec_batch, dec_seq, (c0, n0, m0), False)
        qr, kr = _rope(proj, rope_tab, n_ctx, n_lat, dec_seq, 512)
        ck = jnp.transpose(cache_k[:, l], (0, 2, 1, 3)).astype(BF16)
        cv = jnp.transpose(cache_v[:, l], (0, 2, 1, 3)).astype(BF16)
        at_l = _attn_lat(qr, kr, proj, ck, cv, sink, n_ctx, dec_batch, dec_seq)

        x1, xw, bkt = _outproj((hm_c, hm_l, at_c, at_l), w_out_b, l, xs, t, modl, cond_of_tile,
                               ln1_g[l].reshape(1, d), ln1_b[l].reshape(1, d), wr, br, alpha, tm)
        y = _moe(xw, bkt[0], wg, wu, wd, l, tm_e)
        last = l == depth - 1
        x_new = _final_ln(x1, y, modl, cond_of_tile, ln2_g[l].reshape(1, d), ln2_b[l].reshape(1, d), alpha, tm,
                          split_at=n_ctx_tiles if last else None)
        if last:
            y_ctx, y_lat = x_new
        else:
            xs = (x_new, x_new, n_ctx_tiles, n_ctx_tiles)

        ks.append(kv[:n_ctx, :A_KV_HEADS * A_HD].reshape(batch, seq, A_KV_HEADS, A_HD))
        vs.append(kv[:n_ctx, A_KV_HEADS * A_HD:].reshape(batch, seq, A_KV_HEADS, A_HD))
        cs.append(c_new.reshape(batch, 2, M_HEADS, M_DK, M_DV))
        ns.append(n_new.reshape(batch, 2, M_HEADS, M_DK))
        m4 = m_new[:, :, :4, 0].reshape(batch, N_PAIRS, 2, 2)
        ms.append(jnp.transpose(m4, (0, 2, 1, 3)).reshape(batch, 2, M_HEADS))

    y_prompt = y_ctx.reshape(batch, seq, d)
    y_sample = y_lat.reshape(dec_batch, dec_seq, d)
    return (y_prompt, y_sample, jnp.stack(ks, 1), jnp.stack(vs, 1), jnp.stack(cs, 1), jnp.stack(ns, 1),
            jnp.stack(ms, 1))
```
